```python
import jax, jax.numpy as jnp
from jax import lax
import numpy as np

D_MODEL = 1024
BATCH = 16
SEQ = 2048
DEPTH = 1
DEC_BATCH = 16
DEC_SEQ = 64
PAST_LEN = 1024

CHUNK = 64
D_MIX = D_MODEL
D_POOL = D_MIX // 2
D_CONV = D_MIX - D_POOL
POOL_WINDOWS = (2, 4, 8, 16)
N_POOL_GROUPS = len(POOL_WINDOWS)
POOL_GROUP_DIM = D_POOL // N_POOL_GROUPS
POOL_STATE = max(POOL_WINDOWS) - 1
CONV_WIDTH = 31
CONV_STATE = CONV_WIDTH - 1
D_IN = D_POOL + 2 * D_CONV
N_EXPERT_GROUPS = 4
EXPERTS_PER_GROUP = 8
N_EXPERTS = N_EXPERT_GROUPS * EXPERTS_PER_GROUP
TOP_K = 2
D_FF_EXPERT = D_MODEL // 2
MOE_BLOCK = 128
EPS = 1e-6

kernel_name = "hybrid_pool_conformer_hmoe_stream_step"


def rms_norm(x, g):
    xf = x.astype(jnp.float32)
    y = xf * lax.rsqrt(jnp.mean(xf * xf, axis=-1, keepdims=True) + EPS)
    return (y * g.astype(jnp.float32)).astype(x.dtype)


def layer_norm(x, g, b):
    xf = x.astype(jnp.float32)
    mu = jnp.mean(xf, axis=-1, keepdims=True)
    var = jnp.mean(jnp.square(xf - mu), axis=-1, keepdims=True)
    y = (xf - mu) * lax.rsqrt(var + EPS)
    return (y * g.astype(jnp.float32) + b.astype(jnp.float32)).astype(x.dtype)


def pool_mix(u, prefix, pos0, pool_lin, pool_scale):
    B, T, _ = u.shape
    ext = jnp.concatenate([prefix.astype(u.dtype), u], axis=1)
    extf = ext.astype(jnp.float32)
    cs = jnp.concatenate([jnp.zeros((B, 1, D_POOL), jnp.float32), jnp.cumsum(extf, axis=1)], axis=1)
    P = POOL_STATE
    end = cs[:, P + 1:P + 1 + T]
    pos = pos0 + jnp.arange(T)
    means = []
    for g, w in enumerate(POOL_WINDOWS):
        sl = slice(g * POOL_GROUP_DIM, (g + 1) * POOL_GROUP_DIM)
        s = end[..., sl] - cs[:, P + 1 - w:P + 1 - w + T, sl]
        cnt = jnp.minimum(pos + 1, w).astype(jnp.float32)
        means.append(s / cnt[None, :, None])
    d = jnp.concatenate(means, axis=-1) - extf[:, P:]
    d = d.astype(u.dtype).reshape(B, T, N_POOL_GROUPS, POOL_GROUP_DIM)
    y = jnp.einsum('btgc,gcd->btgd', d, pool_lin).reshape(B, T, D_POOL) * pool_scale
    return y, ext[:, -POOL_STATE:]


def conv_mix(a, b, prefix, conv_dw, conv_dw_b, ln_g, ln_b):
    v = a * jax.nn.sigmoid(b)
    ext = jnp.concatenate([prefix.astype(v.dtype), v], axis=1)
    h = lax.conv_general_dilated(ext, conv_dw[:, None, :].astype(ext.dtype), window_strides=(1,),
                                 padding='VALID', dimension_numbers=('NWC', 'WIO', 'NWC'),
                                 feature_group_count=D_CONV) + conv_dw_b
    h = layer_norm(h, ln_g, ln_b)
    return h * jax.nn.sigmoid(h), ext[:, -CONV_STATE:]


def hier_moe(h, w_rg, b_rg, w_re, b_re, w_eg, w_eu, w_ed):
    N, D = h.shape
    pg = jax.nn.softmax((h @ w_rg).astype(jnp.float32) + b_rg.astype(jnp.float32), axis=-1)
    p_grp, g_sel = lax.top_k(pg, 1)
    le = jnp.einsum('nd,gde->nge', h, w_re).astype(jnp.float32) + b_re.astype(jnp.float32)
    le = jnp.take_along_axis(le, g_sel[:, :, None], axis=1)[:, 0]
    pe = jax.nn.softmax(le, axis=-1)
    top_p, top_i = lax.top_k(pe, TOP_K)
    gates = p_grp * (top_p / jnp.sum(top_p, axis=-1, keepdims=True))
    eid = g_sel * EXPERTS_PER_GROUP + top_i
    M = N * TOP_K
    flat_e = eid.reshape(-1).astype(jnp.int32)
    flat_tok = jnp.repeat(jnp.arange(N, dtype=jnp.int32), TOP_K)
    flat_gate = gates.reshape(-1)
    order = jnp.argsort(flat_e)
    se = flat_e[order]
    counts = jnp.bincount(flat_e, length=N_EXPERTS)
    padded = (counts + MOE_BLOCK - 1) // MOE_BLOCK * MOE_BLOCK
    pad_end = jnp.cumsum(padded)
    pad_start = pad_end - padded
    start = jnp.cumsum(counts) - counts
    dest = pad_start[se] + jnp.arange(M) - start[se]
    n_blocks = -(-(M + N_EXPERTS * (MOE_BLOCK - 1)) // MOE_BLOCK)
    slots = n_blocks * MOE_BLOCK
    buf_tok = jnp.full((slots,), N, jnp.int32).at[dest].set(flat_tok[order])
    buf_gate = jnp.zeros((slots,), jnp.float32).at[dest].set(flat_gate[order])
    block_e = jnp.minimum(jnp.searchsorted(pad_end, jnp.arange(n_blocks) * MOE_BLOCK, side='right'),
                          N_EXPERTS - 1)
    hp = jnp.concatenate([h, jnp.zeros((1, D), h.dtype)], axis=0)
    xb = hp[buf_tok].reshape(n_blocks, MOE_BLOCK, D)

    def expert(args):
        xblk, e = args
        return (jax.nn.silu(xblk @ w_eg[e]) * (xblk @ w_eu[e])) @ w_ed[e]

    yb = lax.map(expert, (xb, block_e)).reshape(slots, D)
    y = jnp.zeros((N + 1, D), jnp.float32).at[buf_tok].add(yb.astype(jnp.float32) * buf_gate[:, None])
    return y[:N].astype(h.dtype)


def trunk_layer(x, pool_prefix, conv_prefix, pos0, norm1_g, w_in, pool_lin, pool_scale, conv_dw,
                conv_dw_b, conv_ln_g, conv_ln_b, w_out, b_out, norm2_g, w_rg, b_rg, w_re, b_re,
                w_eg, w_eu, w_ed):
    B, T, D = x.shape
    z = rms_norm(x, norm1_g) @ w_in
    u_pool = z[..., :D_POOL]
    a = z[..., D_POOL:D_POOL + D_CONV]
    gate = z[..., D_POOL + D_CONV:]
    y_pool, s_pool = pool_mix(u_pool, pool_prefix, pos0, pool_lin, pool_scale)
    y_conv, s_conv = conv_mix(a, gate, conv_prefix, conv_dw, conv_dw_b, conv_ln_g, conv_ln_b)
    x = x + jnp.concatenate([y_pool, y_conv], axis=-1) @ w_out + b_out
    hm = rms_norm(x, norm2_g).reshape(B * T, D)
    x = x + hier_moe(hm, w_rg, b_rg, w_re, b_re, w_eg, w_eu, w_ed).reshape(B, T, D)
    return x, s_pool, s_conv


def setup_inputs(seed: int = 0) -> dict:
    key = jax.random.key(seed)
    ks = jax.random.split(key, 24)
    f32 = jnp.float32
    nrm = lambda k, shp, s: jax.random.normal(k, shp, f32) * s
    L = DEPTH
    return {
        "x_prompt": nrm(ks[0], (BATCH, SEQ, D_MODEL), 1.0),
        "x_sample": nrm(ks[1], (DEC_BATCH, DEC_SEQ, D_MODEL), 1.0),
        "state_pool": nrm(ks[2], (L, DEC_BATCH, POOL_STATE, D_POOL), 1.0),
        "state_conv": nrm(ks[3], (L, DEC_BATCH, CONV_STATE, D_CONV), 0.5),
        "norm1_g": 1.0 + nrm(ks[4], (L, D_MODEL), 0.02),
        "w_in": nrm(ks[5], (L, D_MODEL, D_IN), D_MODEL ** -0.5),
        "pool_lin": nrm(ks[6], (L, N_POOL_GROUPS, POOL_GROUP_DIM, POOL_GROUP_DIM), POOL_GROUP_DIM ** -0.5),
        "pool_scale": 1.0 + nrm(ks[7], (L, D_POOL), 0.02),
        "conv_dw": nrm(ks[8], (L, CONV_WIDTH, D_CONV), CONV_WIDTH ** -0.5),
        "conv_dw_b": nrm(ks[9], (L, D_CONV), 0.02),
        "conv_ln_g": 1.0 + nrm(ks[10], (L, D_CONV), 0.02),
        "conv_ln_b": nrm(ks[11], (L, D_CONV), 0.02),
        "w_out": nrm(ks[12], (L, D_MIX, D_MODEL), D_MIX ** -0.5),
        "b_out": nrm(ks[13], (L, D_MODEL), 0.02),
        "norm2_g": 1.0 + nrm(ks[14], (L, D_MODEL), 0.02),
        "w_rg": nrm(ks[15], (L, D_MODEL, N_EXPERT_GROUPS), D_MODEL ** -0.5),
        "b_rg": nrm(ks[16], (L, N_EXPERT_GROUPS), 0.01),
        "w_re": nrm(ks[17], (L, N_EXPERT_GROUPS, D_MODEL, EXPERTS_PER_GROUP), D_MODEL ** -0.5),
        "b_re": nrm(ks[18], (L, N_EXPERT_GROUPS, EXPERTS_PER_GROUP), 0.01),
        "w_eg": nrm(ks[19], (L, N_EXPERTS, D_MODEL, D_FF_EXPERT), D_MODEL ** -0.5),
        "w_eu": nrm(ks[20], (L, N_EXPERTS, D_MODEL, D_FF_EXPERT), D_MODEL ** -0.5),
        "w_ed": nrm(ks[21], (L, N_EXPERTS, D_FF_EXPERT, D_MODEL), D_FF_EXPERT ** -0.5),
        "norm_f_g": 1.0 + nrm(ks[22], (D_MODEL,), 0.02),
    }


def reference(x_prompt, x_sample, state_pool, state_conv, norm1_g, w_in, pool_lin, pool_scale,
              conv_dw, conv_dw_b, conv_ln_g, conv_ln_b, w_out, b_out, norm2_g, w_rg, b_rg, w_re,
              b_re, w_eg, w_eu, w_ed, norm_f_g):
    hp, hs = x_prompt, x_sample
    B = x_prompt.shape[0]
    pool_p, conv_p, pool_s, conv_s = [], [], [], []
    for l in range(DEPTH):
        params = (norm1_g[l], w_in[l], pool_lin[l], pool_scale[l], conv_dw[l], conv_dw_b[l],
                  conv_ln_g[l], conv_ln_b[l], w_out[l], b_out[l], norm2_g[l], w_rg[l], b_rg[l],
                  w_re[l], b_re[l], w_eg[l], w_eu[l], w_ed[l])
        zp_pool = jnp.zeros((B, POOL_STATE, D_POOL), hp.dtype)
        zp_conv = jnp.zeros((B, CONV_STATE, D_CONV), hp.dtype)
        hp, sp1, sp2 = trunk_layer(hp, zp_pool, zp_conv, 0, *params)
        hs, ss1, ss2 = trunk_layer(hs, state_pool[l], state_conv[l], PAST_LEN, *params)
        pool_p.append(sp1); conv_p.append(sp2); pool_s.append(ss1); conv_s.append(ss2)
    y_prompt = rms_norm(hp, norm_f_g)
    y_sample = rms_norm(hs, norm_f_g)
    new_state_pool_prompt = jnp.stack(pool_p)
    new_state_conv_prompt = jnp.stack(conv_p)
    new_state_pool_sample = jnp.stack(pool_s)
    new_state_conv_sample = jnp.stack(conv_s)
    return (y_prompt, y_sample, new_state_pool_prompt, new_state_conv_prompt, new_state_pool_sample, new_state_conv_sample)
```

```python
import functools

import jax
import jax.numpy as jnp
from jax import lax
from jax.experimental import pallas as pl
from jax.experimental.pallas import tpu as pltpu

F32 = jnp.float32
BF16 = jnp.bfloat16
U32 = jnp.uint32
I32 = jnp.int32

D_MODEL = 1024
D_POOL = 512
D_CONV = 512
D_IN = D_POOL + 2 * D_CONV
POOL_WINDOWS = (2, 4, 8, 16)
POOL_GROUP_DIM = D_POOL // len(POOL_WINDOWS)
POOL_STATE = max(POOL_WINDOWS) - 1
CONV_WIDTH = 31
CONV_STATE = CONV_WIDTH - 1
N_EXPERT_GROUPS = 4
EXPERTS_PER_GROUP = 8
N_EXPERTS = N_EXPERT_GROUPS * EXPERTS_PER_GROUP
TOP_K = 2
D_FF = D_MODEL // 2
EPS = 1e-6

SUBLANES = 8
LANES = 128
POOL_PAD = 16
CONV_PAD = 32
ROUTER_ROWS = LANES
EXPERT_ROW0 = SUBLANES
CONV_CHUNK = 64
HM_WORDS = D_MODEL // 2

PROMPT_TILE = 512
TOKEN_TILE = 512
EXPERT_TILE = 256
VMEM_LIMIT = 48 * 1024 * 1024


def _iota_f32(shape, axis):
    return lax.broadcasted_iota(I32, shape, axis).astype(F32)


def _mixer_kernel(pos0, tt, n_t,
                  x_ref, pp_ref, cp_ref, g1_ref, win_ref, bd01_ref, bd23_ref, pscale_ref,
                  cw_ref, cb_ref, lng_ref, lnb_ref, wout_ref, bout_ref, g2_ref, wrt_ref, br_ref,
                  tri_ref,
                  x1_ref, hm_ref, gates_ref, route_ref, cnt_ref, spool_ref, sconv_ref,
                  pext, cext, hbuf, carry):
    b = pl.program_id(0)
    j = pl.program_id(1)

    @pl.when((b == 0) & (j == 0))
    def _():
        carry[...] = jnp.zeros_like(carry)

    @pl.when(j == 0)
    def _():
        pext[0:POOL_PAD, :] = pp_ref[...]
        cext[0:CONV_PAD, :] = cp_ref[...]

    @pl.when(j > 0)
    def _():
        pext[0:POOL_PAD, :] = pext[tt:tt + POOL_PAD, :]
        cext[0:CONV_PAD, :] = cext[tt:tt + CONV_PAD, :]

    x = x_ref[...]
    ms = jnp.mean(x * x, axis=-1, keepdims=True)
    h = (x * lax.rsqrt(ms + EPS) * g1_ref[...]).astype(BF16)
    z = jnp.dot(h, win_ref[...], preferred_element_type=F32)
    u = z[:, :D_POOL]
    pext[POOL_PAD:POOL_PAD + tt, :] = u
    v = z[:, D_POOL:D_POOL + D_CONV] * jax.nn.sigmoid(z[:, D_POOL + D_CONV:])
    cext[CONV_PAD:CONV_PAD + tt, :] = v

    pos = pos0 + j * tt + lax.broadcasted_iota(I32, (tt, 1), 0)
    parts = []
    for g, w in enumerate(POOL_WINDOWS):
        sl = slice(g * POOL_GROUP_DIM, (g + 1) * POOL_GROUP_DIM)
        s = pext[POOL_PAD:POOL_PAD + tt, sl]
        for i in range(1, w):
            s = s + pext[POOL_PAD - i:POOL_PAD - i + tt, sl]
        cnt = jnp.minimum(pos + 1, w).astype(F32)
        parts.append(s / cnt - u[:, sl])
    d = jnp.concatenate(parts, axis=-1).astype(BF16)
    half = D_POOL // 2
    yp = jnp.concatenate(
        [jnp.dot(d[:, :half], bd01_ref[...], preferred_element_type=F32),
         jnp.dot(d[:, half:], bd23_ref[...], preferred_element_type=F32)], axis=-1)
    yp = yp * pscale_ref[...]

    def conv_chunk(c, carry_):
        r0 = pl.multiple_of(c * CONV_CHUNK, CONV_CHUNK)
        acc = jnp.zeros((CONV_CHUNK, D_CONV), F32)
        win = cext.at[pl.ds(r0, CONV_CHUNK + CONV_PAD), :]
        for k in range(CONV_WIDTH):
            o = CONV_PAD - CONV_STATE + k
            acc = acc + win[o:o + CONV_CHUNK, :] * cw_ref[k:k + 1, :]
        hbuf[pl.ds(r0, CONV_CHUNK), :] = acc
        return carry_

    lax.fori_loop(0, tt // CONV_CHUNK, conv_chunk, 0)
    hc = hbuf[...] + cb_ref[...]
    mu = jnp.mean(hc, axis=-1, keepdims=True)
    var = jnp.mean(jnp.square(hc - mu), axis=-1, keepdims=True)
    yln = (hc - mu) * lax.rsqrt(var + EPS) * lng_ref[...] + lnb_ref[...]
    yc = yln * jax.nn.sigmoid(yln)

    o = (jnp.dot(yp.astype(BF16), wout_ref[0:D_POOL, :], preferred_element_type=F32)
         + jnp.dot(yc.astype(BF16), wout_ref[D_POOL:, :], preferred_element_type=F32))
    x1 = x + (o + bout_ref[...])
    x1_ref[...] = x1

    ms2 = jnp.mean(x1 * x1, axis=-1, keepdims=True)
    hmb = (x1 * lax.rsqrt(ms2 + EPS) * g2_ref[...]).astype(BF16)
    hb32 = hmb.astype(F32)
    lo = lax.bitcast_convert_type(hb32[:, :HM_WORDS], U32)
    hi = lax.bitcast_convert_type(hb32[:, HM_WORDS:], U32)
    hm_ref[...] = (hi & jnp.uint32(0xFFFF0000)) | lax.shift_right_logical(lo, jnp.uint32(16))

    lt = lax.dot_general(wrt_ref[...], hmb, (((1,), (1,)), ((), ())),
                         preferred_element_type=F32) + br_ref[...]
    row8 = _iota_f32((SUBLANES, tt), 0)
    neg = jnp.float32(-jnp.inf)
    lg = jnp.where(row8 < N_EXPERT_GROUPS, lt[0:SUBLANES, :], neg)
    mg = jnp.max(lg, axis=0, keepdims=True)
    p_grp = 1.0 / jnp.sum(jnp.exp(lg - mg), axis=0, keepdims=True)
    gsel = jnp.min(jnp.where(lg == mg, row8, float(SUBLANES)), axis=0, keepdims=True)
    le = jnp.zeros((SUBLANES, tt), F32)
    for g in range(N_EXPERT_GROUPS):
        r0 = EXPERT_ROW0 + g * EXPERTS_PER_GROUP
        le = jnp.where(gsel == float(g), lt[r0:r0 + EXPERTS_PER_GROUP, :], le)
    m1 = jnp.max(le, axis=0, keepdims=True)
    i1 = jnp.min(jnp.where(le == m1, row8, float(SUBLANES)), axis=0, keepdims=True)
    le2 = jnp.where(row8 == i1, neg, le)
    m2 = jnp.max(le2, axis=0, keepdims=True)
    i2 = jnp.min(jnp.where(le2 == m2, row8, float(SUBLANES)), axis=0, keepdims=True)
    e2 = jnp.exp(m2 - m1)
    den = 1.0 + e2
    gate0 = p_grp * (1.0 / den)
    gate1 = p_grp * (e2 / den)
    eid0 = gsel * float(EXPERTS_PER_GROUP) + i1
    eid1 = gsel * float(EXPERTS_PER_GROUP) + i2

    rowe = _iota_f32((N_EXPERTS, tt), 0)
    oh0 = rowe == eid0
    oh1 = rowe == eid1
    ohf = jnp.where(jnp.logical_or(oh0, oh1), 1.0, 0.0)
    before = jnp.dot(ohf.astype(BF16), tri_ref[...], preferred_element_type=F32) + carry[:, 0:1]
    rank0 = jnp.sum(jnp.where(oh0, before, 0.0), axis=0, keepdims=True)
    rank1 = jnp.sum(jnp.where(oh1, before, 0.0), axis=0, keepdims=True)
    carry[...] = carry[...] + jnp.sum(ohf, axis=1, keepdims=True)
    cnt_ref[...] = carry[...]

    route = jnp.where(row8 == 0.0, eid0,
                      jnp.where(row8 == 1.0, eid1,
                                jnp.where(row8 == 2.0, rank0,
                                          jnp.where(row8 == 3.0, rank1, 0.0))))
    route_ref[...] = route.astype(I32)

    tpad = -(-tt // LANES) * LANES
    rowr = _iota_f32((ROUTER_ROWS, tt), 0)
    gt = jnp.where(rowr == 0.0, gate0, jnp.where(rowr == 1.0, gate1, 0.0))
    if tpad != tt:
        gt = jnp.concatenate([gt, jnp.zeros((ROUTER_ROWS, tpad - tt), F32)], axis=1)
    gates_ref[...] = gt.T[0:tt, :]

    @pl.when(j == n_t - 1)
    def _():
        spool_ref[...] = pext[POOL_PAD + tt - POOL_STATE:POOL_PAD + tt, :]
        sconv_ref[...] = cext[CONV_PAD + tt - CONV_STATE:CONV_PAD + tt, :]


def _mixer_call(x, pool_prefix, conv_prefix, pos0, tt, weights, name):
    bsz, seq, _ = x.shape
    n_t = seq // tt
    n_blk = bsz * n_t
    n_rows = bsz * seq
    tri = (lax.broadcasted_iota(I32, (tt, tt), 0) < lax.broadcasted_iota(I32, (tt, tt), 1)).astype(BF16)

    def const(shape):
        return pl.BlockSpec(shape, lambda b, j: (0,) * len(shape))

    def rows(width):
        return pl.BlockSpec((tt, width), lambda b, j: (b * n_t + j, 0))

    in_specs = [
        pl.BlockSpec((None, tt, D_MODEL), lambda b, j: (b, j, 0)),
        pl.BlockSpec((None, POOL_PAD, D_POOL), lambda b, j: (b, 0, 0)),
        pl.BlockSpec((None, CONV_PAD, D_CONV), lambda b, j: (b, 0, 0)),
    ] + [const(w.shape) for w in weights] + [const((tt, tt))]
    out_shape = (
        jax.ShapeDtypeStruct((n_rows, D_MODEL), F32),
        jax.ShapeDtypeStruct((n_rows, HM_WORDS), U32),
        jax.ShapeDtypeStruct((n_rows, LANES), F32),
        jax.ShapeDtypeStruct((n_blk, SUBLANES, tt), I32),
        jax.ShapeDtypeStruct((N_EXPERTS, LANES), F32),
        jax.ShapeDtypeStruct((1, bsz, POOL_STATE, D_POOL), F32),
        jax.ShapeDtypeStruct((1, bsz, CONV_STATE, D_CONV), F32),
    )
    out_specs = (
        rows(D_MODEL), rows(HM_WORDS), rows(LANES),
        pl.BlockSpec((None, SUBLANES, tt), lambda b, j: (b * n_t + j, 0, 0)),
        pl.BlockSpec((N_EXPERTS, LANES), lambda b, j: (0, 0)),
        pl.BlockSpec((None, None, POOL_STATE, D_POOL), lambda b, j: (0, b, 0, 0)),
        pl.BlockSpec((None, None, CONV_STATE, D_CONV), lambda b, j: (0, b, 0, 0)),
    )
    return pl.pallas_call(
        functools.partial(_mixer_kernel, pos0, tt, n_t),
        out_shape=out_shape,
        grid=(bsz, n_t),
        in_specs=in_specs,
        out_specs=out_specs,
        scratch_shapes=[
            pltpu.VMEM((POOL_PAD + tt, D_POOL), F32),
            pltpu.VMEM((CONV_PAD + tt, D_CONV), F32),
            pltpu.VMEM((tt, D_CONV), F32),
            pltpu.VMEM((N_EXPERTS, LANES), F32),
        ],
        compiler_params=pltpu.CompilerParams(
            dimension_semantics=("arbitrary", "arbitrary"), vmem_limit_bytes=VMEM_LIMIT),
        name=name,
    )(x, pool_prefix, conv_prefix, *weights, tri)


def _row_copy_out(hm_ref, xb_ref, sem, t, d):
    return pltpu.make_async_copy(hm_ref.at[pl.ds(t, 1), :], xb_ref.at[pl.ds(d, 1), :], sem)


def _dispatch_kernel(dest_ref, hm_ref, xb_in_ref, xb_ref, sem):
    del xb_in_ref
    tc = hm_ref.shape[0]

    def start(t, c):
        for k in range(TOP_K):
            _row_copy_out(hm_ref, xb_ref, sem, t, dest_ref[0, k, t]).start()
        return c

    lax.fori_loop(0, tc, start, 0)

    def wait(t, c):
        for k in range(TOP_K):
            _row_copy_out(hm_ref, xb_ref, sem, t, dest_ref[0, k, t]).wait()
        return c

    lax.fori_loop(0, tc, wait, 0)


def _dispatch_call(dest_blocks, hm, xb, name):
    n_rows = hm.shape[0]
    tc = TOKEN_TILE
    return pl.pallas_call(
        _dispatch_kernel,
        out_shape=jax.ShapeDtypeStruct(xb.shape, xb.dtype),
        grid=(n_rows // tc,),
        in_specs=[
            pl.BlockSpec((1, TOP_K, tc), lambda i: (i, 0, 0), memory_space=pltpu.SMEM),
            pl.BlockSpec((tc, HM_WORDS), lambda i: (i, 0)),
            pl.BlockSpec(memory_space=pl.ANY),
        ],
        out_specs=pl.BlockSpec(memory_space=pl.ANY),
        scratch_shapes=[pltpu.SemaphoreType.DMA(())],
        input_output_aliases={2: 0},
        compiler_params=pltpu.CompilerParams(dimension_semantics=("arbitrary",)),
        name=name,
    )(dest_blocks, hm, xb)


def _expert_kernel(te_ref, nu_ref, xb_ref, wg_ref, wu_ref, wd_ref, y_ref, wgb, wub, wdb):
    i = pl.program_id(0)
    e = te_ref[i]
    prev = te_ref[jnp.maximum(i - 1, 0)]

    @pl.when((i == 0) | (e != prev))
    def _():
        wgb[...] = wg_ref[...].astype(BF16)
        wub[...] = wu_ref[...].astype(BF16)
        wdb[...] = wd_ref[...].astype(BF16)

    @pl.when(i < nu_ref[0])
    def _():
        w = xb_ref[...]
        lo = lax.bitcast_convert_type(lax.shift_left(w, jnp.uint32(16)), F32)
        hi = lax.bitcast_convert_type(w & jnp.uint32(0xFFFF0000), F32)
        xt = jnp.concatenate([lo.astype(BF16), hi.astype(BF16)], axis=-1)
        g = jnp.dot(xt, wgb[...], preferred_element_type=F32)
        up = jnp.dot(xt, wub[...], preferred_element_type=F32)
        a = (g * jax.nn.sigmoid(g) * up).astype(BF16)
        y_ref[...] = jnp.dot(a, wdb[...], preferred_element_type=F32)

    @pl.when(i >= nu_ref[0])
    def _():
        y_ref[...] = jnp.zeros_like(y_ref)


def _expert_call(tile_expert, n_used, xb, w_eg, w_eu, w_ed):
    n_slots = xb.shape[0]
    tm = EXPERT_TILE
    grid_spec = pltpu.PrefetchScalarGridSpec(
        num_scalar_prefetch=2,
        grid=(n_slots // tm,),
        in_specs=[
            pl.BlockSpec((tm, HM_WORDS), lambda i, te, nu: (i, 0)),
            pl.BlockSpec((None, D_MODEL, D_FF), lambda i, te, nu: (te[i], 0, 0)),
            pl.BlockSpec((None, D_MODEL, D_FF), lambda i, te, nu: (te[i], 0, 0)),
            pl.BlockSpec((None, D_FF, D_MODEL), lambda i, te, nu: (te[i], 0, 0)),
        ],
        out_specs=pl.BlockSpec((tm, D_MODEL), lambda i, te, nu: (i, 0)),
        scratch_shapes=[
            pltpu.VMEM((D_MODEL, D_FF), BF16),
            pltpu.VMEM((D_MODEL, D_FF), BF16),
            pltpu.VMEM((D_FF, D_MODEL), BF16),
        ],
    )
    return pl.pallas_call(
        _expert_kernel,
        out_shape=jax.ShapeDtypeStruct((n_slots, D_MODEL), F32),
        grid_spec=grid_spec,
        compiler_params=pltpu.CompilerParams(
            dimension_semantics=("arbitrary",), vmem_limit_bytes=VMEM_LIMIT),
        name="experts",
    )(tile_expert, n_used, xb, w_eg, w_eu, w_ed)


def _row_copy_in(yb_ref, buf_ref, sem, k, t, d):
    return pltpu.make_async_copy(yb_ref.at[pl.ds(d, 1), :], buf_ref.at[k, pl.ds(t, 1), :], sem)


def _combine_kernel(dest_ref, x1_ref, gates_ref, gf_ref, yb_ref, out_ref, buf, sem):
    tc = x1_ref.shape[0]

    def start(t, c):
        for k in range(TOP_K):
            _row_copy_in(yb_ref, buf, sem, k, t, dest_ref[0, k, t]).start()
        return c

    lax.fori_loop(0, tc, start, 0)

    def wait(t, c):
        for k in range(TOP_K):
            _row_copy_in(yb_ref, buf, sem, k, t, dest_ref[0, k, t]).wait()
        return c

    lax.fori_loop(0, tc, wait, 0)

    gates = gates_ref[...]
    y = buf[0] * gates[:, 0:1] + buf[1] * gates[:, 1:2]
    xo = x1_ref[...] + y
    ms = jnp.mean(xo * xo, axis=-1, keepdims=True)
    out_ref[...] = xo * lax.rsqrt(ms + EPS) * gf_ref[...]


def _combine_call(dest_blocks, x1, gates, gf, yb, name):
    n_rows = x1.shape[0]
    tc = TOKEN_TILE
    return pl.pallas_call(
        _combine_kernel,
        out_shape=jax.ShapeDtypeStruct((n_rows, D_MODEL), F32),
        grid=(n_rows // tc,),
        in_specs=[
            pl.BlockSpec((1, TOP_K, tc), lambda i: (i, 0, 0), memory_space=pltpu.SMEM),
            pl.BlockSpec((tc, D_MODEL), lambda i: (i, 0)),
            pl.BlockSpec((tc, LANES), lambda i: (i, 0)),
            pl.BlockSpec((1, D_MODEL), lambda i: (0, 0)),
            pl.BlockSpec(memory_space=pl.ANY),
        ],
        out_specs=pl.BlockSpec((tc, D_MODEL), lambda i: (i, 0)),
        scratch_shapes=[pltpu.VMEM((TOP_K, tc, D_MODEL), F32), pltpu.SemaphoreType.DMA(())],
        compiler_params=pltpu.CompilerParams(
            dimension_semantics=("arbitrary",), vmem_limit_bytes=VMEM_LIMIT),
        name=name,
    )(dest_blocks, x1, gates, gf, yb)


def _block_diag2(a, b):
    z = jnp.zeros_like(a)
    return jnp.concatenate([jnp.concatenate([a, z], axis=1), jnp.concatenate([z, b], axis=1)], axis=0)


def kernel(x_prompt, x_sample, state_pool, state_conv, norm1_g, w_in, pool_lin, pool_scale, conv_dw, conv_dw_b, conv_ln_g, conv_ln_b, w_out, b_out, norm2_g, w_rg, b_rg, w_re, b_re, w_eg, w_eu, w_ed, norm_f_g):
    assert norm1_g.shape[0] == 1, "single-layer trunk"
    bsz, seq, _ = x_prompt.shape
    dbsz, dseq, _ = x_sample.shape
    past_len = 1024
    n_prompt = bsz * seq
    n_sample = dbsz * dseq
    n_tokens = n_prompt + n_sample
    assert seq % PROMPT_TILE == 0 and n_prompt % TOKEN_TILE == 0 and n_sample % TOKEN_TILE == 0

    row = lambda a: a.reshape(1, -1)
    router_w = jnp.zeros((ROUTER_ROWS, D_MODEL), F32)
    router_w = router_w.at[0:N_EXPERT_GROUPS].set(w_rg[0].T)
    router_w = router_w.at[EXPERT_ROW0:EXPERT_ROW0 + N_EXPERTS].set(
        jnp.transpose(w_re[0], (0, 2, 1)).reshape(N_EXPERTS, D_MODEL))
    router_b = jnp.zeros((ROUTER_ROWS,), F32)
    router_b = router_b.at[0:N_EXPERT_GROUPS].set(b_rg[0])
    router_b = router_b.at[EXPERT_ROW0:EXPERT_ROW0 + N_EXPERTS].set(b_re[0].reshape(-1))
    conv_w = jnp.concatenate([conv_dw[0], jnp.zeros((1, D_CONV), F32)], axis=0)
    weights = [
        row(norm1_g[0]), w_in[0].astype(BF16),
        _block_diag2(pool_lin[0, 0], pool_lin[0, 1]).astype(BF16),
        _block_diag2(pool_lin[0, 2], pool_lin[0, 3]).astype(BF16),
        row(pool_scale[0]), conv_w, row(conv_dw_b[0]), row(conv_ln_g[0]), row(conv_ln_b[0]),
        w_out[0].astype(BF16), row(b_out[0]), row(norm2_g[0]),
        router_w.astype(BF16), router_b.reshape(ROUTER_ROWS, 1),
    ]

    zpool = jnp.zeros((bsz, POOL_PAD, D_POOL), F32)
    zconv = jnp.zeros((bsz, CONV_PAD, D_CONV), F32)
    ppool = jnp.pad(state_pool[0], ((0, 0), (POOL_PAD - POOL_STATE, 0), (0, 0)))
    pconv = jnp.pad(state_conv[0], ((0, 0), (CONV_PAD - CONV_STATE, 0), (0, 0)))

    x1_p, hm_p, gates_p, route_p, cnt_p, sp_p, sc_p = _mixer_call(
        x_prompt, zpool, zconv, 0, PROMPT_TILE, weights, "mixer_prompt")
    x1_s, hm_s, gates_s, route_s, cnt_s, sp_s, sc_s = _mixer_call(
        x_sample, ppool, pconv, past_len, dseq, weights, "mixer_sample")

    tm = EXPERT_TILE
    cnt_p = cnt_p[:, 0].astype(I32)
    cnt_s = cnt_s[:, 0].astype(I32)
    padded = (cnt_p + cnt_s + tm - 1) // tm * tm
    pad_end = jnp.cumsum(padded)
    pad_start = pad_end - padded

    def dest_blocks(route, base):
        eid = jnp.stack([route[:, k, :].reshape(-1) for k in range(TOP_K)])
        rank = jnp.stack([route[:, TOP_K + k, :].reshape(-1) for k in range(TOP_K)])
        dest = base[eid] + rank
        return dest.reshape(TOP_K, -1, TOKEN_TILE).transpose(1, 0, 2)

    dest_p = dest_blocks(route_p, pad_start)
    dest_s = dest_blocks(route_s, pad_start + cnt_p)
    n_tiles = -(-(n_tokens * TOP_K + N_EXPERTS * (tm - 1)) // tm)
    n_used = (pad_end[-1] // tm).astype(I32)
    tile_ids = jnp.minimum(jnp.arange(n_tiles, dtype=I32), n_used - 1)
    tile_expert = jnp.minimum(
        jnp.searchsorted(pad_end, tile_ids * tm, side="right"), N_EXPERTS - 1).astype(I32)

    xb = jnp.zeros((n_tiles * tm, HM_WORDS), U32)
    xb = _dispatch_call(dest_p, hm_p, xb, "dispatch_prompt")
    xb = _dispatch_call(dest_s, hm_s, xb, "dispatch_sample")
    yb = _expert_call(tile_expert, n_used.reshape(1), xb, w_eg[0], w_eu[0], w_ed[0])
    gf = row(norm_f_g)
    y_p = _combine_call(dest_p, x1_p, gates_p, gf, yb, "combine_prompt")
    y_s = _combine_call(dest_s, x1_s, gates_s, gf, yb, "combine_sample")

    return (y_p.reshape(bsz, seq, D_MODEL), y_s.reshape(dbsz, dseq, D_MODEL),
            sp_p, sc_p, sp_s, sc_s)
```

```python
import functools

import jax
import jax.numpy as jnp
from jax import lax
from jax.experimental import pallas as pl
from jax.experimental.pallas import tpu as pltpu

F32 = jnp.float32
BF16 = jnp.bfloat16
U32 = jnp.uint32
I32 = jnp.int32

D_MODEL = 1024
D_POOL = 512
D_CONV = 512
D_IN = D_POOL + 2 * D_CONV
POOL_WINDOWS = (2, 4, 8, 16)
POOL_GROUP_DIM = D_POOL // len(POOL_WINDOWS)
POOL_STATE = max(POOL_WINDOWS) - 1
CONV_WIDTH = 31
CONV_STATE = CONV_WIDTH - 1
N_EXPERT_GROUPS = 4
EXPERTS_PER_GROUP = 8
N_EXPERTS = N_EXPERT_GROUPS * EXPERTS_PER_GROUP
TOP_K = 2
D_FF = D_MODEL // 2
EPS = 1e-6

SUBLANES = 8
LANES = 128
POOL_PAD = 16
CONV_PAD = 32
ROUTER_ROWS = LANES
EXPERT_ROW0 = SUBLANES
CONV_CHUNK = 64
HM_WORDS = D_MODEL // 2
HM_PIECES = HM_WORDS // LANES
Y_PIECES = D_MODEL // LANES

PROMPT_TILE = 512
TOKEN_TILE = 512
EXPERT_TILE = 256
VMEM_LIMIT = 48 * 1024 * 1024


def _iota_f32(shape, axis):
    return lax.broadcasted_iota(I32, shape, axis).astype(F32)


def _mixer_kernel(pos0, tt, n_t,
                  x_ref, pp_ref, cp_ref, g1_ref, win_ref, bd01_ref, bd23_ref, pscale_ref,
                  cw_ref, cb_ref, lng_ref, lnb_ref, wout_ref, bout_ref, g2_ref, wrt_ref, br_ref,
                  tri_ref,
                  x1_ref, hm_ref, gates_ref, route_ref, cnt_ref, spool_ref, sconv_ref,
                  pext, cext, hbuf, carry):
    b = pl.program_id(0)
    j = pl.program_id(1)

    @pl.when((b == 0) & (j == 0))
    def _():
        carry[...] = jnp.zeros_like(carry)

    @pl.when(j == 0)
    def _():
        pext[0:POOL_PAD, :] = pp_ref[...]
        cext[0:CONV_PAD, :] = cp_ref[...]

    @pl.when(j > 0)
    def _():
        pext[0:POOL_PAD, :] = pext[tt:tt + POOL_PAD, :]
        cext[0:CONV_PAD, :] = cext[tt:tt + CONV_PAD, :]

    x = x_ref[...]
    ms = jnp.mean(x * x, axis=-1, keepdims=True)
    h = (x * lax.rsqrt(ms + EPS) * g1_ref[...]).astype(BF16)
    z = jnp.dot(h, win_ref[...], preferred_element_type=F32)
    u = z[:, :D_POOL]
    pext[POOL_PAD:POOL_PAD + tt, :] = u
    v = z[:, D_POOL:D_POOL + D_CONV] * jax.nn.sigmoid(z[:, D_POOL + D_CONV:])
    cext[CONV_PAD:CONV_PAD + tt, :] = v

    pos = pos0 + j * tt + lax.broadcasted_iota(I32, (tt, 1), 0)
    parts = []
    for g, w in enumerate(POOL_WINDOWS):
        sl = slice(g * POOL_GROUP_DIM, (g + 1) * POOL_GROUP_DIM)
        s = pext[POOL_PAD:POOL_PAD + tt, sl]
        for i in range(1, w):
            s = s + pext[POOL_PAD - i:POOL_PAD - i + tt, sl]
        cnt = jnp.minimum(pos + 1, w).astype(F32)
        parts.append(s / cnt - u[:, sl])
    d = jnp.concatenate(parts, axis=-1).astype(BF16)
    half = D_POOL // 2
    yp = jnp.concatenate(
        [jnp.dot(d[:, :half], bd01_ref[...], preferred_element_type=F32),
         jnp.dot(d[:, half:], bd23_ref[...], preferred_element_type=F32)], axis=-1)
    yp = yp * pscale_ref[...]

    def conv_chunk(c, carry_):
        r0 = pl.multiple_of(c * CONV_CHUNK, CONV_CHUNK)
        acc = jnp.zeros((CONV_CHUNK, D_CONV), F32)
        win = cext.at[pl.ds(r0, CONV_CHUNK + CONV_PAD), :]
        for k in range(CONV_WIDTH):
            o = CONV_PAD - CONV_STATE + k
            acc = acc + win[o:o + CONV_CHUNK, :] * cw_ref[k:k + 1, :]
        hbuf[pl.ds(r0, CONV_CHUNK), :] = acc
        return carry_

    lax.fori_loop(0, tt // CONV_CHUNK, conv_chunk, 0)
    hc = hbuf[...] + cb_ref[...]
    mu = jnp.mean(hc, axis=-1, keepdims=True)
    var = jnp.mean(jnp.square(hc - mu), axis=-1, keepdims=True)
    yln = (hc - mu) * lax.rsqrt(var + EPS) * lng_ref[...] + lnb_ref[...]
    yc = yln * jax.nn.sigmoid(yln)

    o = (jnp.dot(yp.astype(BF16), wout_ref[0:D_POOL, :], preferred_element_type=F32)
         + jnp.dot(yc.astype(BF16), wout_ref[D_POOL:, :], preferred_element_type=F32))
    x1 = x + (o + bout_ref[...])
    x1_ref[...] = x1

    ms2 = jnp.mean(x1 * x1, axis=-1, keepdims=True)
    hmb = (x1 * lax.rsqrt(ms2 + EPS) * g2_ref[...]).astype(BF16)
    hb32 = hmb.astype(F32)
    lo = lax.bitcast_convert_type(hb32[:, :HM_WORDS], U32)
    hi = lax.bitcast_convert_type(hb32[:, HM_WORDS:], U32)
    packed = (hi & jnp.uint32(0xFFFF0000)) | lax.shift_right_logical(lo, jnp.uint32(16))
    for p in range(HM_PIECES):
        hm_ref[pl.ds(p, tt, stride=HM_PIECES), :] = packed[:, p * LANES:(p + 1) * LANES]

    lt = lax.dot_general(wrt_ref[...], hmb, (((1,), (1,)), ((), ())),
                         preferred_element_type=F32) + br_ref[...]
    row8 = _iota_f32((SUBLANES, tt), 0)
    neg = jnp.float32(-jnp.inf)
    lg = jnp.where(row8 < N_EXPERT_GROUPS, lt[0:SUBLANES, :], neg)
    mg = jnp.max(lg, axis=0, keepdims=True)
    p_grp = 1.0 / jnp.sum(jnp.exp(lg - mg), axis=0, keepdims=True)
    gsel = jnp.min(jnp.where(lg == mg, row8, float(SUBLANES)), axis=0, keepdims=True)
    le = jnp.zeros((SUBLANES, tt), F32)
    for g in range(N_EXPERT_GROUPS):
        r0 = EXPERT_ROW0 + g * EXPERTS_PER_GROUP
        le = jnp.where(gsel == float(g), lt[r0:r0 + EXPERTS_PER_GROUP, :], le)
    m1 = jnp.max(le, axis=0, keepdims=True)
    i1 = jnp.min(jnp.where(le == m1, row8, float(SUBLANES)), axis=0, keepdims=True)
    le2 = jnp.where(row8 == i1, neg, le)
    m2 = jnp.max(le2, axis=0, keepdims=True)
    i2 = jnp.min(jnp.where(le2 == m2, row8, float(SUBLANES)), axis=0, keepdims=True)
    e2 = jnp.exp(m2 - m1)
    den = 1.0 + e2
    gate0 = p_grp * (1.0 / den)
    gate1 = p_grp * (e2 / den)
    eid0 = gsel * float(EXPERTS_PER_GROUP) + i1
    eid1 = gsel * float(EXPERTS_PER_GROUP) + i2

    rowe = _iota_f32((N_EXPERTS, tt), 0)
    oh0 = rowe == eid0
    oh1 = rowe == eid1
    ohf = jnp.where(jnp.logical_or(oh0, oh1), 1.0, 0.0)
    before = jnp.dot(ohf.astype(BF16), tri_ref[...], preferred_element_type=F32) + carry[:, 0:1]
    rank0 = jnp.sum(jnp.where(oh0, before, 0.0), axis=0, keepdims=True)
    rank1 = jnp.sum(jnp.where(oh1, before, 0.0), axis=0, keepdims=True)
    carry[...] = carry[...] + jnp.sum(ohf, axis=1, keepdims=True)
    cnt_ref[...] = carry[...]

    route = jnp.where(row8 == 0.0, eid0,
                      jnp.where(row8 == 1.0, eid1,
                                jnp.where(row8 == 2.0, rank0,
                                          jnp.where(row8 == 3.0, rank1, 0.0))))
    route_ref[...] = route.astype(I32)

    tpad = -(-tt // LANES) * LANES
    rowr = _iota_f32((ROUTER_ROWS, tt), 0)
    gt = jnp.where(rowr == 0.0, gate0, jnp.where(rowr == 1.0, gate1, 0.0))
    if tpad != tt:
        gt = jnp.concatenate([gt, jnp.zeros((ROUTER_ROWS, tpad - tt), F32)], axis=1)
    gates_ref[...] = gt.T[0:tt, :]

    @pl.when(j == n_t - 1)
    def _():
        spool_ref[...] = pext[POOL_PAD + tt - POOL_STATE:POOL_PAD + tt, :]
        sconv_ref[...] = cext[CONV_PAD + tt - CONV_STATE:CONV_PAD + tt, :]


def _mixer_call(x, pool_prefix, conv_prefix, pos0, tt, weights, name):
    bsz, seq, _ = x.shape
    n_t = seq // tt
    n_blk = bsz * n_t
    n_rows = bsz * seq
    tri = (lax.broadcasted_iota(I32, (tt, tt), 0) < lax.broadcasted_iota(I32, (tt, tt), 1)).astype(BF16)

    def const(shape):
        return pl.BlockSpec(shape, lambda b, j: (0,) * len(shape))

    def rows(width):
        return pl.BlockSpec((tt, width), lambda b, j: (b * n_t + j, 0))

    in_specs = [
        pl.BlockSpec((None, tt, D_MODEL), lambda b, j: (b, j, 0)),
        pl.BlockSpec((None, POOL_PAD, D_POOL), lambda b, j: (b, 0, 0)),
        pl.BlockSpec((None, CONV_PAD, D_CONV), lambda b, j: (b, 0, 0)),
    ] + [const(w.shape) for w in weights] + [const((tt, tt))]
    out_shape = (
        jax.ShapeDtypeStruct((n_rows, D_MODEL), F32),
        jax.ShapeDtypeStruct((n_rows * HM_PIECES, LANES), U32),
        jax.ShapeDtypeStruct((n_rows, LANES), F32),
        jax.ShapeDtypeStruct((n_blk, SUBLANES, tt), I32),
        jax.ShapeDtypeStruct((N_EXPERTS, LANES), F32),
        jax.ShapeDtypeStruct((1, bsz, POOL_STATE, D_POOL), F32),
        jax.ShapeDtypeStruct((1, bsz, CONV_STATE, D_CONV), F32),
    )
    out_specs = (
        rows(D_MODEL),
        pl.BlockSpec((tt * HM_PIECES, LANES), lambda b, j: (b * n_t + j, 0)),
        rows(LANES),
        pl.BlockSpec((None, SUBLANES, tt), lambda b, j: (b * n_t + j, 0, 0)),
        pl.BlockSpec((N_EXPERTS, LANES), lambda b, j: (0, 0)),
        pl.BlockSpec((None, None, POOL_STATE, D_POOL), lambda b, j: (0, b, 0, 0)),
        pl.BlockSpec((None, None, CONV_STATE, D_CONV), lambda b, j: (0, b, 0, 0)),
    )
    return pl.pallas_call(
        functools.partial(_mixer_kernel, pos0, tt, n_t),
        out_shape=out_shape,
        grid=(bsz, n_t),
        in_specs=in_specs,
        out_specs=out_specs,
        scratch_shapes=[
            pltpu.VMEM((POOL_PAD + tt, D_POOL), F32),
            pltpu.VMEM((CONV_PAD + tt, D_CONV), F32),
            pltpu.VMEM((tt, D_CONV), F32),
            pltpu.VMEM((N_EXPERTS, LANES), F32),
        ],
        compiler_params=pltpu.CompilerParams(
            dimension_semantics=("arbitrary", "arbitrary"), vmem_limit_bytes=VMEM_LIMIT),
        name=name,
    )(x, pool_prefix, conv_prefix, *weights, tri)


def _row_copy_out(hm_ref, xb_ref, sem, t, d):
    src = hm_ref.at[pl.ds(pl.multiple_of(t * HM_PIECES, HM_PIECES), HM_PIECES), :]
    dst = xb_ref.at[pl.ds(pl.multiple_of(d * HM_PIECES, HM_PIECES), HM_PIECES), :]
    return pltpu.make_async_copy(src, dst, sem)


def _dispatch_kernel(dest_ref, hm_ref, xb_in_ref, xb_ref, sem):
    del xb_in_ref
    tc = hm_ref.shape[0] // HM_PIECES

    def start(t, c):
        for k in range(TOP_K):
            _row_copy_out(hm_ref, xb_ref, sem, t, dest_ref[0, k, t]).start()
        return c

    lax.fori_loop(0, tc, start, 0)

    def wait(t, c):
        for k in range(TOP_K):
            _row_copy_out(hm_ref, xb_ref, sem, t, dest_ref[0, k, t]).wait()
        return c

    lax.fori_loop(0, tc, wait, 0)


def _dispatch_call(dest_blocks, hm, xb, name):
    n_rows = hm.shape[0] // HM_PIECES
    tc = TOKEN_TILE
    return pl.pallas_call(
        _dispatch_kernel,
        out_shape=jax.ShapeDtypeStruct(xb.shape, xb.dtype),
        grid=(n_rows // tc,),
        in_specs=[
            pl.BlockSpec((1, TOP_K, tc), lambda i: (i, 0, 0), memory_space=pltpu.SMEM),
            pl.BlockSpec((tc * HM_PIECES, LANES), lambda i: (i, 0)),
            pl.BlockSpec(memory_space=pl.ANY),
        ],
        out_specs=pl.BlockSpec(memory_space=pl.ANY),
        scratch_shapes=[pltpu.SemaphoreType.DMA(())],
        input_output_aliases={2: 0},
        compiler_params=pltpu.CompilerParams(dimension_semantics=("arbitrary",)),
        name=name,
    )(dest_blocks, hm, xb)


def _expert_kernel(te_ref, nu_ref, xb_ref, wg_ref, wu_ref, wd_ref, y_ref, wgb, wub, wdb):
    i = pl.program_id(0)
    e = te_ref[i]
    prev = te_ref[jnp.maximum(i - 1, 0)]

    @pl.when((i == 0) | (e != prev))
    def _():
        wgb[...] = wg_ref[...].astype(BF16)
        wub[...] = wu_ref[...].astype(BF16)
        wdb[...] = wd_ref[...].astype(BF16)

    @pl.when(i < nu_ref[0])
    def _():
        tm = y_ref.shape[0] // Y_PIECES
        words = [xb_ref[pl.ds(p, tm, stride=HM_PIECES), :] for p in range(HM_PIECES)]
        lo = [lax.bitcast_convert_type(lax.shift_left(w, jnp.uint32(16)), F32).astype(BF16) for w in words]
        hi = [lax.bitcast_convert_type(w & jnp.uint32(0xFFFF0000), F32).astype(BF16) for w in words]
        xt = jnp.concatenate(lo + hi, axis=-1)
        g = jnp.dot(xt, wgb[...], preferred_element_type=F32)
        up = jnp.dot(xt, wub[...], preferred_element_type=F32)
        a = (g * jax.nn.sigmoid(g) * up).astype(BF16)
        y = jnp.dot(a, wdb[...], preferred_element_type=F32)
        for p in range(Y_PIECES):
            y_ref[pl.ds(p, tm, stride=Y_PIECES), :] = y[:, p * LANES:(p + 1) * LANES]

    @pl.when(i >= nu_ref[0])
    def _():
        y_ref[...] = jnp.zeros_like(y_ref)


def _expert_call(tile_expert, n_used, xb, w_eg, w_eu, w_ed):
    n_slots = xb.shape[0] // HM_PIECES
    tm = EXPERT_TILE
    grid_spec = pltpu.PrefetchScalarGridSpec(
        num_scalar_prefetch=2,
        grid=(n_slots // tm,),
        in_specs=[
            pl.BlockSpec((tm * HM_PIECES, LANES), lambda i, te, nu: (i, 0)),
            pl.BlockSpec((None, D_MODEL, D_FF), lambda i, te, nu: (te[i], 0, 0)),
            pl.BlockSpec((None, D_MODEL, D_FF), lambda i, te, nu: (te[i], 0, 0)),
            pl.BlockSpec((None, D_FF, D_MODEL), lambda i, te, nu: (te[i], 0, 0)),
        ],
        out_specs=pl.BlockSpec((tm * Y_PIECES, LANES), lambda i, te, nu: (i, 0)),
        scratch_shapes=[
            pltpu.VMEM((D_MODEL, D_FF), BF16),
            pltpu.VMEM((D_MODEL, D_FF), BF16),
            pltpu.VMEM((D_FF, D_MODEL), BF16),
        ],
    )
    return pl.pallas_call(
        _expert_kernel,
        out_shape=jax.ShapeDtypeStruct((n_slots * Y_PIECES, LANES), F32),
        grid_spec=grid_spec,
        compiler_params=pltpu.CompilerParams(
            dimension_semantics=("arbitrary",), vmem_limit_bytes=VMEM_LIMIT),
        name="experts",
    )(tile_expert, n_used, xb, w_eg, w_eu, w_ed)


def _row_copy_in(yb_ref, buf_ref, sem, k, t, d):
    src = yb_ref.at[pl.ds(pl.multiple_of(d * Y_PIECES, Y_PIECES), Y_PIECES), :]
    dst = buf_ref.at[k, pl.ds(pl.multiple_of(t * Y_PIECES, Y_PIECES), Y_PIECES), :]
    return pltpu.make_async_copy(src, dst, sem)


def _combine_kernel(dest_ref, x1_ref, gates_ref, gf_ref, yb_ref, out_ref, buf, sem):
    tc = x1_ref.shape[0]

    def start(t, c):
        for k in range(TOP_K):
            _row_copy_in(yb_ref, buf, sem, k, t, dest_ref[0, k, t]).start()
        return c

    lax.fori_loop(0, tc, start, 0)

    def wait(t, c):
        for k in range(TOP_K):
            _row_copy_in(yb_ref, buf, sem, k, t, dest_ref[0, k, t]).wait()
        return c

    lax.fori_loop(0, tc, wait, 0)

    gates = gates_ref[...]
    y = jnp.concatenate(
        [buf[0, pl.ds(p, tc, stride=Y_PIECES), :] * gates[:, 0:1]
         + buf[1, pl.ds(p, tc, stride=Y_PIECES), :] * gates[:, 1:2] for p in range(Y_PIECES)], axis=-1)
    xo = x1_ref[...] + y
    ms = jnp.mean(xo * xo, axis=-1, keepdims=True)
    out_ref[...] = xo * lax.rsqrt(ms + EPS) * gf_ref[...]


def _combine_call(dest_blocks, x1, gates, gf, yb, name):
    n_rows = x1.shape[0]
    tc = TOKEN_TILE
    return pl.pallas_call(
        _combine_kernel,
        out_shape=jax.ShapeDtypeStruct((n_rows, D_MODEL), F32),
        grid=(n_rows // tc,),
        in_specs=[
            pl.BlockSpec((1, TOP_K, tc), lambda i: (i, 0, 0), memory_space=pltpu.SMEM),
            pl.BlockSpec((tc, D_MODEL), lambda i: (i, 0)),
            pl.BlockSpec((tc, LANES), lambda i: (i, 0)),
            pl.BlockSpec((1, D_MODEL), lambda i: (0, 0)),
            pl.BlockSpec(memory_space=pl.ANY),
        ],
        out_specs=pl.BlockSpec((tc, D_MODEL), lambda i: (i, 0)),
        scratch_shapes=[pltpu.VMEM((TOP_K, tc * Y_PIECES, LANES), F32), pltpu.SemaphoreType.DMA(())],
        compiler_params=pltpu.CompilerParams(
            dimension_semantics=("arbitrary",), vmem_limit_bytes=VMEM_LIMIT),
        name=name,
    )(dest_blocks, x1, gates, gf, yb)


def _block_diag2(a, b):
    z = jnp.zeros_like(a)
    return jnp.concatenate([jnp.concatenate([a, z], axis=1), jnp.concatenate([z, b], axis=1)], axis=0)


def kernel(x_prompt, x_sample, state_pool, state_conv, norm1_g, w_in, pool_lin, pool_scale, conv_dw, conv_dw_b, conv_ln_g, conv_ln_b, w_out, b_out, norm2_g, w_rg, b_rg, w_re, b_re, w_eg, w_eu, w_ed, norm_f_g):
    assert norm1_g.shape[0] == 1, "single-layer trunk"
    bsz, seq, _ = x_prompt.shape
    dbsz, dseq, _ = x_sample.shape
    past_len = 1024
    n_prompt = bsz * seq
    n_sample = dbsz * dseq
    n_tokens = n_prompt + n_sample
    assert seq % PROMPT_TILE == 0 and n_prompt % TOKEN_TILE == 0 and n_sample % TOKEN_TILE == 0

    row = lambda a: a.reshape(1, -1)
    router_w = jnp.zeros((ROUTER_ROWS, D_MODEL), F32)
    router_w = router_w.at[0:N_EXPERT_GROUPS].set(w_rg[0].T)
    router_w = router_w.at[EXPERT_ROW0:EXPERT_ROW0 + N_EXPERTS].set(
        jnp.transpose(w_re[0], (0, 2, 1)).reshape(N_EXPERTS, D_MODEL))
    router_b = jnp.zeros((ROUTER_ROWS,), F32)
    router_b = router_b.at[0:N_EXPERT_GROUPS].set(b_rg[0])
    router_b = router_b.at[EXPERT_ROW0:EXPERT_ROW0 + N_EXPERTS].set(b_re[0].reshape(-1))
    conv_w = jnp.concatenate([conv_dw[0], jnp.zeros((1, D_CONV), F32)], axis=0)
    weights = [
        row(norm1_g[0]), w_in[0].astype(BF16),
        _block_diag2(pool_lin[0, 0], pool_lin[0, 1]).astype(BF16),
        _block_diag2(pool_lin[0, 2], pool_lin[0, 3]).astype(BF16),
        row(pool_scale[0]), conv_w, row(conv_dw_b[0]), row(conv_ln_g[0]), row(conv_ln_b[0]),
        w_out[0].astype(BF16), row(b_out[0]), row(norm2_g[0]),
        router_w.astype(BF16), router_b.reshape(ROUTER_ROWS, 1),
    ]

    zpool = jnp.zeros((bsz, POOL_PAD, D_POOL), F32)
    zconv = jnp.zeros((bsz, CONV_PAD, D_CONV), F32)
    ppool = jnp.pad(state_pool[0], ((0, 0), (POOL_PAD - POOL_STATE, 0), (0, 0)))
    pconv = jnp.pad(state_conv[0], ((0, 0), (CONV_PAD - CONV_STATE, 0), (0, 0)))

    x1_p, hm_p, gates_p, route_p, cnt_p, sp_p, sc_p = _mixer_call(
        x_prompt, zpool, zconv, 0, PROMPT_TILE, weights, "mixer_prompt")
    x1_s, hm_s, gates_s, route_s, cnt_s, sp_s, sc_s = _mixer_call(
        x_sample, ppool, pconv, past_len, dseq, weights, "mixer_sample")

    tm = EXPERT_TILE
    cnt_p = cnt_p[:, 0].astype(I32)
    cnt_s = cnt_s[:, 0].astype(I32)
    padded = (cnt_p + cnt_s + tm - 1) // tm * tm
    pad_end = jnp.cumsum(padded)
    pad_start = pad_end - padded

    def dest_blocks(route, base):
        eid = jnp.stack([route[:, k, :].reshape(-1) for k in range(TOP_K)])
        rank = jnp.stack([route[:, TOP_K + k, :].reshape(-1) for k in range(TOP_K)])
        experts = jnp.arange(N_EXPERTS, dtype=I32)
        dest = rank + jnp.sum(jnp.where(eid[..., None] == experts, base, 0), axis=-1)
        return dest.reshape(TOP_K, -1, TOKEN_TILE).transpose(1, 0, 2)

    dest_p = dest_blocks(route_p, pad_start)
    dest_s = dest_blocks(route_s, pad_start + cnt_p)
    n_tiles = -(-(n_tokens * TOP_K + N_EXPERTS * (tm - 1)) // tm)
    n_used = (pad_end[-1] // tm).astype(I32)
    tile_ids = jnp.minimum(jnp.arange(n_tiles, dtype=I32), n_used - 1)
    tile_expert = jnp.minimum(
        jnp.sum(pad_end[None, :] <= (tile_ids * tm)[:, None], axis=1), N_EXPERTS - 1).astype(I32)

    xb = jnp.zeros((n_tiles * tm * HM_PIECES, LANES), U32)
    xb = _dispatch_call(dest_p, hm_p, xb, "dispatch_prompt")
    xb = _dispatch_call(dest_s, hm_s, xb, "dispatch_sample")
    yb = _expert_call(tile_expert, n_used.reshape(1), xb, w_eg[0], w_eu[0], w_ed[0])
    gf = row(norm_f_g)
    y_p = _combine_call(dest_p, x1_p, gates_p, gf, yb, "combine_prompt")
    y_s = _combine_call(dest_s, x1_s, gates_s, gf, yb, "combine_sample")

    return (y_p.reshape(bsz, seq, D_MODEL), y_s.reshape(dbsz, dseq, D_MODEL),
            sp_p, sc_p, sp_s, sc_s)
```

```python
import functools

import jax
import jax.numpy as jnp
from jax import lax
from jax.experimental import pallas as pl
from jax.experimental.pallas import tpu as pltpu

F32 = jnp.float32
BF16 = jnp.bfloat16
U32 = jnp.uint32
I32 = jnp.int32

D_MODEL = 1024
D_POOL = 512
D_CONV = 512
D_IN = D_POOL + 2 * D_CONV
POOL_WINDOWS = (2, 4, 8, 16)
POOL_GROUP_DIM = D_POOL // len(POOL_WINDOWS)
POOL_STATE = max(POOL_WINDOWS) - 1
CONV_WIDTH = 31
CONV_STATE = CONV_WIDTH - 1
N_EXPERT_GROUPS = 4
EXPERTS_PER_GROUP = 8
N_EXPERTS = N_EXPERT_GROUPS * EXPERTS_PER_GROUP
TOP_K = 2
D_FF = D_MODEL // 2
EPS = 1e-6

SUBLANES = 8
LANES = 128
POOL_PAD = 16
CONV_PAD = 32
ROUTER_ROWS = LANES
EXPERT_ROW0 = SUBLANES
CONV_CHUNK = 64
HM_WORDS = D_MODEL // 2
HM_PIECES = HM_WORDS // LANES
Y_PIECES = D_MODEL // LANES

PROMPT_TILE = 512
TOKEN_TILE = 512
EXPERT_TILE = 256
ISSUE_UNROLL = 8
VMEM_LIMIT = 48 * 1024 * 1024


def _iota_f32(shape, axis):
    return lax.broadcasted_iota(I32, shape, axis).astype(F32)


def _mixer_kernel(pos0, tt, n_t,
                  x_ref, pp_ref, cp_ref, g1_ref, win_ref, bd01_ref, bd23_ref, pscale_ref,
                  cw_ref, cb_ref, lng_ref, lnb_ref, wout_ref, bout_ref, g2_ref, wrt_ref, br_ref,
                  tri_ref,
                  x1_ref, hm_ref, gates_ref, route_ref, cnt_ref, spool_ref, sconv_ref,
                  pext, cext, hbuf, carry):
    b = pl.program_id(0)
    j = pl.program_id(1)

    @pl.when((b == 0) & (j == 0))
    def _():
        carry[...] = jnp.zeros_like(carry)

    @pl.when(j == 0)
    def _():
        pext[0:POOL_PAD, :] = pp_ref[...]
        cext[0:CONV_PAD, :] = cp_ref[...]

    @pl.when(j > 0)
    def _():
        pext[0:POOL_PAD, :] = pext[tt:tt + POOL_PAD, :]
        cext[0:CONV_PAD, :] = cext[tt:tt + CONV_PAD, :]

    x = x_ref[...]
    ms = jnp.mean(x * x, axis=-1, keepdims=True)
    h = (x * lax.rsqrt(ms + EPS) * g1_ref[...]).astype(BF16)
    z = jnp.dot(h, win_ref[...], preferred_element_type=F32)
    u = z[:, :D_POOL]
    pext[POOL_PAD:POOL_PAD + tt, :] = u
    v = z[:, D_POOL:D_POOL + D_CONV] * jax.nn.sigmoid(z[:, D_POOL + D_CONV:])
    cext[CONV_PAD:CONV_PAD + tt, :] = v

    pos = pos0 + j * tt + lax.broadcasted_iota(I32, (tt, 1), 0)
    parts = []
    for g, w in enumerate(POOL_WINDOWS):
        sl = slice(g * POOL_GROUP_DIM, (g + 1) * POOL_GROUP_DIM)
        s = pext[POOL_PAD:POOL_PAD + tt, sl]
        for i in range(1, w):
            s = s + pext[POOL_PAD - i:POOL_PAD - i + tt, sl]
        cnt = jnp.minimum(pos + 1, w).astype(F32)
        parts.append(s / cnt - u[:, sl])
    d = jnp.concatenate(parts, axis=-1).astype(BF16)
    half = D_POOL // 2
    yp = jnp.concatenate(
        [jnp.dot(d[:, :half], bd01_ref[...], preferred_element_type=F32),
         jnp.dot(d[:, half:], bd23_ref[...], preferred_element_type=F32)], axis=-1)
    yp = yp * pscale_ref[...]

    def conv_chunk(c, carry_):
        r0 = pl.multiple_of(c * CONV_CHUNK, CONV_CHUNK)
        acc = jnp.zeros((CONV_CHUNK, D_CONV), F32)
        win = cext.at[pl.ds(r0, CONV_CHUNK + CONV_PAD), :]
        for k in range(CONV_WIDTH):
            o = CONV_PAD - CONV_STATE + k
            acc = acc + win[o:o + CONV_CHUNK, :] * cw_ref[k:k + 1, :]
        hbuf[pl.ds(r0, CONV_CHUNK), :] = acc
        return carry_

    lax.fori_loop(0, tt // CONV_CHUNK, conv_chunk, 0)
    hc = hbuf[...] + cb_ref[...]
    mu = jnp.mean(hc, axis=-1, keepdims=True)
    var = jnp.mean(jnp.square(hc - mu), axis=-1, keepdims=True)
    yln = (hc - mu) * lax.rsqrt(var + EPS) * lng_ref[...] + lnb_ref[...]
    yc = yln * jax.nn.sigmoid(yln)

    o = (jnp.dot(yp.astype(BF16), wout_ref[0:D_POOL, :], preferred_element_type=F32)
         + jnp.dot(yc.astype(BF16), wout_ref[D_POOL:, :], preferred_element_type=F32))
    x1 = x + (o + bout_ref[...])
    x1_ref[...] = x1

    ms2 = jnp.mean(x1 * x1, axis=-1, keepdims=True)
    hmb = (x1 * lax.rsqrt(ms2 + EPS) * g2_ref[...]).astype(BF16)
    hb32 = hmb.astype(F32)
    lo = lax.bitcast_convert_type(hb32[:, :HM_WORDS], U32)
    hi = lax.bitcast_convert_type(hb32[:, HM_WORDS:], U32)
    packed = (hi & jnp.uint32(0xFFFF0000)) | lax.shift_right_logical(lo, jnp.uint32(16))
    for p in range(HM_PIECES):
        hm_ref[pl.ds(p, tt, stride=HM_PIECES), :] = packed[:, p * LANES:(p + 1) * LANES]

    lt = lax.dot_general(wrt_ref[...], hmb, (((1,), (1,)), ((), ())),
                         preferred_element_type=F32) + br_ref[...]
    row8 = _iota_f32((SUBLANES, tt), 0)
    neg = jnp.float32(-jnp.inf)
    lg = jnp.where(row8 < N_EXPERT_GROUPS, lt[0:SUBLANES, :], neg)
    mg = jnp.max(lg, axis=0, keepdims=True)
    p_grp = 1.0 / jnp.sum(jnp.exp(lg - mg), axis=0, keepdims=True)
    gsel = jnp.min(jnp.where(lg == mg, row8, float(SUBLANES)), axis=0, keepdims=True)
    le = jnp.zeros((SUBLANES, tt), F32)
    for g in range(N_EXPERT_GROUPS):
        r0 = EXPERT_ROW0 + g * EXPERTS_PER_GROUP
        le = jnp.where(gsel == float(g), lt[r0:r0 + EXPERTS_PER_GROUP, :], le)
    m1 = jnp.max(le, axis=0, keepdims=True)
    i1 = jnp.min(jnp.where(le == m1, row8, float(SUBLANES)), axis=0, keepdims=True)
    le2 = jnp.where(row8 == i1, neg, le)
    m2 = jnp.max(le2, axis=0, keepdims=True)
    i2 = jnp.min(jnp.where(le2 == m2, row8, float(SUBLANES)), axis=0, keepdims=True)
    e2 = jnp.exp(m2 - m1)
    den = 1.0 + e2
    gate0 = p_grp * (1.0 / den)
    gate1 = p_grp * (e2 / den)
    eid0 = gsel * float(EXPERTS_PER_GROUP) + i1
    eid1 = gsel * float(EXPERTS_PER_GROUP) + i2

    rowe = _iota_f32((N_EXPERTS, tt), 0)
    oh0 = rowe == eid0
    oh1 = rowe == eid1
    ohf = jnp.where(jnp.logical_or(oh0, oh1), 1.0, 0.0)
    before = jnp.dot(ohf.astype(BF16), tri_ref[...], preferred_element_type=F32) + carry[:, 0:1]
    rank0 = jnp.sum(jnp.where(oh0, before, 0.0), axis=0, keepdims=True)
    rank1 = jnp.sum(jnp.where(oh1, before, 0.0), axis=0, keepdims=True)
    carry[...] = carry[...] + jnp.sum(ohf, axis=1, keepdims=True)
    cnt_ref[...] = carry[...]

    route = jnp.where(row8 == 0.0, eid0,
                      jnp.where(row8 == 1.0, eid1,
                                jnp.where(row8 == 2.0, rank0,
                                          jnp.where(row8 == 3.0, rank1, 0.0))))
    route_ref[...] = route.astype(I32)

    tpad = -(-tt // LANES) * LANES
    rowr = _iota_f32((ROUTER_ROWS, tt), 0)
    gt = jnp.where(rowr == 0.0, gate0, jnp.where(rowr == 1.0, gate1, 0.0))
    if tpad != tt:
        gt = jnp.concatenate([gt, jnp.zeros((ROUTER_ROWS, tpad - tt), F32)], axis=1)
    gates_ref[...] = gt.T[0:tt, :]

    @pl.when(j == n_t - 1)
    def _():
        spool_ref[...] = pext[POOL_PAD + tt - POOL_STATE:POOL_PAD + tt, :]
        sconv_ref[...] = cext[CONV_PAD + tt - CONV_STATE:CONV_PAD + tt, :]


def _mixer_call(x, pool_prefix, conv_prefix, pos0, tt, weights, name):
    bsz, seq, _ = x.shape
    n_t = seq // tt
    n_blk = bsz * n_t
    n_rows = bsz * seq
    tri = (lax.broadcasted_iota(I32, (tt, tt), 0) < lax.broadcasted_iota(I32, (tt, tt), 1)).astype(BF16)

    def const(shape):
        return pl.BlockSpec(shape, lambda b, j: (0,) * len(shape))

    def rows(width):
        return pl.BlockSpec((tt, width), lambda b, j: (b * n_t + j, 0))

    in_specs = [
        pl.BlockSpec((None, tt, D_MODEL), lambda b, j: (b, j, 0)),
        pl.BlockSpec((None, POOL_PAD, D_POOL), lambda b, j: (b, 0, 0)),
        pl.BlockSpec((None, CONV_PAD, D_CONV), lambda b, j: (b, 0, 0)),
    ] + [const(w.shape) for w in weights] + [const((tt, tt))]
    out_shape = (
        jax.ShapeDtypeStruct((n_rows, D_MODEL), F32),
        jax.ShapeDtypeStruct((n_rows * HM_PIECES, LANES), U32),
        jax.ShapeDtypeStruct((n_rows, LANES), F32),
        jax.ShapeDtypeStruct((n_blk, SUBLANES, tt), I32),
        jax.ShapeDtypeStruct((N_EXPERTS, LANES), F32),
        jax.ShapeDtypeStruct((1, bsz, POOL_STATE, D_POOL), F32),
        jax.ShapeDtypeStruct((1, bsz, CONV_STATE, D_CONV), F32),
    )
    out_specs = (
        rows(D_MODEL),
        pl.BlockSpec((tt * HM_PIECES, LANES), lambda b, j: (b * n_t + j, 0)),
        rows(LANES),
        pl.BlockSpec((None, SUBLANES, tt), lambda b, j: (b * n_t + j, 0, 0)),
        pl.BlockSpec((N_EXPERTS, LANES), lambda b, j: (0, 0)),
        pl.BlockSpec((None, None, POOL_STATE, D_POOL), lambda b, j: (0, b, 0, 0)),
        pl.BlockSpec((None, None, CONV_STATE, D_CONV), lambda b, j: (0, b, 0, 0)),
    )
    return pl.pallas_call(
        functools.partial(_mixer_kernel, pos0, tt, n_t),
        out_shape=out_shape,
        grid=(bsz, n_t),
        in_specs=in_specs,
        out_specs=out_specs,
        scratch_shapes=[
            pltpu.VMEM((POOL_PAD + tt, D_POOL), F32),
            pltpu.VMEM((CONV_PAD + tt, D_CONV), F32),
            pltpu.VMEM((tt, D_CONV), F32),
            pltpu.VMEM((N_EXPERTS, LANES), F32),
        ],
        compiler_params=pltpu.CompilerParams(
            dimension_semantics=("arbitrary", "arbitrary"), vmem_limit_bytes=VMEM_LIMIT),
        name=name,
    )(x, pool_prefix, conv_prefix, *weights, tri)


def _row_copy_out(hm_ref, xb_ref, sem, t, d):
    src = hm_ref.at[pl.ds(pl.multiple_of(t * HM_PIECES, HM_PIECES), HM_PIECES), :]
    dst = xb_ref.at[pl.ds(pl.multiple_of(d * HM_PIECES, HM_PIECES), HM_PIECES), :]
    return pltpu.make_async_copy(src, dst, sem)


def _dispatch_kernel(dest_ref, hm_ref, xb_in_ref, xb_ref, sem):
    del xb_in_ref
    tc = hm_ref.shape[0] // HM_PIECES

    def start(t, c):
        for k in range(TOP_K):
            _row_copy_out(hm_ref, xb_ref, sem, t, dest_ref[0, k, t]).start(priority=k)
        return c

    lax.fori_loop(0, tc, start, 0, unroll=ISSUE_UNROLL)
    for k in range(TOP_K):
        pltpu.make_async_copy(hm_ref, xb_ref.at[pl.ds(0, tc * HM_PIECES), :], sem).wait()


def _dispatch_call(dest_blocks, hm, xb, name):
    n_rows = hm.shape[0] // HM_PIECES
    tc = TOKEN_TILE
    return pl.pallas_call(
        _dispatch_kernel,
        out_shape=jax.ShapeDtypeStruct(xb.shape, xb.dtype),
        grid=(n_rows // tc,),
        in_specs=[
            pl.BlockSpec((1, TOP_K, tc), lambda i: (i, 0, 0), memory_space=pltpu.SMEM),
            pl.BlockSpec((tc * HM_PIECES, LANES), lambda i: (i, 0)),
            pl.BlockSpec(memory_space=pl.ANY),
        ],
        out_specs=pl.BlockSpec(memory_space=pl.ANY),
        scratch_shapes=[pltpu.SemaphoreType.DMA(())],
        input_output_aliases={2: 0},
        compiler_params=pltpu.CompilerParams(dimension_semantics=("arbitrary",)),
        name=name,
    )(dest_blocks, hm, xb)


def _expert_kernel(te_ref, nu_ref, xb_ref, wg_ref, wu_ref, wd_ref, y_ref, wgb, wub, wdb):
    i = pl.program_id(0)
    e = te_ref[i]
    prev = te_ref[jnp.maximum(i - 1, 0)]

    @pl.when((i == 0) | (e != prev))
    def _():
        wgb[...] = wg_ref[...].astype(BF16)
        wub[...] = wu_ref[...].astype(BF16)
        wdb[...] = wd_ref[...].astype(BF16)

    @pl.when(i < nu_ref[0])
    def _():
        tm = y_ref.shape[0] // Y_PIECES
        words = [xb_ref[pl.ds(p, tm, stride=HM_PIECES), :] for p in range(HM_PIECES)]
        lo = [lax.bitcast_convert_type(lax.shift_left(w, jnp.uint32(16)), F32).astype(BF16) for w in words]
        hi = [lax.bitcast_convert_type(w & jnp.uint32(0xFFFF0000), F32).astype(BF16) for w in words]
        xt = jnp.concatenate(lo + hi, axis=-1)
        g = jnp.dot(xt, wgb[...], preferred_element_type=F32)
        up = jnp.dot(xt, wub[...], preferred_element_type=F32)
        a = (g * jax.nn.sigmoid(g) * up).astype(BF16)
        y = jnp.dot(a, wdb[...], preferred_element_type=F32)
        for p in range(Y_PIECES):
            y_ref[pl.ds(p, tm, stride=Y_PIECES), :] = y[:, p * LANES:(p + 1) * LANES]

    @pl.when(i >= nu_ref[0])
    def _():
        y_ref[...] = jnp.zeros_like(y_ref)


def _expert_call(tile_expert, n_used, xb, w_eg, w_eu, w_ed):
    n_slots = xb.shape[0] // HM_PIECES
    tm = EXPERT_TILE
    grid_spec = pltpu.PrefetchScalarGridSpec(
        num_scalar_prefetch=2,
        grid=(n_slots // tm,),
        in_specs=[
            pl.BlockSpec((tm * HM_PIECES, LANES), lambda i, te, nu: (i, 0)),
            pl.BlockSpec((None, D_MODEL, D_FF), lambda i, te, nu: (te[i], 0, 0)),
            pl.BlockSpec((None, D_MODEL, D_FF), lambda i, te, nu: (te[i], 0, 0)),
            pl.BlockSpec((None, D_FF, D_MODEL), lambda i, te, nu: (te[i], 0, 0)),
        ],
        out_specs=pl.BlockSpec((tm * Y_PIECES, LANES), lambda i, te, nu: (i, 0)),
        scratch_shapes=[
            pltpu.VMEM((D_MODEL, D_FF), BF16),
            pltpu.VMEM((D_MODEL, D_FF), BF16),
            pltpu.VMEM((D_FF, D_MODEL), BF16),
        ],
    )
    return pl.pallas_call(
        _expert_kernel,
        out_shape=jax.ShapeDtypeStruct((n_slots * Y_PIECES, LANES), F32),
        grid_spec=grid_spec,
        compiler_params=pltpu.CompilerParams(
            dimension_semantics=("arbitrary",), vmem_limit_bytes=VMEM_LIMIT),
        name="experts",
    )(tile_expert, n_used, xb, w_eg, w_eu, w_ed)


def _row_copy_in(yb_ref, buf_ref, sem, k, t, d):
    src = yb_ref.at[pl.ds(pl.multiple_of(d * Y_PIECES, Y_PIECES), Y_PIECES), :]
    dst = buf_ref.at[k, pl.ds(pl.multiple_of(t * Y_PIECES, Y_PIECES), Y_PIECES), :]
    return pltpu.make_async_copy(src, dst, sem)


def _combine_kernel(dest_ref, x1_ref, gates_ref, gf_ref, yb_ref, out_ref, buf, sem):
    tc = x1_ref.shape[0]

    def start(t, c):
        for k in range(TOP_K):
            _row_copy_in(yb_ref, buf, sem, k, t, dest_ref[0, k, t]).start(priority=k)
        return c

    lax.fori_loop(0, tc, start, 0, unroll=ISSUE_UNROLL)
    for k in range(TOP_K):
        pltpu.make_async_copy(yb_ref.at[pl.ds(0, tc * Y_PIECES), :], buf.at[k], sem).wait()

    gates = gates_ref[...]
    y = jnp.concatenate(
        [buf[0, pl.ds(p, tc, stride=Y_PIECES), :] * gates[:, 0:1]
         + buf[1, pl.ds(p, tc, stride=Y_PIECES), :] * gates[:, 1:2] for p in range(Y_PIECES)], axis=-1)
    xo = x1_ref[...] + y
    ms = jnp.mean(xo * xo, axis=-1, keepdims=True)
    out_ref[...] = xo * lax.rsqrt(ms + EPS) * gf_ref[...]


def _combine_call(dest_blocks, x1, gates, gf, yb, name):
    n_rows = x1.shape[0]
    tc = TOKEN_TILE
    return pl.pallas_call(
        _combine_kernel,
        out_shape=jax.ShapeDtypeStruct((n_rows, D_MODEL), F32),
        grid=(n_rows // tc,),
        in_specs=[
            pl.BlockSpec((1, TOP_K, tc), lambda i: (i, 0, 0), memory_space=pltpu.SMEM),
            pl.BlockSpec((tc, D_MODEL), lambda i: (i, 0)),
            pl.BlockSpec((tc, LANES), lambda i: (i, 0)),
            pl.BlockSpec((1, D_MODEL), lambda i: (0, 0)),
            pl.BlockSpec(memory_space=pl.ANY),
        ],
        out_specs=pl.BlockSpec((tc, D_MODEL), lambda i: (i, 0)),
        scratch_shapes=[pltpu.VMEM((TOP_K, tc * Y_PIECES, LANES), F32), pltpu.SemaphoreType.DMA(())],
        compiler_params=pltpu.CompilerParams(
            dimension_semantics=("arbitrary",), vmem_limit_bytes=VMEM_LIMIT),
        name=name,
    )(dest_blocks, x1, gates, gf, yb)


def _block_diag2(a, b):
    z = jnp.zeros_like(a)
    return jnp.concatenate([jnp.concatenate([a, z], axis=1), jnp.concatenate([z, b], axis=1)], axis=0)


def kernel(x_prompt, x_sample, state_pool, state_conv, norm1_g, w_in, pool_lin, pool_scale, conv_dw, conv_dw_b, conv_ln_g, conv_ln_b, w_out, b_out, norm2_g, w_rg, b_rg, w_re, b_re, w_eg, w_eu, w_ed, norm_f_g):
    assert norm1_g.shape[0] == 1, "single-layer trunk"
    bsz, seq, _ = x_prompt.shape
    dbsz, dseq, _ = x_sample.shape
    past_len = 1024
    n_prompt = bsz * seq
    n_sample = dbsz * dseq
    n_tokens = n_prompt + n_sample
    assert seq % PROMPT_TILE == 0 and n_prompt % TOKEN_TILE == 0 and n_sample % TOKEN_TILE == 0

    row = lambda a: a.reshape(1, -1)
    router_w = jnp.zeros((ROUTER_ROWS, D_MODEL), F32)
    router_w = router_w.at[0:N_EXPERT_GROUPS].set(w_rg[0].T)
    router_w = router_w.at[EXPERT_ROW0:EXPERT_ROW0 + N_EXPERTS].set(
        jnp.transpose(w_re[0], (0, 2, 1)).reshape(N_EXPERTS, D_MODEL))
    router_b = jnp.zeros((ROUTER_ROWS,), F32)
    router_b = router_b.at[0:N_EXPERT_GROUPS].set(b_rg[0])
    router_b = router_b.at[EXPERT_ROW0:EXPERT_ROW0 + N_EXPERTS].set(b_re[0].reshape(-1))
    conv_w = jnp.concatenate([conv_dw[0], jnp.zeros((1, D_CONV), F32)], axis=0)
    weights = [
        row(norm1_g[0]), w_in[0].astype(BF16),
        _block_diag2(pool_lin[0, 0], pool_lin[0, 1]).astype(BF16),
        _block_diag2(pool_lin[0, 2], pool_lin[0, 3]).astype(BF16),
        row(pool_scale[0]), conv_w, row(conv_dw_b[0]), row(conv_ln_g[0]), row(conv_ln_b[0]),
        w_out[0].astype(BF16), row(b_out[0]), row(norm2_g[0]),
        router_w.astype(BF16), router_b.reshape(ROUTER_ROWS, 1),
    ]

    zpool = jnp.zeros((bsz, POOL_PAD, D_POOL), F32)
    zconv = jnp.zeros((bsz, CONV_PAD, D_CONV), F32)
    ppool = jnp.pad(state_pool[0], ((0, 0), (POOL_PAD - POOL_STATE, 0), (0, 0)))
    pconv = jnp.pad(state_conv[0], ((0, 0), (CONV_PAD - CONV_STATE, 0), (0, 0)))

    x1_p, hm_p, gates_p, route_p, cnt_p, sp_p, sc_p = _mixer_call(
        x_prompt, zpool, zconv, 0, PROMPT_TILE, weights, "mixer_prompt")
    x1_s, hm_s, gates_s, route_s, cnt_s, sp_s, sc_s = _mixer_call(
        x_sample, ppool, pconv, past_len, dseq, weights, "mixer_sample")

    tm = EXPERT_TILE
    cnt_p = cnt_p[:, 0].astype(I32)
    cnt_s = cnt_s[:, 0].astype(I32)
    padded = (cnt_p + cnt_s + tm - 1) // tm * tm
    pad_end = jnp.cumsum(padded)
    pad_start = pad_end - padded

    def dest_blocks(route, base):
        eid = jnp.stack([route[:, k, :].reshape(-1) for k in range(TOP_K)])
        rank = jnp.stack([route[:, TOP_K + k, :].reshape(-1) for k in range(TOP_K)])
        experts = jnp.arange(N_EXPERTS, dtype=I32)
        dest = rank + jnp.sum(jnp.where(eid[..., None] == experts, base, 0), axis=-1)
        return dest.reshape(TOP_K, -1, TOKEN_TILE).transpose(1, 0, 2)

    dest_p = dest_blocks(route_p, pad_start)
    dest_s = dest_blocks(route_s, pad_start + cnt_p)
    n_tiles = -(-(n_tokens * TOP_K + N_EXPERTS * (tm - 1)) // tm)
    n_used = (pad_end[-1] // tm).astype(I32)
    tile_ids = jnp.minimum(jnp.arange(n_tiles, dtype=I32), n_used - 1)
    tile_expert = jnp.minimum(
        jnp.sum(pad_end[None, :] <= (tile_ids * tm)[:, None], axis=1), N_EXPERTS - 1).astype(I32)

    xb = jnp.zeros((n_tiles * tm * HM_PIECES, LANES), U32)
    xb = _dispatch_call(dest_p, hm_p, xb, "dispatch_prompt")
    xb = _dispatch_call(dest_s, hm_s, xb, "dispatch_sample")
    yb = _expert_call(tile_expert, n_used.reshape(1), xb, w_eg[0], w_eu[0], w_ed[0])
    gf = row(norm_f_g)
    y_p = _combine_call(dest_p, x1_p, gates_p, gf, yb, "combine_prompt")
    y_s = _combine_call(dest_s, x1_s, gates_s, gf, yb, "combine_sample")

    return (y_p.reshape(bsz, seq, D_MODEL), y_s.reshape(dbsz, dseq, D_MODEL),
            sp_p, sc_p, sp_s, sc_s)
```

```python
import functools

import jax
import jax.numpy as jnp
from jax import lax
from jax.experimental import pallas as pl
from jax.experimental.pallas import tpu as pltpu

F32 = jnp.float32
BF16 = jnp.bfloat16
U32 = jnp.uint32
I32 = jnp.int32

D_MODEL = 1024
D_POOL = 512
D_CONV = 512
D_IN = D_POOL + 2 * D_CONV
POOL_WINDOWS = (2, 4, 8, 16)
POOL_GROUP_DIM = D_POOL // len(POOL_WINDOWS)
POOL_STATE = max(POOL_WINDOWS) - 1
CONV_WIDTH = 31
CONV_STATE = CONV_WIDTH - 1
N_EXPERT_GROUPS = 4
EXPERTS_PER_GROUP = 8
N_EXPERTS = N_EXPERT_GROUPS * EXPERTS_PER_GROUP
TOP_K = 2
D_FF = D_MODEL // 2
EPS = 1e-6

SUBLANES = 8
LANES = 128
POOL_PAD = 32
CONV_PAD = 32
ROUTER_ROWS = LANES
EXPERT_ROW0 = SUBLANES
CONV_CHUNK = 64
CONV_BLOCKS = D_CONV // LANES
HM_WORDS = D_MODEL // 2
HM_PIECES = HM_WORDS // LANES
Y_PIECES = D_MODEL // LANES

PROMPT_TILE = 512
TOKEN_TILE = 512
EXPERT_TILE = 256
ISSUE_UNROLL = 8
VMEM_LIMIT = 48 * 1024 * 1024


def _iota_f32(shape, axis):
    return lax.broadcasted_iota(I32, shape, axis).astype(F32)


def _mixer_kernel(pos0, tt, n_t,
                  x_ref, pp_ref, cp_ref, g1_ref, win_ref, bd01_ref, bd23_ref, pscale_ref,
                  cw_ref, cb_ref, lng_ref, lnb_ref, wout_ref, bout_ref, g2_ref, wrt_ref, br_ref,
                  tri_ref,
                  x1_ref, hm_ref, gates_ref, route_ref, cnt_ref, spool_ref, sconv_ref,
                  pext, cext, hbuf, carry, lv2, lv4, lv8, csh):
    b = pl.program_id(0)
    j = pl.program_id(1)

    @pl.when((b == 0) & (j == 0))
    def _():
        carry[...] = jnp.zeros_like(carry)

    @pl.when(j == 0)
    def _():
        pext[0:POOL_PAD, :] = pp_ref[...]
        for lb in range(CONV_BLOCKS):
            cext[lb, 0:CONV_PAD, :] = cp_ref[:, lb * LANES:(lb + 1) * LANES]

    @pl.when(j > 0)
    def _():
        pext[0:POOL_PAD, :] = pext[tt:tt + POOL_PAD, :]
        cext[:, 0:CONV_PAD, :] = cext[:, tt:tt + CONV_PAD, :]

    x = x_ref[...]
    ms = jnp.mean(x * x, axis=-1, keepdims=True)
    h = (x * lax.rsqrt(ms + EPS) * g1_ref[...]).astype(BF16)
    z = jnp.dot(h, win_ref[...], preferred_element_type=F32)
    u = z[:, :D_POOL]
    pext[POOL_PAD:POOL_PAD + tt, :] = u
    v = z[:, D_POOL:D_POOL + D_CONV] * jax.nn.sigmoid(z[:, D_POOL + D_CONV:])
    for lb in range(CONV_BLOCKS):
        cext[lb, CONV_PAD:CONV_PAD + tt, :] = v[:, lb * LANES:(lb + 1) * LANES]

    pos = pos0 + j * tt + lax.broadcasted_iota(I32, (tt, 1), 0)
    end = POOL_PAD + tt
    gd = POOL_GROUP_DIM
    lv2[8:end, :] = pext[8:end, :] + pext[7:end - 1, :]
    lv4[16:end, gd:] = lv2[16:end, gd:] + lv2[14:end - 2, gd:]
    lv8[24:end, 2 * gd:] = lv4[24:end, 2 * gd:] + lv4[20:end - 4, 2 * gd:]
    sums = [lv2[POOL_PAD:end, 0:gd], lv4[POOL_PAD:end, gd:2 * gd], lv8[POOL_PAD:end, 2 * gd:3 * gd],
            lv8[POOL_PAD:end, 3 * gd:] + lv8[POOL_PAD - 8:end - 8, 3 * gd:]]
    parts = []
    for g, w in enumerate(POOL_WINDOWS):
        sl = slice(g * gd, (g + 1) * gd)
        cnt = jnp.minimum(pos + 1, w).astype(F32)
        parts.append(sums[g] / cnt - u[:, sl])
    d = jnp.concatenate(parts, axis=-1).astype(BF16)
    half = D_POOL // 2
    yp = jnp.concatenate(
        [jnp.dot(d[:, :half], bd01_ref[...], preferred_element_type=F32),
         jnp.dot(d[:, half:], bd23_ref[...], preferred_element_type=F32)], axis=-1)
    yp = yp * pscale_ref[...]

    n_sh = CONV_PAD + tt - SUBLANES
    for s in range(1, SUBLANES):
        csh[s - 1, :, 0:n_sh, :] = cext[:, s:s + n_sh, :]

    for lb in range(CONV_BLOCKS):
        lanes = slice(lb * LANES, (lb + 1) * LANES)
        wts = [jnp.broadcast_to(cw_ref[k:k + 1, lanes], (CONV_CHUNK, LANES)) for k in range(CONV_WIDTH)]

        def conv_chunk(c, carry_, lb=lb, lanes=lanes, wts=wts):
            r0 = pl.multiple_of(c * CONV_CHUNK, CONV_CHUNK)
            acc = None
            for k in range(CONV_WIDTH):
                q, s = divmod(CONV_PAD - CONV_STATE + k, SUBLANES)
                rows = pl.ds(pl.multiple_of(r0 + q * SUBLANES, SUBLANES), CONV_CHUNK)
                tap = (cext[lb, rows, :] if s == 0 else csh[s - 1, lb, rows, :]) * wts[k]
                acc = tap if acc is None else acc + tap
            hbuf[pl.ds(r0, CONV_CHUNK), lanes] = acc
            return carry_

        lax.fori_loop(0, tt // CONV_CHUNK, conv_chunk, 0)
    hc = hbuf[...] + cb_ref[...]
    mu = jnp.mean(hc, axis=-1, keepdims=True)
    var = jnp.mean(jnp.square(hc - mu), axis=-1, keepdims=True)
    yln = (hc - mu) * lax.rsqrt(var + EPS) * lng_ref[...] + lnb_ref[...]
    yc = yln * jax.nn.sigmoid(yln)

    o = (jnp.dot(yp.astype(BF16), wout_ref[0:D_POOL, :], preferred_element_type=F32)
         + jnp.dot(yc.astype(BF16), wout_ref[D_POOL:, :], preferred_element_type=F32))
    x1 = x + (o + bout_ref[...])
    x1_ref[...] = x1

    ms2 = jnp.mean(x1 * x1, axis=-1, keepdims=True)
    hmb = (x1 * lax.rsqrt(ms2 + EPS) * g2_ref[...]).astype(BF16)
    hb32 = hmb.astype(F32)
    lo = lax.bitcast_convert_type(hb32[:, :HM_WORDS], U32)
    hi = lax.bitcast_convert_type(hb32[:, HM_WORDS:], U32)
    packed = (hi & jnp.uint32(0xFFFF0000)) | lax.shift_right_logical(lo, jnp.uint32(16))
    for p in range(HM_PIECES):
        hm_ref[pl.ds(p, tt, stride=HM_PIECES), :] = packed[:, p * LANES:(p + 1) * LANES]

    lt = lax.dot_general(wrt_ref[...], hmb, (((1,), (1,)), ((), ())),
                         preferred_element_type=F32) + br_ref[...]
    row8 = _iota_f32((SUBLANES, tt), 0)
    neg = jnp.float32(-jnp.inf)
    lg = jnp.where(row8 < N_EXPERT_GROUPS, lt[0:SUBLANES, :], neg)
    mg = jnp.max(lg, axis=0, keepdims=True)
    p_grp = 1.0 / jnp.sum(jnp.exp(lg - mg), axis=0, keepdims=True)
    gsel = jnp.min(jnp.where(lg == mg, row8, float(SUBLANES)), axis=0, keepdims=True)
    le = jnp.zeros((SUBLANES, tt), F32)
    for g in range(N_EXPERT_GROUPS):
        r0 = EXPERT_ROW0 + g * EXPERTS_PER_GROUP
        le = jnp.where(gsel == float(g), lt[r0:r0 + EXPERTS_PER_GROUP, :], le)
    m1 = jnp.max(le, axis=0, keepdims=True)
    i1 = jnp.min(jnp.where(le == m1, row8, float(SUBLANES)), axis=0, keepdims=True)
    le2 = jnp.where(row8 == i1, neg, le)
    m2 = jnp.max(le2, axis=0, keepdims=True)
    i2 = jnp.min(jnp.where(le2 == m2, row8, float(SUBLANES)), axis=0, keepdims=True)
    e2 = jnp.exp(m2 - m1)
    den = 1.0 + e2
    gate0 = p_grp * (1.0 / den)
    gate1 = p_grp * (e2 / den)
    eid0 = gsel * float(EXPERTS_PER_GROUP) + i1
    eid1 = gsel * float(EXPERTS_PER_GROUP) + i2

    rowe = _iota_f32((N_EXPERTS, tt), 0)
    oh0 = rowe == eid0
    oh1 = rowe == eid1
    ohf = jnp.where(jnp.logical_or(oh0, oh1), 1.0, 0.0)
    before = jnp.dot(ohf.astype(BF16), tri_ref[...], preferred_element_type=F32) + carry[:, 0:1]
    rank0 = jnp.sum(jnp.where(oh0, before, 0.0), axis=0, keepdims=True)
    rank1 = jnp.sum(jnp.where(oh1, before, 0.0), axis=0, keepdims=True)
    carry[...] = carry[...] + jnp.sum(ohf, axis=1, keepdims=True)
    cnt_ref[...] = carry[...]

    route = jnp.where(row8 == 0.0, eid0,
                      jnp.where(row8 == 1.0, eid1,
                                jnp.where(row8 == 2.0, rank0,
                                          jnp.where(row8 == 3.0, rank1, 0.0))))
    route_ref[...] = route.astype(I32)

    tpad = -(-tt // LANES) * LANES
    rowr = _iota_f32((ROUTER_ROWS, tt), 0)
    gt = jnp.where(rowr == 0.0, gate0, jnp.where(rowr == 1.0, gate1, 0.0))
    if tpad != tt:
        gt = jnp.concatenate([gt, jnp.zeros((ROUTER_ROWS, tpad - tt), F32)], axis=1)
    gates_ref[...] = gt.T[0:tt, :]

    @pl.when(j == n_t - 1)
    def _():
        spool_ref[...] = pext[POOL_PAD + tt - POOL_STATE:POOL_PAD + tt, :]
        sconv_ref[...] = jnp.concatenate(
            [cext[lb, CONV_PAD + tt - CONV_STATE:CONV_PAD + tt, :] for lb in range(CONV_BLOCKS)], axis=-1)


def _mixer_call(x, pool_prefix, conv_prefix, pos0, tt, weights, name):
    bsz, seq, _ = x.shape
    n_t = seq // tt
    n_blk = bsz * n_t
    n_rows = bsz * seq
    tri = (lax.broadcasted_iota(I32, (tt, tt), 0) < lax.broadcasted_iota(I32, (tt, tt), 1)).astype(BF16)

    def const(shape):
        return pl.BlockSpec(shape, lambda b, j: (0,) * len(shape))

    def rows(width):
        return pl.BlockSpec((tt, width), lambda b, j: (b * n_t + j, 0))

    in_specs = [
        pl.BlockSpec((None, tt, D_MODEL), lambda b, j: (b, j, 0)),
        pl.BlockSpec((None, POOL_PAD, D_POOL), lambda b, j: (b, 0, 0)),
        pl.BlockSpec((None, CONV_PAD, D_CONV), lambda b, j: (b, 0, 0)),
    ] + [const(w.shape) for w in weights] + [const((tt, tt))]
    out_shape = (
        jax.ShapeDtypeStruct((n_rows, D_MODEL), F32),
        jax.ShapeDtypeStruct((n_rows * HM_PIECES, LANES), U32),
        jax.ShapeDtypeStruct((n_rows, LANES), F32),
        jax.ShapeDtypeStruct((n_blk, SUBLANES, tt), I32),
        jax.ShapeDtypeStruct((N_EXPERTS, LANES), F32),
        jax.ShapeDtypeStruct((1, bsz, POOL_STATE, D_POOL), F32),
        jax.ShapeDtypeStruct((1, bsz, CONV_STATE, D_CONV), F32),
    )
    out_specs = (
        rows(D_MODEL),
        pl.BlockSpec((tt * HM_PIECES, LANES), lambda b, j: (b * n_t + j, 0)),
        rows(LANES),
        pl.BlockSpec((None, SUBLANES, tt), lambda b, j: (b * n_t + j, 0, 0)),
        pl.BlockSpec((N_EXPERTS, LANES), lambda b, j: (0, 0)),
        pl.BlockSpec((None, None, POOL_STATE, D_POOL), lambda b, j: (0, b, 0, 0)),
        pl.BlockSpec((None, None, CONV_STATE, D_CONV), lambda b, j: (0, b, 0, 0)),
    )
    return pl.pallas_call(
        functools.partial(_mixer_kernel, pos0, tt, n_t),
        out_shape=out_shape,
        grid=(bsz, n_t),
        in_specs=in_specs,
        out_specs=out_specs,
        scratch_shapes=[
            pltpu.VMEM((POOL_PAD + tt, D_POOL), F32),
            pltpu.VMEM((CONV_BLOCKS, CONV_PAD + tt, LANES), F32),
            pltpu.VMEM((tt, D_CONV), F32),
            pltpu.VMEM((N_EXPERTS, LANES), F32),
            pltpu.VMEM((POOL_PAD + tt, D_POOL), F32),
            pltpu.VMEM((POOL_PAD + tt, D_POOL), F32),
            pltpu.VMEM((POOL_PAD + tt, D_POOL), F32),
            pltpu.VMEM((SUBLANES - 1, CONV_BLOCKS, CONV_PAD + tt, LANES), F32),
        ],
        compiler_params=pltpu.CompilerParams(
            dimension_semantics=("arbitrary", "arbitrary"), vmem_limit_bytes=VMEM_LIMIT),
        name=name,
    )(x, pool_prefix, conv_prefix, *weights, tri)


def _row_copy_out(hm_ref, xb_ref, sem, t, d):
    src = hm_ref.at[pl.ds(pl.multiple_of(t * HM_PIECES, HM_PIECES), HM_PIECES), :]
    dst = xb_ref.at[pl.ds(pl.multiple_of(d * HM_PIECES, HM_PIECES), HM_PIECES), :]
    return pltpu.make_async_copy(src, dst, sem)


def _dispatch_kernel(dest_ref, hm_ref, xb_in_ref, xb_ref, sem):
    del xb_in_ref
    tc = hm_ref.shape[0] // HM_PIECES

    def start(t, c):
        for k in range(TOP_K):
            _row_copy_out(hm_ref, xb_ref, sem, t, dest_ref[0, k, t]).start(priority=k)
        return c

    lax.fori_loop(0, tc, start, 0, unroll=ISSUE_UNROLL)
    for k in range(TOP_K):
        pltpu.make_async_copy(hm_ref, xb_ref.at[pl.ds(0, tc * HM_PIECES), :], sem).wait()


def _dispatch_call(dest_blocks, hm, xb, name):
    n_rows = hm.shape[0] // HM_PIECES
    tc = TOKEN_TILE
    return pl.pallas_call(
        _dispatch_kernel,
        out_shape=jax.ShapeDtypeStruct(xb.shape, xb.dtype),
        grid=(n_rows // tc,),
        in_specs=[
            pl.BlockSpec((1, TOP_K, tc), lambda i: (i, 0, 0), memory_space=pltpu.SMEM),
            pl.BlockSpec((tc * HM_PIECES, LANES), lambda i: (i, 0)),
            pl.BlockSpec(memory_space=pl.ANY),
        ],
        out_specs=pl.BlockSpec(memory_space=pl.ANY),
        scratch_shapes=[pltpu.SemaphoreType.DMA(())],
        input_output_aliases={2: 0},
        compiler_params=pltpu.CompilerParams(dimension_semantics=("arbitrary",)),
        name=name,
    )(dest_blocks, hm, xb)


def _expert_kernel(te_ref, nu_ref, xb_ref, wg_ref, wu_ref, wd_ref, y_ref, wgb, wub, wdb):
    i = pl.program_id(0)
    e = te_ref[i]
    prev = te_ref[jnp.maximum(i - 1, 0)]

    @pl.when((i == 0) | (e != prev))
    def _():
        wgb[...] = wg_ref[...].astype(BF16)
        wub[...] = wu_ref[...].astype(BF16)
        wdb[...] = wd_ref[...].astype(BF16)

    @pl.when(i < nu_ref[0])
    def _():
        tm = y_ref.shape[0] // Y_PIECES
        words = [xb_ref[pl.ds(p, tm, stride=HM_PIECES), :] for p in range(HM_PIECES)]
        lo = [lax.bitcast_convert_type(lax.shift_left(w, jnp.uint32(16)), F32).astype(BF16) for w in words]
        hi = [lax.bitcast_convert_type(w & jnp.uint32(0xFFFF0000), F32).astype(BF16) for w in words]
        xt = jnp.concatenate(lo + hi, axis=-1)
        g = jnp.dot(xt, wgb[...], preferred_element_type=F32)
        up = jnp.dot(xt, wub[...], preferred_element_type=F32)
        a = (g * jax.nn.sigmoid(g) * up).astype(BF16)
        y = jnp.dot(a, wdb[...], preferred_element_type=F32)
        for p in range(Y_PIECES):
            y_ref[pl.ds(p, tm, stride=Y_PIECES), :] = y[:, p * LANES:(p + 1) * LANES]

    @pl.when(i >= nu_ref[0])
    def _():
        y_ref[...] = jnp.zeros_like(y_ref)


def _expert_call(tile_expert, n_used, xb, w_eg, w_eu, w_ed):
    n_slots = xb.shape[0] // HM_PIECES
    tm = EXPERT_TILE
    grid_spec = pltpu.PrefetchScalarGridSpec(
        num_scalar_prefetch=2,
        grid=(n_slots // tm,),
        in_specs=[
            pl.BlockSpec((tm * HM_PIECES, LANES), lambda i, te, nu: (i, 0)),
            pl.BlockSpec((None, D_MODEL, D_FF), lambda i, te, nu: (te[i], 0, 0)),
            pl.BlockSpec((None, D_MODEL, D_FF), lambda i, te, nu: (te[i], 0, 0)),
            pl.BlockSpec((None, D_FF, D_MODEL), lambda i, te, nu: (te[i], 0, 0)),
        ],
        out_specs=pl.BlockSpec((tm * Y_PIECES, LANES), lambda i, te, nu: (i, 0)),
        scratch_shapes=[
            pltpu.VMEM((D_MODEL, D_FF), BF16),
            pltpu.VMEM((D_MODEL, D_FF), BF16),
            pltpu.VMEM((D_FF, D_MODEL), BF16),
        ],
    )
    return pl.pallas_call(
        _expert_kernel,
        out_shape=jax.ShapeDtypeStruct((n_slots * Y_PIECES, LANES), F32),
        grid_spec=grid_spec,
        compiler_params=pltpu.CompilerParams(
            dimension_semantics=("arbitrary",), vmem_limit_bytes=VMEM_LIMIT),
        name="experts",
    )(tile_expert, n_used, xb, w_eg, w_eu, w_ed)


def _row_copy_in(yb_ref, buf_ref, sem, k, t, d):
    src = yb_ref.at[pl.ds(pl.multiple_of(d * Y_PIECES, Y_PIECES), Y_PIECES), :]
    dst = buf_ref.at[k, pl.ds(pl.multiple_of(t * Y_PIECES, Y_PIECES), Y_PIECES), :]
    return pltpu.make_async_copy(src, dst, sem)


def _combine_kernel(dest_ref, x1_ref, gates_ref, gf_ref, yb_ref, out_ref, buf, sem):
    tc = x1_ref.shape[0]

    def start(t, c):
        for k in range(TOP_K):
            _row_copy_in(yb_ref, buf, sem, k, t, dest_ref[0, k, t]).start(priority=k)
        return c

    lax.fori_loop(0, tc, start, 0, unroll=ISSUE_UNROLL)
    for k in range(TOP_K):
        pltpu.make_async_copy(yb_ref.at[pl.ds(0, tc * Y_PIECES), :], buf.at[k], sem).wait()

    gates = gates_ref[...]
    y = jnp.concatenate(
        [buf[0, pl.ds(p, tc, stride=Y_PIECES), :] * gates[:, 0:1]
         + buf[1, pl.ds(p, tc, stride=Y_PIECES), :] * gates[:, 1:2] for p in range(Y_PIECES)], axis=-1)
    xo = x1_ref[...] + y
    ms = jnp.mean(xo * xo, axis=-1, keepdims=True)
    out_ref[...] = xo * lax.rsqrt(ms + EPS) * gf_ref[...]


def _combine_call(dest_blocks, x1, gates, gf, yb, name):
    n_rows = x1.shape[0]
    tc = TOKEN_TILE
    return pl.pallas_call(
        _combine_kernel,
        out_shape=jax.ShapeDtypeStruct((n_rows, D_MODEL), F32),
        grid=(n_rows // tc,),
        in_specs=[
            pl.BlockSpec((1, TOP_K, tc), lambda i: (i, 0, 0), memory_space=pltpu.SMEM),
            pl.BlockSpec((tc, D_MODEL), lambda i: (i, 0)),
            pl.BlockSpec((tc, LANES), lambda i: (i, 0)),
            pl.BlockSpec((1, D_MODEL), lambda i: (0, 0)),
            pl.BlockSpec(memory_space=pl.ANY),
        ],
        out_specs=pl.BlockSpec((tc, D_MODEL), lambda i: (i, 0)),
        scratch_shapes=[pltpu.VMEM((TOP_K, tc * Y_PIECES, LANES), F32), pltpu.SemaphoreType.DMA(())],
        compiler_params=pltpu.CompilerParams(
            dimension_semantics=("arbitrary",), vmem_limit_bytes=VMEM_LIMIT),
        name=name,
    )(dest_blocks, x1, gates, gf, yb)


def _block_diag2(a, b):
    z = jnp.zeros_like(a)
    return jnp.concatenate([jnp.concatenate([a, z], axis=1), jnp.concatenate([z, b], axis=1)], axis=0)


def kernel(x_prompt, x_sample, state_pool, state_conv, norm1_g, w_in, pool_lin, pool_scale, conv_dw, conv_dw_b, conv_ln_g, conv_ln_b, w_out, b_out, norm2_g, w_rg, b_rg, w_re, b_re, w_eg, w_eu, w_ed, norm_f_g):
    assert norm1_g.shape[0] == 1, "single-layer trunk"
    bsz, seq, _ = x_prompt.shape
    dbsz, dseq, _ = x_sample.shape
    past_len = 1024
    n_prompt = bsz * seq
    n_sample = dbsz * dseq
    n_tokens = n_prompt + n_sample
    assert seq % PROMPT_TILE == 0 and n_prompt % TOKEN_TILE == 0 and n_sample % TOKEN_TILE == 0

    row = lambda a: a.reshape(1, -1)
    router_w = jnp.zeros((ROUTER_ROWS, D_MODEL), F32)
    router_w = router_w.at[0:N_EXPERT_GROUPS].set(w_rg[0].T)
    router_w = router_w.at[EXPERT_ROW0:EXPERT_ROW0 + N_EXPERTS].set(
        jnp.transpose(w_re[0], (0, 2, 1)).reshape(N_EXPERTS, D_MODEL))
    router_b = jnp.zeros((ROUTER_ROWS,), F32)
    router_b = router_b.at[0:N_EXPERT_GROUPS].set(b_rg[0])
    router_b = router_b.at[EXPERT_ROW0:EXPERT_ROW0 + N_EXPERTS].set(b_re[0].reshape(-1))
    conv_w = jnp.concatenate([conv_dw[0], jnp.zeros((1, D_CONV), F32)], axis=0)
    weights = [
        row(norm1_g[0]), w_in[0].astype(BF16),
        _block_diag2(pool_lin[0, 0], pool_lin[0, 1]).astype(BF16),
        _block_diag2(pool_lin[0, 2], pool_lin[0, 3]).astype(BF16),
        row(pool_scale[0]), conv_w, row(conv_dw_b[0]), row(conv_ln_g[0]), row(conv_ln_b[0]),
        w_out[0].astype(BF16), row(b_out[0]), row(norm2_g[0]),
        router_w.astype(BF16), router_b.reshape(ROUTER_ROWS, 1),
    ]

    zpool = jnp.zeros((bsz, POOL_PAD, D_POOL), F32)
    zconv = jnp.zeros((bsz, CONV_PAD, D_CONV), F32)
    ppool = jnp.pad(state_pool[0], ((0, 0), (POOL_PAD - POOL_STATE, 0), (0, 0)))
    pconv = jnp.pad(state_conv[0], ((0, 0), (CONV_PAD - CONV_STATE, 0), (0, 0)))

    x1_p, hm_p, gates_p, route_p, cnt_p, sp_p, sc_p = _mixer_call(
        x_prompt, zpool, zconv, 0, PROMPT_TILE, weights, "mixer_prompt")
    x1_s, hm_s, gates_s, route_s, cnt_s, sp_s, sc_s = _mixer_call(
        x_sample, ppool, pconv, past_len, dseq, weights, "mixer_sample")

    tm = EXPERT_TILE
    cnt_p = cnt_p[:, 0].astype(I32)
    cnt_s = cnt_s[:, 0].astype(I32)
    padded = (cnt_p + cnt_s + tm - 1) // tm * tm
    pad_end = jnp.cumsum(padded)
    pad_start = pad_end - padded

    def dest_blocks(route, base):
        eid = jnp.stack([route[:, k, :].reshape(-1) for k in range(TOP_K)])
        rank = jnp.stack([route[:, TOP_K + k, :].reshape(-1) for k in range(TOP_K)])
        experts = jnp.arange(N_EXPERTS, dtype=I32)
        dest = rank + jnp.sum(jnp.where(eid[..., None] == experts, base, 0), axis=-1)
        return dest.reshape(TOP_K, -1, TOKEN_TILE).transpose(1, 0, 2)

    dest_p = dest_blocks(route_p, pad_start)
    dest_s = dest_blocks(route_s, pad_start + cnt_p)
    n_tiles = -(-(n_tokens * TOP_K + N_EXPERTS * (tm - 1)) // tm)
    n_used = (pad_end[-1] // tm).astype(I32)
    tile_ids = jnp.minimum(jnp.arange(n_tiles, dtype=I32), n_used - 1)
    tile_expert = jnp.minimum(
        jnp.sum(pad_end[None, :] <= (tile_ids * tm)[:, None], axis=1), N_EXPERTS - 1).astype(I32)

    xb = jnp.zeros((n_tiles * tm * HM_PIECES, LANES), U32)
    xb = _dispatch_call(dest_p, hm_p, xb, "dispatch_prompt")
    xb = _dispatch_call(dest_s, hm_s, xb, "dispatch_sample")
    yb = _expert_call(tile_expert, n_used.reshape(1), xb, w_eg[0], w_eu[0], w_ed[0])
    gf = row(norm_f_g)
    y_p = _combine_call(dest_p, x1_p, gates_p, gf, yb, "combine_prompt")
    y_s = _combine_call(dest_s, x1_s, gates_s, gf, yb, "combine_sample")

    return (y_p.reshape(bsz, seq, D_MODEL), y_s.reshape(dbsz, dseq, D_MODEL),
            sp_p, sc_p, sp_s, sc_s)
```

```python
import functools

import jax
import jax.numpy as jnp
from jax import lax
from jax.experimental import pallas as pl
from jax.experimental.pallas import tpu as pltpu

F32 = jnp.float32
BF16 = jnp.bfloat16
U32 = jnp.uint32
I32 = jnp.int32

D_MODEL = 1024
D_POOL = 512
D_CONV = 512
D_IN = D_POOL + 2 * D_CONV
POOL_WINDOWS = (2, 4, 8, 16)
POOL_GROUP_DIM = D_POOL // len(POOL_WINDOWS)
POOL_STATE = max(POOL_WINDOWS) - 1
CONV_WIDTH = 31
CONV_STATE = CONV_WIDTH - 1
N_EXPERT_GROUPS = 4
EXPERTS_PER_GROUP = 8
N_EXPERTS = N_EXPERT_GROUPS * EXPERTS_PER_GROUP
TOP_K = 2
D_FF = D_MODEL // 2
EPS = 1e-6

SUBLANES = 8
LANES = 128
POOL_PAD = 32
CONV_PAD = 32
ROUTER_ROWS = LANES
EXPERT_ROW0 = SUBLANES
CONV_CHUNK = 64
CONV_BLOCKS = D_CONV // LANES
HM_WORDS = D_MODEL // 2
HM_PIECES = HM_WORDS // LANES
Y_PIECES = D_MODEL // LANES

PROMPT_TILE = 512
TOKEN_TILE = 512
EXPERT_TILE = 256
COMBINE_CHUNK = 64
VMEM_LIMIT = 48 * 1024 * 1024


def _iota_f32(shape, axis):
    return lax.broadcasted_iota(I32, shape, axis).astype(F32)


def _mixer_kernel(pos0, tt, n_t,
                  x_ref, pp_ref, cp_ref, g1_ref, win_ref, bd01_ref, bd23_ref, pscale_ref,
                  cw_ref, cb_ref, lng_ref, lnb_ref, wout_ref, bout_ref, g2_ref, wrt_ref, br_ref,
                  tri_ref,
                  x1_ref, hm_ref, gates_ref, route_ref, cnt_ref, spool_ref, sconv_ref,
                  pext, cext, hbuf, carry, lv2, lv4, lv8, csh):
    b = pl.program_id(0)
    j = pl.program_id(1)

    @pl.when((b == 0) & (j == 0))
    def _():
        carry[...] = jnp.zeros_like(carry)

    @pl.when(j == 0)
    def _():
        pext[0:POOL_PAD, :] = pp_ref[...]
        for lb in range(CONV_BLOCKS):
            cext[lb, 0:CONV_PAD, :] = cp_ref[:, lb * LANES:(lb + 1) * LANES]

    @pl.when(j > 0)
    def _():
        pext[0:POOL_PAD, :] = pext[tt:tt + POOL_PAD, :]
        cext[:, 0:CONV_PAD, :] = cext[:, tt:tt + CONV_PAD, :]

    x = x_ref[...]
    ms = jnp.mean(x * x, axis=-1, keepdims=True)
    h = (x * lax.rsqrt(ms + EPS) * g1_ref[...]).astype(BF16)
    z = jnp.dot(h, win_ref[...], preferred_element_type=F32)
    u = z[:, :D_POOL]
    pext[POOL_PAD:POOL_PAD + tt, :] = u
    v = z[:, D_POOL:D_POOL + D_CONV] * jax.nn.sigmoid(z[:, D_POOL + D_CONV:])
    for lb in range(CONV_BLOCKS):
        cext[lb, CONV_PAD:CONV_PAD + tt, :] = v[:, lb * LANES:(lb + 1) * LANES]

    pos = pos0 + j * tt + lax.broadcasted_iota(I32, (tt, 1), 0)
    end = POOL_PAD + tt
    gd = POOL_GROUP_DIM
    lv2[8:end, :] = pext[8:end, :] + pext[7:end - 1, :]
    lv4[16:end, gd:] = lv2[16:end, gd:] + lv2[14:end - 2, gd:]
    lv8[24:end, 2 * gd:] = lv4[24:end, 2 * gd:] + lv4[20:end - 4, 2 * gd:]
    sums = [lv2[POOL_PAD:end, 0:gd], lv4[POOL_PAD:end, gd:2 * gd], lv8[POOL_PAD:end, 2 * gd:3 * gd],
            lv8[POOL_PAD:end, 3 * gd:] + lv8[POOL_PAD - 8:end - 8, 3 * gd:]]
    parts = []
    for g, w in enumerate(POOL_WINDOWS):
        sl = slice(g * gd, (g + 1) * gd)
        cnt = jnp.minimum(pos + 1, w).astype(F32)
        parts.append(sums[g] / cnt - u[:, sl])
    d = jnp.concatenate(parts, axis=-1).astype(BF16)
    half = D_POOL // 2
    yp = jnp.concatenate(
        [jnp.dot(d[:, :half], bd01_ref[...], preferred_element_type=F32),
         jnp.dot(d[:, half:], bd23_ref[...], preferred_element_type=F32)], axis=-1)
    yp = yp * pscale_ref[...]

    n_sh = CONV_PAD + tt - SUBLANES
    for s in range(1, SUBLANES):
        csh[s - 1, :, 0:n_sh, :] = cext[:, s:s + n_sh, :]

    for lb in range(CONV_BLOCKS):
        lanes = slice(lb * LANES, (lb + 1) * LANES)
        wts = [jnp.broadcast_to(cw_ref[k:k + 1, lanes], (CONV_CHUNK, LANES)) for k in range(CONV_WIDTH)]

        def conv_chunk(c, carry_, lb=lb, lanes=lanes, wts=wts):
            r0 = pl.multiple_of(c * CONV_CHUNK, CONV_CHUNK)
            acc = None
            for k in range(CONV_WIDTH):
                q, s = divmod(CONV_PAD - CONV_STATE + k, SUBLANES)
                rows = pl.ds(pl.multiple_of(r0 + q * SUBLANES, SUBLANES), CONV_CHUNK)
                tap = (cext[lb, rows, :] if s == 0 else csh[s - 1, lb, rows, :]) * wts[k]
                acc = tap if acc is None else acc + tap
            hbuf[pl.ds(r0, CONV_CHUNK), lanes] = acc
            return carry_

        lax.fori_loop(0, tt // CONV_CHUNK, conv_chunk, 0)
    hc = hbuf[...] + cb_ref[...]
    mu = jnp.mean(hc, axis=-1, keepdims=True)
    var = jnp.mean(jnp.square(hc - mu), axis=-1, keepdims=True)
    yln = (hc - mu) * lax.rsqrt(var + EPS) * lng_ref[...] + lnb_ref[...]
    yc = yln * jax.nn.sigmoid(yln)

    o = (jnp.dot(yp.astype(BF16), wout_ref[0:D_POOL, :], preferred_element_type=F32)
         + jnp.dot(yc.astype(BF16), wout_ref[D_POOL:, :], preferred_element_type=F32))
    x1 = x + (o + bout_ref[...])
    x1_ref[...] = x1

    ms2 = jnp.mean(x1 * x1, axis=-1, keepdims=True)
    hmb = (x1 * lax.rsqrt(ms2 + EPS) * g2_ref[...]).astype(BF16)
    hb32 = hmb.astype(F32)
    lo = lax.bitcast_convert_type(hb32[:, :HM_WORDS], U32)
    hi = lax.bitcast_convert_type(hb32[:, HM_WORDS:], U32)
    packed = (hi & jnp.uint32(0xFFFF0000)) | lax.shift_right_logical(lo, jnp.uint32(16))
    for p in range(HM_PIECES):
        hm_ref[pl.ds(p, tt, stride=HM_PIECES), :] = packed[:, p * LANES:(p + 1) * LANES]

    lt = lax.dot_general(wrt_ref[...], hmb, (((1,), (1,)), ((), ())),
                         preferred_element_type=F32) + br_ref[...]
    row8 = _iota_f32((SUBLANES, tt), 0)
    neg = jnp.float32(-jnp.inf)
    lg = jnp.where(row8 < N_EXPERT_GROUPS, lt[0:SUBLANES, :], neg)
    mg = jnp.max(lg, axis=0, keepdims=True)
    p_grp = 1.0 / jnp.sum(jnp.exp(lg - mg), axis=0, keepdims=True)
    gsel = jnp.min(jnp.where(lg == mg, row8, float(SUBLANES)), axis=0, keepdims=True)
    le = jnp.zeros((SUBLANES, tt), F32)
    for g in range(N_EXPERT_GROUPS):
        r0 = EXPERT_ROW0 + g * EXPERTS_PER_GROUP
        le = jnp.where(gsel == float(g), lt[r0:r0 + EXPERTS_PER_GROUP, :], le)
    m1 = jnp.max(le, axis=0, keepdims=True)
    i1 = jnp.min(jnp.where(le == m1, row8, float(SUBLANES)), axis=0, keepdims=True)
    le2 = jnp.where(row8 == i1, neg, le)
    m2 = jnp.max(le2, axis=0, keepdims=True)
    i2 = jnp.min(jnp.where(le2 == m2, row8, float(SUBLANES)), axis=0, keepdims=True)
    e2 = jnp.exp(m2 - m1)
    den = 1.0 + e2
    gate0 = p_grp * (1.0 / den)
    gate1 = p_grp * (e2 / den)
    eid0 = gsel * float(EXPERTS_PER_GROUP) + i1
    eid1 = gsel * float(EXPERTS_PER_GROUP) + i2

    rowe = _iota_f32((N_EXPERTS, tt), 0)
    oh0 = rowe == eid0
    oh1 = rowe == eid1
    ohf = jnp.where(jnp.logical_or(oh0, oh1), 1.0, 0.0)
    before = jnp.dot(ohf.astype(BF16), tri_ref[...], preferred_element_type=F32) + carry[:, 0:1]
    rank0 = jnp.sum(jnp.where(oh0, before, 0.0), axis=0, keepdims=True)
    rank1 = jnp.sum(jnp.where(oh1, before, 0.0), axis=0, keepdims=True)
    carry[...] = carry[...] + jnp.sum(ohf, axis=1, keepdims=True)
    cnt_ref[...] = carry[...]

    route = jnp.where(row8 == 0.0, eid0,
                      jnp.where(row8 == 1.0, eid1,
                                jnp.where(row8 == 2.0, rank0,
                                          jnp.where(row8 == 3.0, rank1, 0.0))))
    route_ref[...] = route.astype(I32)

    tpad = -(-tt // LANES) * LANES
    rowr = _iota_f32((ROUTER_ROWS, tt), 0)
    gt = jnp.where(rowr == 0.0, gate0, jnp.where(rowr == 1.0, gate1, 0.0))
    if tpad != tt:
        gt = jnp.concatenate([gt, jnp.zeros((ROUTER_ROWS, tpad - tt), F32)], axis=1)
    gates_ref[...] = gt.T[0:tt, :]

    @pl.when(j == n_t - 1)
    def _():
        spool_ref[...] = pext[POOL_PAD + tt - POOL_STATE:POOL_PAD + tt, :]
        sconv_ref[...] = jnp.concatenate(
            [cext[lb, CONV_PAD + tt - CONV_STATE:CONV_PAD + tt, :] for lb in range(CONV_BLOCKS)], axis=-1)


def _mixer_call(x, pool_prefix, conv_prefix, pos0, tt, weights, name):
    bsz, seq, _ = x.shape
    n_t = seq // tt
    n_blk = bsz * n_t
    n_rows = bsz * seq
    tri = (lax.broadcasted_iota(I32, (tt, tt), 0) < lax.broadcasted_iota(I32, (tt, tt), 1)).astype(BF16)

    def const(shape):
        return pl.BlockSpec(shape, lambda b, j: (0,) * len(shape))

    def rows(width):
        return pl.BlockSpec((tt, width), lambda b, j: (b * n_t + j, 0))

    in_specs = [
        pl.BlockSpec((None, tt, D_MODEL), lambda b, j: (b, j, 0)),
        pl.BlockSpec((None, POOL_PAD, D_POOL), lambda b, j: (b, 0, 0)),
        pl.BlockSpec((None, CONV_PAD, D_CONV), lambda b, j: (b, 0, 0)),
    ] + [const(w.shape) for w in weights] + [const((tt, tt))]
    out_shape = (
        jax.ShapeDtypeStruct((n_rows, D_MODEL), F32),
        jax.ShapeDtypeStruct((n_rows * HM_PIECES, LANES), U32),
        jax.ShapeDtypeStruct((n_rows, LANES), F32),
        jax.ShapeDtypeStruct((n_blk, SUBLANES, tt), I32),
        jax.ShapeDtypeStruct((N_EXPERTS, LANES), F32),
        jax.ShapeDtypeStruct((1, bsz, POOL_STATE, D_POOL), F32),
        jax.ShapeDtypeStruct((1, bsz, CONV_STATE, D_CONV), F32),
    )
    out_specs = (
        rows(D_MODEL),
        pl.BlockSpec((tt * HM_PIECES, LANES), lambda b, j: (b * n_t + j, 0)),
        rows(LANES),
        pl.BlockSpec((None, SUBLANES, tt), lambda b, j: (b * n_t + j, 0, 0)),
        pl.BlockSpec((N_EXPERTS, LANES), lambda b, j: (0, 0)),
        pl.BlockSpec((None, None, POOL_STATE, D_POOL), lambda b, j: (0, b, 0, 0)),
        pl.BlockSpec((None, None, CONV_STATE, D_CONV), lambda b, j: (0, b, 0, 0)),
    )
    return pl.pallas_call(
        functools.partial(_mixer_kernel, pos0, tt, n_t),
        out_shape=out_shape,
        grid=(bsz, n_t),
        in_specs=in_specs,
        out_specs=out_specs,
        scratch_shapes=[
            pltpu.VMEM((POOL_PAD + tt, D_POOL), F32),
            pltpu.VMEM((CONV_BLOCKS, CONV_PAD + tt, LANES), F32),
            pltpu.VMEM((tt, D_CONV), F32),
            pltpu.VMEM((N_EXPERTS, LANES), F32),
            pltpu.VMEM((POOL_PAD + tt, D_POOL), F32),
            pltpu.VMEM((POOL_PAD + tt, D_POOL), F32),
            pltpu.VMEM((POOL_PAD + tt, D_POOL), F32),
            pltpu.VMEM((SUBLANES - 1, CONV_BLOCKS, CONV_PAD + tt, LANES), F32),
        ],
        compiler_params=pltpu.CompilerParams(
            dimension_semantics=("arbitrary", "arbitrary"), vmem_limit_bytes=VMEM_LIMIT),
        name=name,
    )(x, pool_prefix, conv_prefix, *weights, tri)


def _row_copy_out(hm_ref, xb_ref, sem, t, d):
    src = hm_ref.at[pl.ds(pl.multiple_of(t * HM_PIECES, HM_PIECES), HM_PIECES), :]
    dst = xb_ref.at[pl.ds(pl.multiple_of(d * HM_PIECES, HM_PIECES), HM_PIECES), :]
    return pltpu.make_async_copy(src, dst, sem)


def _dispatch_kernel(dest_ref, hm_ref, xb_in_ref, xb_ref, sem):
    del xb_in_ref
    tc = hm_ref.shape[0] // HM_PIECES

    def start(t, c):
        for k in range(TOP_K):
            _row_copy_out(hm_ref, xb_ref, sem, t, dest_ref[0, k, t]).start(priority=k)
        return c

    for t in range(tc):
        start(t, 0)
    for k in range(TOP_K):
        pltpu.make_async_copy(hm_ref, xb_ref.at[pl.ds(0, tc * HM_PIECES), :], sem).wait()


def _dispatch_call(dest_blocks, hm, xb, name):
    n_rows = hm.shape[0] // HM_PIECES
    tc = TOKEN_TILE
    return pl.pallas_call(
        _dispatch_kernel,
        out_shape=jax.ShapeDtypeStruct(xb.shape, xb.dtype),
        grid=(n_rows // tc,),
        in_specs=[
            pl.BlockSpec((1, TOP_K, tc), lambda i: (i, 0, 0), memory_space=pltpu.SMEM),
            pl.BlockSpec((tc * HM_PIECES, LANES), lambda i: (i, 0)),
            pl.BlockSpec(memory_space=pl.ANY),
        ],
        out_specs=pl.BlockSpec(memory_space=pl.ANY),
        scratch_shapes=[pltpu.SemaphoreType.DMA(())],
        input_output_aliases={2: 0},
        compiler_params=pltpu.CompilerParams(dimension_semantics=("arbitrary",)),
        name=name,
    )(dest_blocks, hm, xb)


def _expert_kernel(te_ref, nu_ref, xb_ref, wg_ref, wu_ref, wd_ref, y_ref, wgb, wub, wdb):
    i = pl.program_id(0)
    e = te_ref[i]
    prev = te_ref[jnp.maximum(i - 1, 0)]

    @pl.when((i == 0) | (e != prev))
    def _():
        wgb[...] = wg_ref[...].astype(BF16)
        wub[...] = wu_ref[...].astype(BF16)
        wdb[...] = wd_ref[...].astype(BF16)

    @pl.when(i < nu_ref[0])
    def _():
        tm = y_ref.shape[0] // Y_PIECES
        words = [xb_ref[pl.ds(p, tm, stride=HM_PIECES), :] for p in range(HM_PIECES)]
        lo = [lax.bitcast_convert_type(lax.shift_left(w, jnp.uint32(16)), F32).astype(BF16) for w in words]
        hi = [lax.bitcast_convert_type(w & jnp.uint32(0xFFFF0000), F32).astype(BF16) for w in words]
        xt = jnp.concatenate(lo + hi, axis=-1)
        g = jnp.dot(xt, wgb[...], preferred_element_type=F32)
        up = jnp.dot(xt, wub[...], preferred_element_type=F32)
        a = (g * jax.nn.sigmoid(g) * up).astype(BF16)
        y = jnp.dot(a, wdb[...], preferred_element_type=F32)
        for p in range(Y_PIECES):
            y_ref[pl.ds(p, tm, stride=Y_PIECES), :] = y[:, p * LANES:(p + 1) * LANES]

    @pl.when(i >= nu_ref[0])
    def _():
        y_ref[...] = jnp.zeros_like(y_ref)


def _expert_call(tile_expert, n_used, xb, w_eg, w_eu, w_ed):
    n_slots = xb.shape[0] // HM_PIECES
    tm = EXPERT_TILE
    grid_spec = pltpu.PrefetchScalarGridSpec(
        num_scalar_prefetch=2,
        grid=(n_slots // tm,),
        in_specs=[
            pl.BlockSpec((tm * HM_PIECES, LANES), lambda i, te, nu: (i, 0)),
            pl.BlockSpec((None, D_MODEL, D_FF), lambda i, te, nu: (te[i], 0, 0)),
            pl.BlockSpec((None, D_MODEL, D_FF), lambda i, te, nu: (te[i], 0, 0)),
            pl.BlockSpec((None, D_FF, D_MODEL), lambda i, te, nu: (te[i], 0, 0)),
        ],
        out_specs=pl.BlockSpec((tm * Y_PIECES, LANES), lambda i, te, nu: (i, 0)),
        scratch_shapes=[
            pltpu.VMEM((D_MODEL, D_FF), BF16),
            pltpu.VMEM((D_MODEL, D_FF), BF16),
            pltpu.VMEM((D_FF, D_MODEL), BF16),
        ],
    )
    return pl.pallas_call(
        _expert_kernel,
        out_shape=jax.ShapeDtypeStruct((n_slots * Y_PIECES, LANES), F32),
        grid_spec=grid_spec,
        compiler_params=pltpu.CompilerParams(
            dimension_semantics=("arbitrary",), vmem_limit_bytes=VMEM_LIMIT),
        name="experts",
    )(tile_expert, n_used, xb, w_eg, w_eu, w_ed)


def _gather_rows(dest_ref, yb_ref, buf_ref, sem, tokens):
    for t in tokens:
        for k in range(TOP_K):
            d = dest_ref[0, k, t]
            src = yb_ref.at[pl.ds(pl.multiple_of(d * Y_PIECES, Y_PIECES), Y_PIECES), :]
            dst = buf_ref.at[k, pl.ds(t * Y_PIECES, Y_PIECES), :]
            pltpu.make_async_copy(src, dst, sem).start(priority=k)


def _wait_rows(yb_ref, buf_ref, sem):
    for k in range(TOP_K):
        pltpu.make_async_copy(yb_ref.at[pl.ds(0, buf_ref.shape[1]), :], buf_ref.at[k], sem).wait()


def _combine_kernel(dest0_ref, dest1_ref, dest2_ref, x1_ref, gates_ref, gf_ref, yb_ref, out_ref, buf, sems):
    i = pl.program_id(0)
    tc = TOKEN_TILE
    n_chunks = tc // COMBINE_CHUNK

    @pl.when(i == 0)
    def _():
        _gather_rows(dest0_ref, yb_ref, buf.at[0], sems.at[0], range(tc))

    gf = gf_ref[...]
    for half, next_dest in ((0, dest1_ref), (1, dest2_ref)):
        cur, nxt = buf.at[half], buf.at[1 - half]
        _wait_rows(yb_ref, cur, sems.at[half])
        for c in range(n_chunks):
            rows = pl.ds(half * tc + c * COMBINE_CHUNK, COMBINE_CHUNK)
            gates = gates_ref[rows, :]
            parts = [cur[0, pl.ds(c * COMBINE_CHUNK * Y_PIECES + p, COMBINE_CHUNK, stride=Y_PIECES), :] * gates[:, 0:1]
                     + cur[1, pl.ds(c * COMBINE_CHUNK * Y_PIECES + p, COMBINE_CHUNK, stride=Y_PIECES), :] * gates[:, 1:2]
                     for p in range(Y_PIECES)]
            xo = x1_ref[rows, :] + jnp.concatenate(parts, axis=-1)
            _gather_rows(next_dest, yb_ref, nxt, sems.at[1 - half],
                         range(c * COMBINE_CHUNK, (c + 1) * COMBINE_CHUNK))
            ms = jnp.mean(xo * xo, axis=-1, keepdims=True)
            out_ref[rows, :] = xo * lax.rsqrt(ms + EPS) * gf

    @pl.when(i == pl.num_programs(0) - 1)
    def _():
        _wait_rows(yb_ref, buf.at[0], sems.at[0])


def _combine_call(dest_blocks, x1, gates, gf, yb, name):
    n_rows = x1.shape[0]
    tc = TOKEN_TILE
    n_tiles = n_rows // tc
    assert n_tiles % 2 == 0

    def dest_spec(index_map):
        return pl.BlockSpec((1, TOP_K, tc), index_map, memory_space=pltpu.SMEM)

    return pl.pallas_call(
        _combine_kernel,
        out_shape=jax.ShapeDtypeStruct((n_rows, D_MODEL), F32),
        grid=(n_tiles // 2,),
        in_specs=[
            dest_spec(lambda i: (0, 0, 0)),
            dest_spec(lambda i: (2 * i + 1, 0, 0)),
            dest_spec(lambda i: (jnp.minimum(2 * i + 2, n_tiles - 1), 0, 0)),
            pl.BlockSpec((2 * tc, D_MODEL), lambda i: (i, 0)),
            pl.BlockSpec((2 * tc, LANES), lambda i: (i, 0)),
            pl.BlockSpec((1, D_MODEL), lambda i: (0, 0)),
            pl.BlockSpec(memory_space=pl.ANY),
        ],
        out_specs=pl.BlockSpec((2 * tc, D_MODEL), lambda i: (i, 0)),
        scratch_shapes=[pltpu.VMEM((2, TOP_K, tc * Y_PIECES, LANES), F32), pltpu.SemaphoreType.DMA((2,))],
        compiler_params=pltpu.CompilerParams(
            dimension_semantics=("arbitrary",), vmem_limit_bytes=VMEM_LIMIT),
        name=name,
    )(dest_blocks, dest_blocks, dest_blocks, x1, gates, gf, yb)


def _block_diag2(a, b):
    z = jnp.zeros_like(a)
    return jnp.concatenate([jnp.concatenate([a, z], axis=1), jnp.concatenate([z, b], axis=1)], axis=0)


def kernel(x_prompt, x_sample, state_pool, state_conv, norm1_g, w_in, pool_lin, pool_scale, conv_dw, conv_dw_b, conv_ln_g, conv_ln_b, w_out, b_out, norm2_g, w_rg, b_rg, w_re, b_re, w_eg, w_eu, w_ed, norm_f_g):
    assert norm1_g.shape[0] == 1, "single-layer trunk"
    bsz, seq, _ = x_prompt.shape
    dbsz, dseq, _ = x_sample.shape
    past_len = 1024
    n_prompt = bsz * seq
    n_sample = dbsz * dseq
    n_tokens = n_prompt + n_sample
    assert seq % PROMPT_TILE == 0 and n_prompt % TOKEN_TILE == 0 and n_sample % TOKEN_TILE == 0

    row = lambda a: a.reshape(1, -1)
    router_w = jnp.zeros((ROUTER_ROWS, D_MODEL), F32)
    router_w = router_w.at[0:N_EXPERT_GROUPS].set(w_rg[0].T)
    router_w = router_w.at[EXPERT_ROW0:EXPERT_ROW0 + N_EXPERTS].set(
        jnp.transpose(w_re[0], (0, 2, 1)).reshape(N_EXPERTS, D_MODEL))
    router_b = jnp.zeros((ROUTER_ROWS,), F32)
    router_b = router_b.at[0:N_EXPERT_GROUPS].set(b_rg[0])
    router_b = router_b.at[EXPERT_ROW0:EXPERT_ROW0 + N_EXPERTS].set(b_re[0].reshape(-1))
    conv_w = jnp.concatenate([conv_dw[0], jnp.zeros((1, D_CONV), F32)], axis=0)
    weights = [
        row(norm1_g[0]), w_in[0].astype(BF16),
        _block_diag2(pool_lin[0, 0], pool_lin[0, 1]).astype(BF16),
        _block_diag2(pool_lin[0, 2], pool_lin[0, 3]).astype(BF16),
        row(pool_scale[0]), conv_w, row(conv_dw_b[0]), row(conv_ln_g[0]), row(conv_ln_b[0]),
        w_out[0].astype(BF16), row(b_out[0]), row(norm2_g[0]),
        router_w.astype(BF16), router_b.reshape(ROUTER_ROWS, 1),
    ]

    zpool = jnp.zeros((bsz, POOL_PAD, D_POOL), F32)
    zconv = jnp.zeros((bsz, CONV_PAD, D_CONV), F32)
    ppool = jnp.pad(state_pool[0], ((0, 0), (POOL_PAD - POOL_STATE, 0), (0, 0)))
    pconv = jnp.pad(state_conv[0], ((0, 0), (CONV_PAD - CONV_STATE, 0), (0, 0)))

    x1_p, hm_p, gates_p, route_p, cnt_p, sp_p, sc_p = _mixer_call(
        x_prompt, zpool, zconv, 0, PROMPT_TILE, weights, "mixer_prompt")
    x1_s, hm_s, gates_s, route_s, cnt_s, sp_s, sc_s = _mixer_call(
        x_sample, ppool, pconv, past_len, dseq, weights, "mixer_sample")

    tm = EXPERT_TILE
    cnt_p = cnt_p[:, 0].astype(I32)
    cnt_s = cnt_s[:, 0].astype(I32)
    padded = (cnt_p + cnt_s + tm - 1) // tm * tm
    pad_end = jnp.cumsum(padded)
    pad_start = pad_end - padded

    def dest_blocks(route, base):
        eid = jnp.stack([route[:, k, :].reshape(-1) for k in range(TOP_K)])
        rank = jnp.stack([route[:, TOP_K + k, :].reshape(-1) for k in range(TOP_K)])
        experts = jnp.arange(N_EXPERTS, dtype=I32)
        dest = rank + jnp.sum(jnp.where(eid[..., None] == experts, base, 0), axis=-1)
        return dest.reshape(TOP_K, -1, TOKEN_TILE).transpose(1, 0, 2)

    dest_p = dest_blocks(route_p, pad_start)
    dest_s = dest_blocks(route_s, pad_start + cnt_p)
    n_tiles = -(-(n_tokens * TOP_K + N_EXPERTS * (tm - 1)) // tm)
    n_used = (pad_end[-1] // tm).astype(I32)
    tile_ids = jnp.minimum(jnp.arange(n_tiles, dtype=I32), n_used - 1)
    tile_expert = jnp.minimum(
        jnp.sum(pad_end[None, :] <= (tile_ids * tm)[:, None], axis=1), N_EXPERTS - 1).astype(I32)

    xb = jnp.zeros((n_tiles * tm * HM_PIECES, LANES), U32)
    xb = _dispatch_call(dest_p, hm_p, xb, "dispatch_prompt")
    xb = _dispatch_call(dest_s, hm_s, xb, "dispatch_sample")
    yb = _expert_call(tile_expert, n_used.reshape(1), xb, w_eg[0], w_eu[0], w_ed[0])
    gf = row(norm_f_g)
    y_p = _combine_call(dest_p, x1_p, gates_p, gf, yb, "combine_prompt")
    y_s = _combine_call(dest_s, x1_s, gates_s, gf, yb, "combine_sample")

    return (y_p.reshape(bsz, seq, D_MODEL), y_s.reshape(dbsz, dseq, D_MODEL),
            sp_p, sc_p, sp_s, sc_s)
```

```python
import functools

import jax
import jax.numpy as jnp
from jax import lax
from jax.experimental import pallas as pl
from jax.experimental.pallas import tpu as pltpu

F32 = jnp.float32
BF16 = jnp.bfloat16
U32 = jnp.uint32
I32 = jnp.int32

D_MODEL = 1024
D_POOL = 512
D_CONV = 512
D_IN = D_POOL + 2 * D_CONV
POOL_WINDOWS = (2, 4, 8, 16)
POOL_GROUP_DIM = D_POOL // len(POOL_WINDOWS)
POOL_STATE = max(POOL_WINDOWS) - 1
CONV_WIDTH = 31
CONV_STATE = CONV_WIDTH - 1
N_EXPERT_GROUPS = 4
EXPERTS_PER_GROUP = 8
N_EXPERTS = N_EXPERT_GROUPS * EXPERTS_PER_GROUP
TOP_K = 2
D_FF = D_MODEL // 2
EPS = 1e-6

SUBLANES = 8
LANES = 128
POOL_PAD = 32
CONV_PAD = 32
ROUTER_ROWS = LANES
EXPERT_ROW0 = SUBLANES
CONV_CHUNK = 64
CONV_BLOCKS = D_CONV // LANES
HM_WORDS = D_MODEL // 2
HM_PIECES = HM_WORDS // LANES

PROMPT_TILE = 512
TOKEN_TILE = 512
EXPERT_TILE = 256
COMBINE_CHUNK = 64
VMEM_LIMIT = 48 * 1024 * 1024


def _iota_f32(shape, axis):
    return lax.broadcasted_iota(I32, shape, axis).astype(F32)


def _mixer_kernel(pos0, tt, n_t,
                  x_ref, pp_ref, cp_ref, g1_ref, win_ref, bd01_ref, bd23_ref, pscale_ref,
                  cw_ref, cb_ref, lng_ref, lnb_ref, wout_ref, bout_ref, g2_ref, wrt_ref, br_ref,
                  tri_ref,
                  x1_ref, hm_ref, gates_ref, route_ref, cnt_ref, spool_ref, sconv_ref,
                  pext, cext, hbuf, carry, lv2, lv4, lv8, csh):
    b = pl.program_id(0)
    j = pl.program_id(1)

    @pl.when((b == 0) & (j == 0))
    def _():
        carry[...] = jnp.zeros_like(carry)

    @pl.when(j == 0)
    def _():
        pext[0:POOL_PAD, :] = pp_ref[...]
        for lb in range(CONV_BLOCKS):
            cext[lb, 0:CONV_PAD, :] = cp_ref[:, lb * LANES:(lb + 1) * LANES]

    @pl.when(j > 0)
    def _():
        pext[0:POOL_PAD, :] = pext[tt:tt + POOL_PAD, :]
        cext[:, 0:CONV_PAD, :] = cext[:, tt:tt + CONV_PAD, :]

    x = x_ref[...]
    ms = jnp.mean(x * x, axis=-1, keepdims=True)
    h = (x * lax.rsqrt(ms + EPS) * g1_ref[...]).astype(BF16)
    z = jnp.dot(h, win_ref[...], preferred_element_type=F32)
    u = z[:, :D_POOL]
    pext[POOL_PAD:POOL_PAD + tt, :] = u
    v = z[:, D_POOL:D_POOL + D_CONV] * jax.nn.sigmoid(z[:, D_POOL + D_CONV:])
    for lb in range(CONV_BLOCKS):
        cext[lb, CONV_PAD:CONV_PAD + tt, :] = v[:, lb * LANES:(lb + 1) * LANES]

    pos = pos0 + j * tt + lax.broadcasted_iota(I32, (tt, 1), 0)
    end = POOL_PAD + tt
    gd = POOL_GROUP_DIM
    lv2[8:end, :] = pext[8:end, :] + pext[7:end - 1, :]
    lv4[16:end, gd:] = lv2[16:end, gd:] + lv2[14:end - 2, gd:]
    lv8[24:end, 2 * gd:] = lv4[24:end, 2 * gd:] + lv4[20:end - 4, 2 * gd:]
    sums = [lv2[POOL_PAD:end, 0:gd], lv4[POOL_PAD:end, gd:2 * gd], lv8[POOL_PAD:end, 2 * gd:3 * gd],
            lv8[POOL_PAD:end, 3 * gd:] + lv8[POOL_PAD - 8:end - 8, 3 * gd:]]
    parts = []
    for g, w in enumerate(POOL_WINDOWS):
        sl = slice(g * gd, (g + 1) * gd)
        cnt = jnp.minimum(pos + 1, w).astype(F32)
        parts.append(sums[g] / cnt - u[:, sl])
    d = jnp.concatenate(parts, axis=-1).astype(BF16)
    half = D_POOL // 2
    yp = jnp.concatenate(
        [jnp.dot(d[:, :half], bd01_ref[...], preferred_element_type=F32),
         jnp.dot(d[:, half:], bd23_ref[...], preferred_element_type=F32)], axis=-1)
    yp = yp * pscale_ref[...]

    n_sh = CONV_PAD + tt - SUBLANES
    for s in range(1, SUBLANES):
        csh[s - 1, :, 0:n_sh, :] = cext[:, s:s + n_sh, :]

    for lb in range(CONV_BLOCKS):
        lanes = slice(lb * LANES, (lb + 1) * LANES)
        wts = [jnp.broadcast_to(cw_ref[k:k + 1, lanes], (CONV_CHUNK, LANES)) for k in range(CONV_WIDTH)]

        def conv_chunk(c, carry_, lb=lb, lanes=lanes, wts=wts):
            r0 = pl.multiple_of(c * CONV_CHUNK, CONV_CHUNK)
            acc = None
            for k in range(CONV_WIDTH):
                q, s = divmod(CONV_PAD - CONV_STATE + k, SUBLANES)
                rows = pl.ds(pl.multiple_of(r0 + q * SUBLANES, SUBLANES), CONV_CHUNK)
                tap = (cext[lb, rows, :] if s == 0 else csh[s - 1, lb, rows, :]) * wts[k]
                acc = tap if acc is None else acc + tap
            hbuf[pl.ds(r0, CONV_CHUNK), lanes] = acc
            return carry_

        lax.fori_loop(0, tt // CONV_CHUNK, conv_chunk, 0)
    hc = hbuf[...] + cb_ref[...]
    mu = jnp.mean(hc, axis=-1, keepdims=True)
    var = jnp.mean(jnp.square(hc - mu), axis=-1, keepdims=True)
    yln = (hc - mu) * lax.rsqrt(var + EPS) * lng_ref[...] + lnb_ref[...]
    yc = yln * jax.nn.sigmoid(yln)

    o = (jnp.dot(yp.astype(BF16), wout_ref[0:D_POOL, :], preferred_element_type=F32)
         + jnp.dot(yc.astype(BF16), wout_ref[D_POOL:, :], preferred_element_type=F32))
    x1 = x + (o + bout_ref[...])
    x1_ref[...] = x1

    ms2 = jnp.mean(x1 * x1, axis=-1, keepdims=True)
    hmb = (x1 * lax.rsqrt(ms2 + EPS) * g2_ref[...]).astype(BF16)
    packed = _pack_bf16_pairs(hmb.astype(F32))
    for p in range(HM_PIECES):
        hm_ref[pl.ds(p, tt, stride=HM_PIECES), :] = packed[:, p * LANES:(p + 1) * LANES]

    lt = lax.dot_general(wrt_ref[...], hmb, (((1,), (1,)), ((), ())),
                         preferred_element_type=F32) + br_ref[...]
    row8 = _iota_f32((SUBLANES, tt), 0)
    neg = jnp.float32(-jnp.inf)
    lg = jnp.where(row8 < N_EXPERT_GROUPS, lt[0:SUBLANES, :], neg)
    mg = jnp.max(lg, axis=0, keepdims=True)
    p_grp = 1.0 / jnp.sum(jnp.exp(lg - mg), axis=0, keepdims=True)
    gsel = jnp.min(jnp.where(lg == mg, row8, float(SUBLANES)), axis=0, keepdims=True)
    le = jnp.zeros((SUBLANES, tt), F32)
    for g in range(N_EXPERT_GROUPS):
        r0 = EXPERT_ROW0 + g * EXPERTS_PER_GROUP
        le = jnp.where(gsel == float(g), lt[r0:r0 + EXPERTS_PER_GROUP, :], le)
    m1 = jnp.max(le, axis=0, keepdims=True)
    i1 = jnp.min(jnp.where(le == m1, row8, float(SUBLANES)), axis=0, keepdims=True)
    le2 = jnp.where(row8 == i1, neg, le)
    m2 = jnp.max(le2, axis=0, keepdims=True)
    i2 = jnp.min(jnp.where(le2 == m2, row8, float(SUBLANES)), axis=0, keepdims=True)
    e2 = jnp.exp(m2 - m1)
    den = 1.0 + e2
    gate0 = p_grp * (1.0 / den)
    gate1 = p_grp * (e2 / den)
    eid0 = gsel * float(EXPERTS_PER_GROUP) + i1
    eid1 = gsel * float(EXPERTS_PER_GROUP) + i2

    rowe = _iota_f32((N_EXPERTS, tt), 0)
    oh0 = rowe == eid0
    oh1 = rowe == eid1
    ohf = jnp.where(jnp.logical_or(oh0, oh1), 1.0, 0.0)
    before = jnp.dot(ohf.astype(BF16), tri_ref[...], preferred_element_type=F32) + carry[:, 0:1]
    rank0 = jnp.sum(jnp.where(oh0, before, 0.0), axis=0, keepdims=True)
    rank1 = jnp.sum(jnp.where(oh1, before, 0.0), axis=0, keepdims=True)
    carry[...] = carry[...] + jnp.sum(ohf, axis=1, keepdims=True)
    cnt_ref[...] = carry[...]

    route = jnp.where(row8 == 0.0, eid0,
                      jnp.where(row8 == 1.0, eid1,
                                jnp.where(row8 == 2.0, rank0,
                                          jnp.where(row8 == 3.0, rank1, 0.0))))
    route_ref[...] = route.astype(I32)

    tpad = -(-tt // LANES) * LANES
    rowr = _iota_f32((ROUTER_ROWS, tt), 0)
    gt = jnp.where(rowr == 0.0, gate0, jnp.where(rowr == 1.0, gate1, 0.0))
    if tpad != tt:
        gt = jnp.concatenate([gt, jnp.zeros((ROUTER_ROWS, tpad - tt), F32)], axis=1)
    gates_ref[...] = gt.T[0:tt, :]

    @pl.when(j == n_t - 1)
    def _():
        spool_ref[...] = pext[POOL_PAD + tt - POOL_STATE:POOL_PAD + tt, :]
        sconv_ref[...] = jnp.concatenate(
            [cext[lb, CONV_PAD + tt - CONV_STATE:CONV_PAD + tt, :] for lb in range(CONV_BLOCKS)], axis=-1)


def _mixer_call(x, pool_prefix, conv_prefix, pos0, tt, weights, name):
    bsz, seq, _ = x.shape
    n_t = seq // tt
    n_blk = bsz * n_t
    n_rows = bsz * seq
    tri = (lax.broadcasted_iota(I32, (tt, tt), 0) < lax.broadcasted_iota(I32, (tt, tt), 1)).astype(BF16)

    def const(shape):
        return pl.BlockSpec(shape, lambda b, j: (0,) * len(shape))

    def rows(width):
        return pl.BlockSpec((tt, width), lambda b, j: (b * n_t + j, 0))

    in_specs = [
        pl.BlockSpec((None, tt, D_MODEL), lambda b, j: (b, j, 0)),
        pl.BlockSpec((None, POOL_PAD, D_POOL), lambda b, j: (b, 0, 0)),
        pl.BlockSpec((None, CONV_PAD, D_CONV), lambda b, j: (b, 0, 0)),
    ] + [const(w.shape) for w in weights] + [const((tt, tt))]
    out_shape = (
        jax.ShapeDtypeStruct((n_rows, D_MODEL), F32),
        jax.ShapeDtypeStruct((n_rows * HM_PIECES, LANES), U32),
        jax.ShapeDtypeStruct((n_rows, LANES), F32),
        jax.ShapeDtypeStruct((n_blk, SUBLANES, tt), I32),
        jax.ShapeDtypeStruct((N_EXPERTS, LANES), F32),
        jax.ShapeDtypeStruct((1, bsz, POOL_STATE, D_POOL), F32),
        jax.ShapeDtypeStruct((1, bsz, CONV_STATE, D_CONV), F32),
    )
    out_specs = (
        rows(D_MODEL),
        pl.BlockSpec((tt * HM_PIECES, LANES), lambda b, j: (b * n_t + j, 0)),
        rows(LANES),
        pl.BlockSpec((None, SUBLANES, tt), lambda b, j: (b * n_t + j, 0, 0)),
        pl.BlockSpec((N_EXPERTS, LANES), lambda b, j: (0, 0)),
        pl.BlockSpec((None, None, POOL_STATE, D_POOL), lambda b, j: (0, b, 0, 0)),
        pl.BlockSpec((None, None, CONV_STATE, D_CONV), lambda b, j: (0, b, 0, 0)),
    )
    return pl.pallas_call(
        functools.partial(_mixer_kernel, pos0, tt, n_t),
        out_shape=out_shape,
        grid=(bsz, n_t),
        in_specs=in_specs,
        out_specs=out_specs,
        scratch_shapes=[
            pltpu.VMEM((POOL_PAD + tt, D_POOL), F32),
            pltpu.VMEM((CONV_BLOCKS, CONV_PAD + tt, LANES), F32),
            pltpu.VMEM((tt, D_CONV), F32),
            pltpu.VMEM((N_EXPERTS, LANES), F32),
            pltpu.VMEM((POOL_PAD + tt, D_POOL), F32),
            pltpu.VMEM((POOL_PAD + tt, D_POOL), F32),
            pltpu.VMEM((POOL_PAD + tt, D_POOL), F32),
            pltpu.VMEM((SUBLANES - 1, CONV_BLOCKS, CONV_PAD + tt, LANES), F32),
        ],
        compiler_params=pltpu.CompilerParams(
            dimension_semantics=("arbitrary", "arbitrary"), vmem_limit_bytes=VMEM_LIMIT),
        name=name,
    )(x, pool_prefix, conv_prefix, *weights, tri)


def _row_copy_out(hm_ref, xb_ref, sem, t, d):
    src = hm_ref.at[pl.ds(pl.multiple_of(t * HM_PIECES, HM_PIECES), HM_PIECES), :]
    dst = xb_ref.at[pl.ds(pl.multiple_of(d * HM_PIECES, HM_PIECES), HM_PIECES), :]
    return pltpu.make_async_copy(src, dst, sem)


def _dispatch_kernel(dest_ref, hm_ref, xb_in_ref, xb_ref, sem):
    del xb_in_ref
    tc = hm_ref.shape[0] // HM_PIECES

    def start(t, c):
        for k in range(TOP_K):
            _row_copy_out(hm_ref, xb_ref, sem, t, dest_ref[0, k, t]).start(priority=k)
        return c

    for t in range(tc):
        start(t, 0)
    for k in range(TOP_K):
        pltpu.make_async_copy(hm_ref, xb_ref.at[pl.ds(0, tc * HM_PIECES), :], sem).wait()


def _dispatch_call(dest_blocks, hm, xb, name):
    n_rows = hm.shape[0] // HM_PIECES
    tc = TOKEN_TILE
    return pl.pallas_call(
        _dispatch_kernel,
        out_shape=jax.ShapeDtypeStruct(xb.shape, xb.dtype),
        grid=(n_rows // tc,),
        in_specs=[
            pl.BlockSpec((1, TOP_K, tc), lambda i: (i, 0, 0), memory_space=pltpu.SMEM),
            pl.BlockSpec((tc * HM_PIECES, LANES), lambda i: (i, 0)),
            pl.BlockSpec(memory_space=pl.ANY),
        ],
        out_specs=pl.BlockSpec(memory_space=pl.ANY),
        scratch_shapes=[pltpu.SemaphoreType.DMA(())],
        input_output_aliases={2: 0},
        compiler_params=pltpu.CompilerParams(dimension_semantics=("arbitrary",)),
        name=name,
    )(dest_blocks, hm, xb)


def _pack_bf16_pairs(x):
    w = x.shape[1] // 2
    rounded = x.astype(BF16).astype(F32)
    lo = lax.bitcast_convert_type(rounded[:, :w], U32)
    hi = lax.bitcast_convert_type(rounded[:, w:], U32)
    return (hi & jnp.uint32(0xFFFF0000)) | lax.shift_right_logical(lo, jnp.uint32(16))


def _unpack_bf16_pairs(words):
    lo = lax.bitcast_convert_type(lax.shift_left(words, jnp.uint32(16)), F32)
    hi = lax.bitcast_convert_type(words & jnp.uint32(0xFFFF0000), F32)
    return lo, hi


def _expert_kernel(t0_ref, nt_ref, used_ref, wg_ref, wu_ref, wd_ref, xb_ref, yb_ref,
                   wgb, wub, wdb, xbuf, ybuf, sem_in, sem_out):
    e = pl.program_id(0)
    tile_rows = xbuf.shape[1]
    tm = tile_rows // HM_PIECES
    t0 = t0_ref[e]
    n = nt_ref[e]
    wgb[...] = wg_ref[...].astype(BF16)
    wub[...] = wu_ref[...].astype(BF16)
    wdb[...] = wd_ref[...].astype(BF16)

    def tile(ref, j):
        return ref.at[pl.ds(pl.multiple_of((t0 + j) * tile_rows, tile_rows), tile_rows), :]

    def fetch(j, slot):
        return pltpu.make_async_copy(tile(xb_ref, j), xbuf.at[slot], sem_in.at[slot])

    def put(j, slot):
        return pltpu.make_async_copy(ybuf.at[slot], tile(yb_ref, j), sem_out.at[slot])

    @pl.when(n > 0)
    def _():
        fetch(0, 0).start()

    def body(j, carry):
        slot = j % 2
        fetch(j, slot).wait()

        @pl.when(j + 1 < n)
        def _():
            fetch(j + 1, 1 - slot).start()

        @pl.when(j >= 2)
        def _():
            put(j - 2, slot).wait()

        pieces = [_unpack_bf16_pairs(xbuf[slot, pl.ds(p, tm, stride=HM_PIECES), :]) for p in range(HM_PIECES)]
        xt = jnp.concatenate([lo.astype(BF16) for lo, _ in pieces] + [hi.astype(BF16) for _, hi in pieces],
                             axis=-1)
        g = jnp.dot(xt, wgb[...], preferred_element_type=F32)
        up = jnp.dot(xt, wub[...], preferred_element_type=F32)
        a = (g * jax.nn.sigmoid(g) * up).astype(BF16)
        packed = _pack_bf16_pairs(jnp.dot(a, wdb[...], preferred_element_type=F32))
        for p in range(HM_PIECES):
            ybuf[slot, pl.ds(p, tm, stride=HM_PIECES), :] = packed[:, p * LANES:(p + 1) * LANES]
        put(j, slot).start()
        return carry

    lax.fori_loop(0, n, body, 0)

    @pl.when(n >= 2)
    def _():
        put(n - 2, n % 2).wait()

    @pl.when(n >= 1)
    def _():
        put(n - 1, (n - 1) % 2).wait()

    @pl.when(e == pl.num_programs(0) - 1)
    def _():
        ybuf[0] = jnp.zeros(ybuf.shape[1:], ybuf.dtype)
        n_all = yb_ref.shape[0] // tile_rows

        def zero_tile(j, carry):
            dst = yb_ref.at[pl.ds(pl.multiple_of(j * tile_rows, tile_rows), tile_rows), :]
            cp = pltpu.make_async_copy(ybuf.at[0], dst, sem_out.at[0])
            cp.start()
            cp.wait()
            return carry

        lax.fori_loop(used_ref[0], n_all, zero_tile, 0)


def _expert_call(first_tile, n_tiles, n_used, xb, w_eg, w_eu, w_ed):
    tm = EXPERT_TILE
    tile_rows = tm * HM_PIECES
    grid_spec = pltpu.PrefetchScalarGridSpec(
        num_scalar_prefetch=3,
        grid=(N_EXPERTS,),
        in_specs=[
            pl.BlockSpec((None, D_MODEL, D_FF), lambda e, *_: (e, 0, 0)),
            pl.BlockSpec((None, D_MODEL, D_FF), lambda e, *_: (e, 0, 0)),
            pl.BlockSpec((None, D_FF, D_MODEL), lambda e, *_: (e, 0, 0)),
            pl.BlockSpec(memory_space=pl.ANY),
        ],
        out_specs=pl.BlockSpec(memory_space=pl.ANY),
        scratch_shapes=[
            pltpu.VMEM((D_MODEL, D_FF), BF16),
            pltpu.VMEM((D_MODEL, D_FF), BF16),
            pltpu.VMEM((D_FF, D_MODEL), BF16),
            pltpu.VMEM((2, tile_rows, LANES), U32),
            pltpu.VMEM((2, tile_rows, LANES), U32),
            pltpu.SemaphoreType.DMA((2,)),
            pltpu.SemaphoreType.DMA((2,)),
        ],
    )
    return pl.pallas_call(
        _expert_kernel,
        out_shape=jax.ShapeDtypeStruct(xb.shape, U32),
        grid_spec=grid_spec,
        compiler_params=pltpu.CompilerParams(
            dimension_semantics=("arbitrary",), vmem_limit_bytes=VMEM_LIMIT),
        name="experts",
    )(first_tile, n_tiles, n_used, w_eg, w_eu, w_ed, xb)


def _gather_rows(dest_ref, yb_ref, buf_ref, sem, tokens):
    for t in tokens:
        for k in range(TOP_K):
            d = dest_ref[0, k, t]
            src = yb_ref.at[pl.ds(pl.multiple_of(d * HM_PIECES, HM_PIECES), HM_PIECES), :]
            dst = buf_ref.at[k, pl.ds(t * HM_PIECES, HM_PIECES), :]
            pltpu.make_async_copy(src, dst, sem).start(priority=k)


def _wait_rows(yb_ref, buf_ref, sem):
    for k in range(TOP_K):
        pltpu.make_async_copy(yb_ref.at[pl.ds(0, buf_ref.shape[1]), :], buf_ref.at[k], sem).wait()


def _combine_kernel(dest0_ref, dest1_ref, dest2_ref, x1_ref, gates_ref, gf_ref, yb_ref, out_ref, buf, sems):
    i = pl.program_id(0)
    tc = TOKEN_TILE
    n_chunks = tc // COMBINE_CHUNK

    @pl.when(i == 0)
    def _():
        _gather_rows(dest0_ref, yb_ref, buf.at[0], sems.at[0], range(tc))

    gf = gf_ref[...]
    for half, next_dest in ((0, dest1_ref), (1, dest2_ref)):
        cur, nxt = buf.at[half], buf.at[1 - half]
        _wait_rows(yb_ref, cur, sems.at[half])
        for c in range(n_chunks):
            rows = pl.ds(half * tc + c * COMBINE_CHUNK, COMBINE_CHUNK)
            gates = gates_ref[rows, :]
            lows, highs = [], []
            for p in range(HM_PIECES):
                piece = pl.ds(c * COMBINE_CHUNK * HM_PIECES + p, COMBINE_CHUNK, stride=HM_PIECES)
                lo0, hi0 = _unpack_bf16_pairs(cur[0, piece, :])
                lo1, hi1 = _unpack_bf16_pairs(cur[1, piece, :])
                lows.append(lo0 * gates[:, 0:1] + lo1 * gates[:, 1:2])
                highs.append(hi0 * gates[:, 0:1] + hi1 * gates[:, 1:2])
            xo = x1_ref[rows, :] + jnp.concatenate(lows + highs, axis=-1)
            _gather_rows(next_dest, yb_ref, nxt, sems.at[1 - half],
                         range(c * COMBINE_CHUNK, (c + 1) * COMBINE_CHUNK))
            ms = jnp.mean(xo * xo, axis=-1, keepdims=True)
            out_ref[rows, :] = xo * lax.rsqrt(ms + EPS) * gf

    @pl.when(i == pl.num_programs(0) - 1)
    def _():
        _wait_rows(yb_ref, buf.at[0], sems.at[0])


def _combine_call(dest_blocks, x1, gates, gf, yb, name):
    n_rows = x1.shape[0]
    tc = TOKEN_TILE
    n_tiles = n_rows // tc
    assert n_tiles % 2 == 0

    def dest_spec(index_map):
        return pl.BlockSpec((1, TOP_K, tc), index_map, memory_space=pltpu.SMEM)

    return pl.pallas_call(
        _combine_kernel,
        out_shape=jax.ShapeDtypeStruct((n_rows, D_MODEL), F32),
        grid=(n_tiles // 2,),
        in_specs=[
            dest_spec(lambda i: (0, 0, 0)),
            dest_spec(lambda i: (2 * i + 1, 0, 0)),
            dest_spec(lambda i: (jnp.minimum(2 * i + 2, n_tiles - 1), 0, 0)),
            pl.BlockSpec((2 * tc, D_MODEL), lambda i: (i, 0)),
            pl.BlockSpec((2 * tc, LANES), lambda i: (i, 0)),
            pl.BlockSpec((1, D_MODEL), lambda i: (0, 0)),
            pl.BlockSpec(memory_space=pl.ANY),
        ],
        out_specs=pl.BlockSpec((2 * tc, D_MODEL), lambda i: (i, 0)),
        scratch_shapes=[pltpu.VMEM((2, TOP_K, tc * HM_PIECES, LANES), U32), pltpu.SemaphoreType.DMA((2,))],
        compiler_params=pltpu.CompilerParams(
            dimension_semantics=("arbitrary",), vmem_limit_bytes=VMEM_LIMIT),
        name=name,
    )(dest_blocks, dest_blocks, dest_blocks, x1, gates, gf, yb)


def _block_diag2(a, b):
    z = jnp.zeros_like(a)
    return jnp.concatenate([jnp.concatenate([a, z], axis=1), jnp.concatenate([z, b], axis=1)], axis=0)


def kernel(x_prompt, x_sample, state_pool, state_conv, norm1_g, w_in, pool_lin, pool_scale, conv_dw, conv_dw_b, conv_ln_g, conv_ln_b, w_out, b_out, norm2_g, w_rg, b_rg, w_re, b_re, w_eg, w_eu, w_ed, norm_f_g):
    assert norm1_g.shape[0] == 1, "single-layer trunk"
    bsz, seq, _ = x_prompt.shape
    dbsz, dseq, _ = x_sample.shape
    past_len = 1024
    n_prompt = bsz * seq
    n_sample = dbsz * dseq
    n_tokens = n_prompt + n_sample
    assert seq % PROMPT_TILE == 0 and n_prompt % TOKEN_TILE == 0 and n_sample % TOKEN_TILE == 0

    row = lambda a: a.reshape(1, -1)
    router_w = jnp.zeros((ROUTER_ROWS, D_MODEL), F32)
    router_w = router_w.at[0:N_EXPERT_GROUPS].set(w_rg[0].T)
    router_w = router_w.at[EXPERT_ROW0:EXPERT_ROW0 + N_EXPERTS].set(
        jnp.transpose(w_re[0], (0, 2, 1)).reshape(N_EXPERTS, D_MODEL))
    router_b = jnp.zeros((ROUTER_ROWS,), F32)
    router_b = router_b.at[0:N_EXPERT_GROUPS].set(b_rg[0])
    router_b = router_b.at[EXPERT_ROW0:EXPERT_ROW0 + N_EXPERTS].set(b_re[0].reshape(-1))
    conv_w = jnp.concatenate([conv_dw[0], jnp.zeros((1, D_CONV), F32)], axis=0)
    weights = [
        row(norm1_g[0]), w_in[0].astype(BF16),
        _block_diag2(pool_lin[0, 0], pool_lin[0, 1]).astype(BF16),
        _block_diag2(pool_lin[0, 2], pool_lin[0, 3]).astype(BF16),
        row(pool_scale[0]), conv_w, row(conv_dw_b[0]), row(conv_ln_g[0]), row(conv_ln_b[0]),
        w_out[0].astype(BF16), row(b_out[0]), row(norm2_g[0]),
        router_w.astype(BF16), router_b.reshape(ROUTER_ROWS, 1),
    ]

    zpool = jnp.zeros((bsz, POOL_PAD, D_POOL), F32)
    zconv = jnp.zeros((bsz, CONV_PAD, D_CONV), F32)
    ppool = jnp.pad(state_pool[0], ((0, 0), (POOL_PAD - POOL_STATE, 0), (0, 0)))
    pconv = jnp.pad(state_conv[0], ((0, 0), (CONV_PAD - CONV_STATE, 0), (0, 0)))

    x1_p, hm_p, gates_p, route_p, cnt_p, sp_p, sc_p = _mixer_call(
        x_prompt, zpool, zconv, 0, PROMPT_TILE, weights, "mixer_prompt")
    x1_s, hm_s, gates_s, route_s, cnt_s, sp_s, sc_s = _mixer_call(
        x_sample, ppool, pconv, past_len, dseq, weights, "mixer_sample")

    tm = EXPERT_TILE
    cnt_p = cnt_p[:, 0].astype(I32)
    cnt_s = cnt_s[:, 0].astype(I32)
    padded = (cnt_p + cnt_s + tm - 1) // tm * tm
    pad_end = jnp.cumsum(padded)
    pad_start = pad_end - padded

    def dest_blocks(route, base):
        eid = jnp.stack([route[:, k, :].reshape(-1) for k in range(TOP_K)])
        rank = jnp.stack([route[:, TOP_K + k, :].reshape(-1) for k in range(TOP_K)])
        experts = jnp.arange(N_EXPERTS, dtype=I32)
        dest = rank + jnp.sum(jnp.where(eid[..., None] == experts, base, 0), axis=-1)
        return dest.reshape(TOP_K, -1, TOKEN_TILE).transpose(1, 0, 2)

    dest_p = dest_blocks(route_p, pad_start)
    dest_s = dest_blocks(route_s, pad_start + cnt_p)
    n_tiles = -(-(n_tokens * TOP_K + N_EXPERTS * (tm - 1)) // tm)
    n_used = (pad_end[-1:] // tm).astype(I32)

    xb = jnp.zeros((n_tiles * tm * HM_PIECES, LANES), U32)
    xb = _dispatch_call(dest_p, hm_p, xb, "dispatch_prompt")
    xb = _dispatch_call(dest_s, hm_s, xb, "dispatch_sample")
    yb = _expert_call((pad_start // tm).astype(I32), (padded // tm).astype(I32), n_used,
                      xb, w_eg[0], w_eu[0], w_ed[0])
    gf = row(norm_f_g)
    y_p = _combine_call(dest_p, x1_p, gates_p, gf, yb, "combine_prompt")
    y_s = _combine_call(dest_s, x1_s, gates_s, gf, yb, "combine_sample")

    return (y_p.reshape(bsz, seq, D_MODEL), y_s.reshape(dbsz, dseq, D_MODEL),
            sp_p, sc_p, sp_s, sc_s)
```

```python
import functools

import jax
import jax.numpy as jnp
from jax import lax
from jax.experimental import pallas as pl
from jax.experimental.pallas import tpu as pltpu

F32 = jnp.float32
BF16 = jnp.bfloat16
U32 = jnp.uint32
I32 = jnp.int32

D_MODEL = 1024
D_POOL = 512
D_CONV = 512
D_IN = D_POOL + 2 * D_CONV
POOL_WINDOWS = (2, 4, 8, 16)
POOL_GROUP_DIM = D_POOL // len(POOL_WINDOWS)
POOL_STATE = max(POOL_WINDOWS) - 1
CONV_WIDTH = 31
CONV_STATE = CONV_WIDTH - 1
N_EXPERT_GROUPS = 4
EXPERTS_PER_GROUP = 8
N_EXPERTS = N_EXPERT_GROUPS * EXPERTS_PER_GROUP
TOP_K = 2
D_FF = D_MODEL // 2
EPS = 1e-6

SUBLANES = 8
LANES = 128
POOL_PAD = 32
CONV_PAD = 32
ROUTER_ROWS = LANES
EXPERT_ROW0 = SUBLANES
CONV_CHUNK = 64
CONV_BLOCKS = D_CONV // LANES
HM_WORDS = D_MODEL // 2
HM_PIECES = HM_WORDS // LANES

PROMPT_TILE = 512
TOKEN_TILE = 512
EXPERT_TILE = 512
COMBINE_CHUNK = 64
VMEM_LIMIT = 48 * 1024 * 1024


def _iota_f32(shape, axis):
    return lax.broadcasted_iota(I32, shape, axis).astype(F32)


def _mixer_kernel(pos0, tt, n_t,
                  x_ref, pp_ref, cp_ref, g1_ref, win_ref, bd01_ref, bd23_ref, pscale_ref,
                  cw_ref, cb_ref, lng_ref, lnb_ref, wout_ref, bout_ref, g2_ref, wrt_ref, br_ref,
                  tri_ref,
                  x1_ref, hm_ref, gates_ref, route_ref, cnt_ref, spool_ref, sconv_ref,
                  pext, cext, hbuf, carry, lv2, lv4, lv8, csh):
    b = pl.program_id(0)
    j = pl.program_id(1)

    @pl.when((b == 0) & (j == 0))
    def _():
        carry[...] = jnp.zeros_like(carry)

    @pl.when(j == 0)
    def _():
        pext[0:POOL_PAD, :] = pp_ref[...]
        for lb in range(CONV_BLOCKS):
            cext[lb, 0:CONV_PAD, :] = cp_ref[:, lb * LANES:(lb + 1) * LANES]

    @pl.when(j > 0)
    def _():
        pext[0:POOL_PAD, :] = pext[tt:tt + POOL_PAD, :]
        cext[:, 0:CONV_PAD, :] = cext[:, tt:tt + CONV_PAD, :]

    x = x_ref[...]
    ms = jnp.mean(x * x, axis=-1, keepdims=True)
    h = (x * lax.rsqrt(ms + EPS) * g1_ref[...]).astype(BF16)
    z = jnp.dot(h, win_ref[...], preferred_element_type=F32)
    u = z[:, :D_POOL]
    pext[POOL_PAD:POOL_PAD + tt, :] = u
    v = z[:, D_POOL:D_POOL + D_CONV] * jax.nn.sigmoid(z[:, D_POOL + D_CONV:])
    for lb in range(CONV_BLOCKS):
        cext[lb, CONV_PAD:CONV_PAD + tt, :] = v[:, lb * LANES:(lb + 1) * LANES]

    pos = pos0 + j * tt + lax.broadcasted_iota(I32, (tt, 1), 0)
    end = POOL_PAD + tt
    gd = POOL_GROUP_DIM
    lv2[8:end, :] = pext[8:end, :] + pext[7:end - 1, :]
    lv4[16:end, gd:] = lv2[16:end, gd:] + lv2[14:end - 2, gd:]
    lv8[24:end, 2 * gd:] = lv4[24:end, 2 * gd:] + lv4[20:end - 4, 2 * gd:]
    sums = [lv2[POOL_PAD:end, 0:gd], lv4[POOL_PAD:end, gd:2 * gd], lv8[POOL_PAD:end, 2 * gd:3 * gd],
            lv8[POOL_PAD:end, 3 * gd:] + lv8[POOL_PAD - 8:end - 8, 3 * gd:]]
    parts = []
    for g, w in enumerate(POOL_WINDOWS):
        sl = slice(g * gd, (g + 1) * gd)
        cnt = jnp.minimum(pos + 1, w).astype(F32)
        parts.append(sums[g] / cnt - u[:, sl])
    d = jnp.concatenate(parts, axis=-1).astype(BF16)
    half = D_POOL // 2
    yp = jnp.concatenate(
        [jnp.dot(d[:, :half], bd01_ref[...], preferred_element_type=F32),
         jnp.dot(d[:, half:], bd23_ref[...], preferred_element_type=F32)], axis=-1)
    yp = yp * pscale_ref[...]

    n_sh = CONV_PAD + tt - SUBLANES
    for s in range(1, SUBLANES):
        csh[s - 1, :, 0:n_sh, :] = cext[:, s:s + n_sh, :]

    for lb in range(CONV_BLOCKS):
        lanes = slice(lb * LANES, (lb + 1) * LANES)
        wts = [jnp.broadcast_to(cw_ref[k:k + 1, lanes], (CONV_CHUNK, LANES)) for k in range(CONV_WIDTH)]

        def conv_chunk(c, carry_, lb=lb, lanes=lanes, wts=wts):
            r0 = pl.multiple_of(c * CONV_CHUNK, CONV_CHUNK)
            acc = None
            for k in range(CONV_WIDTH):
                q, s = divmod(CONV_PAD - CONV_STATE + k, SUBLANES)
                rows = pl.ds(pl.multiple_of(r0 + q * SUBLANES, SUBLANES), CONV_CHUNK)
                tap = (cext[lb, rows, :] if s == 0 else csh[s - 1, lb, rows, :]) * wts[k]
                acc = tap if acc is None else acc + tap
            hbuf[pl.ds(r0, CONV_CHUNK), lanes] = acc
            return carry_

        lax.fori_loop(0, tt // CONV_CHUNK, conv_chunk, 0)
    hc = hbuf[...] + cb_ref[...]
    mu = jnp.mean(hc, axis=-1, keepdims=True)
    var = jnp.mean(jnp.square(hc - mu), axis=-1, keepdims=True)
    yln = (hc - mu) * lax.rsqrt(var + EPS) * lng_ref[...] + lnb_ref[...]
    yc = yln * jax.nn.sigmoid(yln)

    o = (jnp.dot(yp.astype(BF16), wout_ref[0:D_POOL, :], preferred_element_type=F32)
         + jnp.dot(yc.astype(BF16), wout_ref[D_POOL:, :], preferred_element_type=F32))
    x1 = x + (o + bout_ref[...])
    x1_ref[...] = x1

    ms2 = jnp.mean(x1 * x1, axis=-1, keepdims=True)
    hmb = (x1 * lax.rsqrt(ms2 + EPS) * g2_ref[...]).astype(BF16)
    packed = _pack_bf16_pairs(hmb.astype(F32))
    for p in range(HM_PIECES):
        hm_ref[pl.ds(p, tt, stride=HM_PIECES), :] = packed[:, p * LANES:(p + 1) * LANES]

    lt = lax.dot_general(wrt_ref[...], hmb, (((1,), (1,)), ((), ())),
                         preferred_element_type=F32) + br_ref[...]
    row8 = _iota_f32((SUBLANES, tt), 0)
    neg = jnp.float32(-jnp.inf)
    lg = jnp.where(row8 < N_EXPERT_GROUPS, lt[0:SUBLANES, :], neg)
    mg = jnp.max(lg, axis=0, keepdims=True)
    p_grp = 1.0 / jnp.sum(jnp.exp(lg - mg), axis=0, keepdims=True)
    gsel = jnp.min(jnp.where(lg == mg, row8, float(SUBLANES)), axis=0, keepdims=True)
    le = jnp.zeros((SUBLANES, tt), F32)
    for g in range(N_EXPERT_GROUPS):
        r0 = EXPERT_ROW0 + g * EXPERTS_PER_GROUP
        le = jnp.where(gsel == float(g), lt[r0:r0 + EXPERTS_PER_GROUP, :], le)
    m1 = jnp.max(le, axis=0, keepdims=True)
    i1 = jnp.min(jnp.where(le == m1, row8, float(SUBLANES)), axis=0, keepdims=True)
    le2 = jnp.where(row8 == i1, neg, le)
    m2 = jnp.max(le2, axis=0, keepdims=True)
    i2 = jnp.min(jnp.where(le2 == m2, row8, float(SUBLANES)), axis=0, keepdims=True)
    e2 = jnp.exp(m2 - m1)
    den = 1.0 + e2
    gate0 = p_grp * (1.0 / den)
    gate1 = p_grp * (e2 / den)
    eid0 = gsel * float(EXPERTS_PER_GROUP) + i1
    eid1 = gsel * float(EXPERTS_PER_GROUP) + i2

    rowe = _iota_f32((N_EXPERTS, tt), 0)
    oh0 = rowe == eid0
    oh1 = rowe == eid1
    ohf = jnp.where(jnp.logical_or(oh0, oh1), 1.0, 0.0)
    before = jnp.dot(ohf.astype(BF16), tri_ref[...], preferred_element_type=F32) + carry[:, 0:1]
    rank0 = jnp.sum(jnp.where(oh0, before, 0.0), axis=0, keepdims=True)
    rank1 = jnp.sum(jnp.where(oh1, before, 0.0), axis=0, keepdims=True)
    carry[...] = carry[...] + jnp.sum(ohf, axis=1, keepdims=True)
    cnt_ref[...] = carry[...]

    route = jnp.where(row8 == 0.0, eid0,
                      jnp.where(row8 == 1.0, eid1,
                                jnp.where(row8 == 2.0, rank0,
                                          jnp.where(row8 == 3.0, rank1, 0.0))))
    route_ref[...] = route.astype(I32)

    tpad = -(-tt // LANES) * LANES
    rowr = _iota_f32((ROUTER_ROWS, tt), 0)
    gt = jnp.where(rowr == 0.0, gate0, jnp.where(rowr == 1.0, gate1, 0.0))
    if tpad != tt:
        gt = jnp.concatenate([gt, jnp.zeros((ROUTER_ROWS, tpad - tt), F32)], axis=1)
    gates_ref[...] = gt.T[0:tt, :]

    @pl.when(j == n_t - 1)
    def _():
        spool_ref[...] = pext[POOL_PAD + tt - POOL_STATE:POOL_PAD + tt, :]
        sconv_ref[...] = jnp.concatenate(
            [cext[lb, CONV_PAD + tt - CONV_STATE:CONV_PAD + tt, :] for lb in range(CONV_BLOCKS)], axis=-1)


def _mixer_call(x, pool_prefix, conv_prefix, pos0, tt, weights, name):
    bsz, seq, _ = x.shape
    n_t = seq // tt
    n_blk = bsz * n_t
    n_rows = bsz * seq
    tri = (lax.broadcasted_iota(I32, (tt, tt), 0) < lax.broadcasted_iota(I32, (tt, tt), 1)).astype(BF16)

    def const(shape):
        return pl.BlockSpec(shape, lambda b, j: (0,) * len(shape))

    def rows(width):
        return pl.BlockSpec((tt, width), lambda b, j: (b * n_t + j, 0))

    in_specs = [
        pl.BlockSpec((None, tt, D_MODEL), lambda b, j: (b, j, 0)),
        pl.BlockSpec((None, POOL_PAD, D_POOL), lambda b, j: (b, 0, 0)),
        pl.BlockSpec((None, CONV_PAD, D_CONV), lambda b, j: (b, 0, 0)),
    ] + [const(w.shape) for w in weights] + [const((tt, tt))]
    out_shape = (
        jax.ShapeDtypeStruct((n_rows, D_MODEL), F32),
        jax.ShapeDtypeStruct((n_rows * HM_PIECES, LANES), U32),
        jax.ShapeDtypeStruct((n_rows, LANES), F32),
        jax.ShapeDtypeStruct((n_blk, SUBLANES, tt), I32),
        jax.ShapeDtypeStruct((N_EXPERTS, LANES), F32),
        jax.ShapeDtypeStruct((1, bsz, POOL_STATE, D_POOL), F32),
        jax.ShapeDtypeStruct((1, bsz, CONV_STATE, D_CONV), F32),
    )
    out_specs = (
        rows(D_MODEL),
        pl.BlockSpec((tt * HM_PIECES, LANES), lambda b, j: (b * n_t + j, 0)),
        rows(LANES),
        pl.BlockSpec((None, SUBLANES, tt), lambda b, j: (b * n_t + j, 0, 0)),
        pl.BlockSpec((N_EXPERTS, LANES), lambda b, j: (0, 0)),
        pl.BlockSpec((None, None, POOL_STATE, D_POOL), lambda b, j: (0, b, 0, 0)),
        pl.BlockSpec((None, None, CONV_STATE, D_CONV), lambda b, j: (0, b, 0, 0)),
    )
    return pl.pallas_call(
        functools.partial(_mixer_kernel, pos0, tt, n_t),
        out_shape=out_shape,
        grid=(bsz, n_t),
        in_specs=in_specs,
        out_specs=out_specs,
        scratch_shapes=[
            pltpu.VMEM((POOL_PAD + tt, D_POOL), F32),
            pltpu.VMEM((CONV_BLOCKS, CONV_PAD + tt, LANES), F32),
            pltpu.VMEM((tt, D_CONV), F32),
            pltpu.VMEM((N_EXPERTS, LANES), F32),
            pltpu.VMEM((POOL_PAD + tt, D_POOL), F32),
            pltpu.VMEM((POOL_PAD + tt, D_POOL), F32),
            pltpu.VMEM((POOL_PAD + tt, D_POOL), F32),
            pltpu.VMEM((SUBLANES - 1, CONV_BLOCKS, CONV_PAD + tt, LANES), F32),
        ],
        compiler_params=pltpu.CompilerParams(
            dimension_semantics=("arbitrary", "arbitrary"), vmem_limit_bytes=VMEM_LIMIT),
        name=name,
    )(x, pool_prefix, conv_prefix, *weights, tri)


def _row_copy_out(hm_ref, xb_ref, sem, t, d):
    src = hm_ref.at[pl.ds(pl.multiple_of(t * HM_PIECES, HM_PIECES), HM_PIECES), :]
    dst = xb_ref.at[pl.ds(pl.multiple_of(d * HM_PIECES, HM_PIECES), HM_PIECES), :]
    return pltpu.make_async_copy(src, dst, sem)


def _dispatch_kernel(dest_ref, hm_ref, xb_in_ref, xb_ref, sem):
    del xb_in_ref
    tc = hm_ref.shape[0] // HM_PIECES

    def start(t, c):
        for k in range(TOP_K):
            _row_copy_out(hm_ref, xb_ref, sem, t, dest_ref[0, k, t]).start(priority=k)
        return c

    for t in range(tc):
        start(t, 0)
    for k in range(TOP_K):
        pltpu.make_async_copy(hm_ref, xb_ref.at[pl.ds(0, tc * HM_PIECES), :], sem).wait()


def _dispatch_call(dest_blocks, hm, xb, name):
    n_rows = hm.shape[0] // HM_PIECES
    tc = TOKEN_TILE
    return pl.pallas_call(
        _dispatch_kernel,
        out_shape=jax.ShapeDtypeStruct(xb.shape, xb.dtype),
        grid=(n_rows // tc,),
        in_specs=[
            pl.BlockSpec((1, TOP_K, tc), lambda i: (i, 0, 0), memory_space=pltpu.SMEM),
            pl.BlockSpec((tc * HM_PIECES, LANES), lambda i: (i, 0)),
            pl.BlockSpec(memory_space=pl.ANY),
        ],
        out_specs=pl.BlockSpec(memory_space=pl.ANY),
        scratch_shapes=[pltpu.SemaphoreType.DMA(())],
        input_output_aliases={2: 0},
        compiler_params=pltpu.CompilerParams(dimension_semantics=("arbitrary",)),
        name=name,
    )(dest_blocks, hm, xb)


def _pack_bf16_pairs(x):
    w = x.shape[1] // 2
    rounded = x.astype(BF16).astype(F32)
    lo = lax.bitcast_convert_type(rounded[:, :w], U32)
    hi = lax.bitcast_convert_type(rounded[:, w:], U32)
    return (hi & jnp.uint32(0xFFFF0000)) | lax.shift_right_logical(lo, jnp.uint32(16))


def _unpack_bf16_pairs(words):
    lo = lax.bitcast_convert_type(lax.shift_left(words, jnp.uint32(16)), F32)
    hi = lax.bitcast_convert_type(words & jnp.uint32(0xFFFF0000), F32)
    return lo, hi


def _expert_kernel(t0_ref, nt_ref, used_ref, wg_ref, wu_ref, wd_ref, xb_ref, yb_ref,
                   wgb, wub, wdb, xbuf, ybuf, sem_in, sem_out):
    e = pl.program_id(0)
    tile_rows = xbuf.shape[1]
    tm = tile_rows // HM_PIECES
    t0 = t0_ref[e]
    n = nt_ref[e]
    wgb[...] = wg_ref[...].astype(BF16)
    wub[...] = wu_ref[...].astype(BF16)
    wdb[...] = wd_ref[...].astype(BF16)

    def tile(ref, j):
        return ref.at[pl.ds(pl.multiple_of((t0 + j) * tile_rows, tile_rows), tile_rows), :]

    def fetch(j, slot):
        return pltpu.make_async_copy(tile(xb_ref, j), xbuf.at[slot], sem_in.at[slot])

    def put(j, slot):
        return pltpu.make_async_copy(ybuf.at[slot], tile(yb_ref, j), sem_out.at[slot])

    @pl.when(n > 0)
    def _():
        fetch(0, 0).start()

    def body(j, carry):
        slot = j % 2
        fetch(j, slot).wait()

        @pl.when(j + 1 < n)
        def _():
            fetch(j + 1, 1 - slot).start()

        @pl.when(j >= 2)
        def _():
            put(j - 2, slot).wait()

        pieces = [_unpack_bf16_pairs(xbuf[slot, pl.ds(p, tm, stride=HM_PIECES), :]) for p in range(HM_PIECES)]
        xt = jnp.concatenate([lo.astype(BF16) for lo, _ in pieces] + [hi.astype(BF16) for _, hi in pieces],
                             axis=-1)
        g = jnp.dot(xt, wgb[...], preferred_element_type=F32)
        up = jnp.dot(xt, wub[...], preferred_element_type=F32)
        a = (g * jax.nn.sigmoid(g) * up).astype(BF16)
        packed = _pack_bf16_pairs(jnp.dot(a, wdb[...], preferred_element_type=F32))
        for p in range(HM_PIECES):
            ybuf[slot, pl.ds(p, tm, stride=HM_PIECES), :] = packed[:, p * LANES:(p + 1) * LANES]
        put(j, slot).start()
        return carry

    lax.fori_loop(0, n, body, 0)

    @pl.when(n >= 2)
    def _():
        put(n - 2, n % 2).wait()

    @pl.when(n >= 1)
    def _():
        put(n - 1, (n - 1) % 2).wait()

    @pl.when(e == pl.num_programs(0) - 1)
    def _():
        ybuf[0] = jnp.zeros(ybuf.shape[1:], ybuf.dtype)
        n_all = yb_ref.shape[0] // tile_rows

        def zero_tile(j, carry):
            dst = yb_ref.at[pl.ds(pl.multiple_of(j * tile_rows, tile_rows), tile_rows), :]
            cp = pltpu.make_async_copy(ybuf.at[0], dst, sem_out.at[0])
            cp.start()
            cp.wait()
            return carry

        lax.fori_loop(used_ref[0], n_all, zero_tile, 0)


def _expert_call(first_tile, n_tiles, n_used, xb, w_eg, w_eu, w_ed):
    tm = EXPERT_TILE
    tile_rows = tm * HM_PIECES
    grid_spec = pltpu.PrefetchScalarGridSpec(
        num_scalar_prefetch=3,
        grid=(N_EXPERTS,),
        in_specs=[
            pl.BlockSpec((None, D_MODEL, D_FF), lambda e, *_: (e, 0, 0)),
            pl.BlockSpec((None, D_MODEL, D_FF), lambda e, *_: (e, 0, 0)),
            pl.BlockSpec((None, D_FF, D_MODEL), lambda e, *_: (e, 0, 0)),
            pl.BlockSpec(memory_space=pl.ANY),
        ],
        out_specs=pl.BlockSpec(memory_space=pl.ANY),
        scratch_shapes=[
            pltpu.VMEM((D_MODEL, D_FF), BF16),
            pltpu.VMEM((D_MODEL, D_FF), BF16),
            pltpu.VMEM((D_FF, D_MODEL), BF16),
            pltpu.VMEM((2, tile_rows, LANES), U32),
            pltpu.VMEM((2, tile_rows, LANES), U32),
            pltpu.SemaphoreType.DMA((2,)),
            pltpu.SemaphoreType.DMA((2,)),
        ],
    )
    return pl.pallas_call(
        _expert_kernel,
        out_shape=jax.ShapeDtypeStruct(xb.shape, U32),
        grid_spec=grid_spec,
        compiler_params=pltpu.CompilerParams(
            dimension_semantics=("arbitrary",), vmem_limit_bytes=VMEM_LIMIT),
        name="experts",
    )(first_tile, n_tiles, n_used, w_eg, w_eu, w_ed, xb)


def _gather_rows(dest_ref, yb_ref, buf_ref, sem, tokens):
    for t in tokens:
        for k in range(TOP_K):
            d = dest_ref[0, k, t]
            src = yb_ref.at[pl.ds(pl.multiple_of(d * HM_PIECES, HM_PIECES), HM_PIECES), :]
            dst = buf_ref.at[k, pl.ds(t * HM_PIECES, HM_PIECES), :]
            pltpu.make_async_copy(src, dst, sem).start(priority=k)


def _wait_rows(yb_ref, buf_ref, sem):
    for k in range(TOP_K):
        pltpu.make_async_copy(yb_ref.at[pl.ds(0, buf_ref.shape[1]), :], buf_ref.at[k], sem).wait()


def _combine_kernel(dest0_ref, dest1_ref, dest2_ref, x1_ref, gates_ref, gf_ref, yb_ref, out_ref, buf, sems):
    i = pl.program_id(0)
    tc = TOKEN_TILE
    n_chunks = tc // COMBINE_CHUNK

    @pl.when(i == 0)
    def _():
        _gather_rows(dest0_ref, yb_ref, buf.at[0], sems.at[0], range(tc))

    gf = gf_ref[...]
    for half, next_dest in ((0, dest1_ref), (1, dest2_ref)):
        cur, nxt = buf.at[half], buf.at[1 - half]
        _wait_rows(yb_ref, cur, sems.at[half])
        for c in range(n_chunks):
            rows = pl.ds(half * tc + c * COMBINE_CHUNK, COMBINE_CHUNK)
            gates = gates_ref[rows, :]
            lows, highs = [], []
            for p in range(HM_PIECES):
                piece = pl.ds(c * COMBINE_CHUNK * HM_PIECES + p, COMBINE_CHUNK, stride=HM_PIECES)
                lo0, hi0 = _unpack_bf16_pairs(cur[0, piece, :])
                lo1, hi1 = _unpack_bf16_pairs(cur[1, piece, :])
                lows.append(lo0 * gates[:, 0:1] + lo1 * gates[:, 1:2])
                highs.append(hi0 * gates[:, 0:1] + hi1 * gates[:, 1:2])
            xo = x1_ref[rows, :] + jnp.concatenate(lows + highs, axis=-1)
            _gather_rows(next_dest, yb_ref, nxt, sems.at[1 - half],
                         range(c * COMBINE_CHUNK, (c + 1) * COMBINE_CHUNK))
            ms = jnp.mean(xo * xo, axis=-1, keepdims=True)
            out_ref[rows, :] = xo * lax.rsqrt(ms + EPS) * gf

    @pl.when(i == pl.num_programs(0) - 1)
    def _():
        _wait_rows(yb_ref, buf.at[0], sems.at[0])


def _combine_call(dest_blocks, x1, gates, gf, yb, name):
    n_rows = x1.shape[0]
    tc = TOKEN_TILE
    n_tiles = n_rows // tc
    assert n_tiles % 2 == 0

    def dest_spec(index_map):
        return pl.BlockSpec((1, TOP_K, tc), index_map, memory_space=pltpu.SMEM)

    return pl.pallas_call(
        _combine_kernel,
        out_shape=jax.ShapeDtypeStruct((n_rows, D_MODEL), F32),
        grid=(n_tiles // 2,),
        in_specs=[
            dest_spec(lambda i: (0, 0, 0)),
            dest_spec(lambda i: (2 * i + 1, 0, 0)),
            dest_spec(lambda i: (jnp.minimum(2 * i + 2, n_tiles - 1), 0, 0)),
            pl.BlockSpec((2 * tc, D_MODEL), lambda i: (i, 0)),
            pl.BlockSpec((2 * tc, LANES), lambda i: (i, 0)),
            pl.BlockSpec((1, D_MODEL), lambda i: (0, 0)),
            pl.BlockSpec(memory_space=pl.ANY),
        ],
        out_specs=pl.BlockSpec((2 * tc, D_MODEL), lambda i: (i, 0)),
        scratch_shapes=[pltpu.VMEM((2, TOP_K, tc * HM_PIECES, LANES), U32), pltpu.SemaphoreType.DMA((2,))],
        compiler_params=pltpu.CompilerParams(
            dimension_semantics=("arbitrary",), vmem_limit_bytes=VMEM_LIMIT),
        name=name,
    )(dest_blocks, dest_blocks, dest_blocks, x1, gates, gf, yb)


def _block_diag2(a, b):
    z = jnp.zeros_like(a)
    return jnp.concatenate([jnp.concatenate([a, z], axis=1), jnp.concatenate([z, b], axis=1)], axis=0)


def kernel(x_prompt, x_sample, state_pool, state_conv, norm1_g, w_in, pool_lin, pool_scale, conv_dw, conv_dw_b, conv_ln_g, conv_ln_b, w_out, b_out, norm2_g, w_rg, b_rg, w_re, b_re, w_eg, w_eu, w_ed, norm_f_g):
    assert norm1_g.shape[0] == 1, "single-layer trunk"
    bsz, seq, _ = x_prompt.shape
    dbsz, dseq, _ = x_sample.shape
    past_len = 1024
    n_prompt = bsz * seq
    n_sample = dbsz * dseq
    n_tokens = n_prompt + n_sample
    assert seq % PROMPT_TILE == 0 and n_prompt % TOKEN_TILE == 0 and n_sample % TOKEN_TILE == 0

    row = lambda a: a.reshape(1, -1)
    router_w = jnp.zeros((ROUTER_ROWS, D_MODEL), F32)
    router_w = router_w.at[0:N_EXPERT_GROUPS].set(w_rg[0].T)
    router_w = router_w.at[EXPERT_ROW0:EXPERT_ROW0 + N_EXPERTS].set(
        jnp.transpose(w_re[0], (0, 2, 1)).reshape(N_EXPERTS, D_MODEL))
    router_b = jnp.zeros((ROUTER_ROWS,), F32)
    router_b = router_b.at[0:N_EXPERT_GROUPS].set(b_rg[0])
    router_b = router_b.at[EXPERT_ROW0:EXPERT_ROW0 + N_EXPERTS].set(b_re[0].reshape(-1))
    conv_w = jnp.concatenate([conv_dw[0], jnp.zeros((1, D_CONV), F32)], axis=0)
    weights = [
        row(norm1_g[0]), w_in[0].astype(BF16),
        _block_diag2(pool_lin[0, 0], pool_lin[0, 1]).astype(BF16),
        _block_diag2(pool_lin[0, 2], pool_lin[0, 3]).astype(BF16),
        row(pool_scale[0]), conv_w, row(conv_dw_b[0]), row(conv_ln_g[0]), row(conv_ln_b[0]),
        w_out[0].astype(BF16), row(b_out[0]), row(norm2_g[0]),
        router_w.astype(BF16), router_b.reshape(ROUTER_ROWS, 1),
    ]

    zpool = jnp.zeros((bsz, POOL_PAD, D_POOL), F32)
    zconv = jnp.zeros((bsz, CONV_PAD, D_CONV), F32)
    ppool = jnp.pad(state_pool[0], ((0, 0), (POOL_PAD - POOL_STATE, 0), (0, 0)))
    pconv = jnp.pad(state_conv[0], ((0, 0), (CONV_PAD - CONV_STATE, 0), (0, 0)))

    x1_p, hm_p, gates_p, route_p, cnt_p, sp_p, sc_p = _mixer_call(
        x_prompt, zpool, zconv, 0, PROMPT_TILE, weights, "mixer_prompt")
    x1_s, hm_s, gates_s, route_s, cnt_s, sp_s, sc_s = _mixer_call(
        x_sample, ppool, pconv, past_len, dseq, weights, "mixer_sample")

    tm = EXPERT_TILE
    cnt_p = cnt_p[:, 0].astype(I32)
    cnt_s = cnt_s[:, 0].astype(I32)
    padded = (cnt_p + cnt_s + tm - 1) // tm * tm
    pad_end = jnp.cumsum(padded)
    pad_start = pad_end - padded

    def dest_blocks(route, base):
        eid = jnp.stack([route[:, k, :].reshape(-1) for k in range(TOP_K)])
        rank = jnp.stack([route[:, TOP_K + k, :].reshape(-1) for k in range(TOP_K)])
        experts = jnp.arange(N_EXPERTS, dtype=I32)
        dest = rank + jnp.sum(jnp.where(eid[..., None] == experts, base, 0), axis=-1)
        return dest.reshape(TOP_K, -1, TOKEN_TILE).transpose(1, 0, 2)

    dest_p = dest_blocks(route_p, pad_start)
    dest_s = dest_blocks(route_s, pad_start + cnt_p)
    n_tiles = -(-(n_tokens * TOP_K + N_EXPERTS * (tm - 1)) // tm)
    n_used = (pad_end[-1:] // tm).astype(I32)

    xb = jnp.zeros((n_tiles * tm * HM_PIECES, LANES), U32)
    xb = _dispatch_call(dest_p, hm_p, xb, "dispatch_prompt")
    xb = _dispatch_call(dest_s, hm_s, xb, "dispatch_sample")
    yb = _expert_call((pad_start // tm).astype(I32), (padded // tm).astype(I32), n_used,
                      xb, w_eg[0], w_eu[0], w_ed[0])
    gf = row(norm_f_g)
    y_p = _combine_call(dest_p, x1_p, gates_p, gf, yb, "combine_prompt")
    y_s = _combine_call(dest_s, x1_s, gates_s, gf, yb, "combine_sample")

    return (y_p.reshape(bsz, seq, D_MODEL), y_s.reshape(dbsz, dseq, D_MODEL),
            sp_p, sc_p, sp_s, sc_s)
```

```python
import functools

import jax
import jax.numpy as jnp
from jax import lax
from jax.experimental import pallas as pl
from jax.experimental.pallas import tpu as pltpu

F32 = jnp.float32
BF16 = jnp.bfloat16
U32 = jnp.uint32
I32 = jnp.int32

D_MODEL = 1024
D_POOL = 512
D_CONV = 512
D_IN = D_POOL + 2 * D_CONV
POOL_WINDOWS = (2, 4, 8, 16)
POOL_GROUP_DIM = D_POOL // len(POOL_WINDOWS)
POOL_STATE = max(POOL_WINDOWS) - 1
CONV_WIDTH = 31
CONV_STATE = CONV_WIDTH - 1
N_EXPERT_GROUPS = 4
EXPERTS_PER_GROUP = 8
N_EXPERTS = N_EXPERT_GROUPS * EXPERTS_PER_GROUP
TOP_K = 2
D_FF = D_MODEL // 2
EPS = 1e-6

SUBLANES = 8
LANES = 128
POOL_PAD = 32
CONV_PAD = 32
ROUTER_ROWS = LANES
EXPERT_ROW0 = SUBLANES
CONV_CHUNK = 64
CONV_BLOCKS = D_CONV // LANES
HM_WORDS = D_MODEL // 2
HM_PIECES = HM_WORDS // LANES

PROMPT_TILE = 512
TOKEN_TILE = 512
EXPERT_TILE = 512
X_SLOTS = 3
COMBINE_CHUNK = 64
VMEM_LIMIT = 48 * 1024 * 1024


def _iota_f32(shape, axis):
    return lax.broadcasted_iota(I32, shape, axis).astype(F32)


def _mixer_kernel(pos0, tt, n_t,
                  x_ref, pp_ref, cp_ref, g1_ref, win_ref, bd01_ref, bd23_ref, pscale_ref,
                  cw_ref, cb_ref, lng_ref, lnb_ref, wout_ref, bout_ref, g2_ref, wrt_ref, br_ref,
                  tri_ref,
                  x1_ref, hm_ref, gates_ref, route_ref, cnt_ref, spool_ref, sconv_ref,
                  pext, cext, hbuf, carry, lv2, lv4, lv8, csh):
    b = pl.program_id(0)
    j = pl.program_id(1)

    @pl.when((b == 0) & (j == 0))
    def _():
        carry[...] = jnp.zeros_like(carry)

    @pl.when(j == 0)
    def _():
        pext[0:POOL_PAD, :] = pp_ref[...]
        for lb in range(CONV_BLOCKS):
            cext[lb, 0:CONV_PAD, :] = cp_ref[:, lb * LANES:(lb + 1) * LANES]

    @pl.when(j > 0)
    def _():
        pext[0:POOL_PAD, :] = pext[tt:tt + POOL_PAD, :]
        cext[:, 0:CONV_PAD, :] = cext[:, tt:tt + CONV_PAD, :]

    x = x_ref[...]
    ms = jnp.mean(x * x, axis=-1, keepdims=True)
    h = (x * lax.rsqrt(ms + EPS) * g1_ref[...]).astype(BF16)
    z = jnp.dot(h, win_ref[...], preferred_element_type=F32)
    u = z[:, :D_POOL]
    pext[POOL_PAD:POOL_PAD + tt, :] = u
    v = z[:, D_POOL:D_POOL + D_CONV] * jax.nn.sigmoid(z[:, D_POOL + D_CONV:])
    for lb in range(CONV_BLOCKS):
        cext[lb, CONV_PAD:CONV_PAD + tt, :] = v[:, lb * LANES:(lb + 1) * LANES]

    pos = pos0 + j * tt + lax.broadcasted_iota(I32, (tt, 1), 0)
    end = POOL_PAD + tt
    gd = POOL_GROUP_DIM
    lv2[8:end, :] = pext[8:end, :] + pext[7:end - 1, :]
    lv4[16:end, gd:] = lv2[16:end, gd:] + lv2[14:end - 2, gd:]
    lv8[24:end, 2 * gd:] = lv4[24:end, 2 * gd:] + lv4[20:end - 4, 2 * gd:]
    sums = [lv2[POOL_PAD:end, 0:gd], lv4[POOL_PAD:end, gd:2 * gd], lv8[POOL_PAD:end, 2 * gd:3 * gd],
            lv8[POOL_PAD:end, 3 * gd:] + lv8[POOL_PAD - 8:end - 8, 3 * gd:]]
    parts = []
    for g, w in enumerate(POOL_WINDOWS):
        sl = slice(g * gd, (g + 1) * gd)
        cnt = jnp.minimum(pos + 1, w).astype(F32)
        parts.append(sums[g] / cnt - u[:, sl])
    d = jnp.concatenate(parts, axis=-1).astype(BF16)
    half = D_POOL // 2
    yp = jnp.concatenate(
        [jnp.dot(d[:, :half], bd01_ref[...], preferred_element_type=F32),
         jnp.dot(d[:, half:], bd23_ref[...], preferred_element_type=F32)], axis=-1)
    yp = yp * pscale_ref[...]

    n_sh = CONV_PAD + tt - SUBLANES
    for s in range(1, SUBLANES):
        csh[s - 1, :, 0:n_sh, :] = cext[:, s:s + n_sh, :]

    for lb in range(CONV_BLOCKS):
        lanes = slice(lb * LANES, (lb + 1) * LANES)
        wts = [jnp.broadcast_to(cw_ref[k:k + 1, lanes], (CONV_CHUNK, LANES)) for k in range(CONV_WIDTH)]

        def conv_chunk(c, carry_, lb=lb, lanes=lanes, wts=wts):
            r0 = pl.multiple_of(c * CONV_CHUNK, CONV_CHUNK)
            acc = None
            for k in range(CONV_WIDTH):
                q, s = divmod(CONV_PAD - CONV_STATE + k, SUBLANES)
                rows = pl.ds(pl.multiple_of(r0 + q * SUBLANES, SUBLANES), CONV_CHUNK)
                tap = (cext[lb, rows, :] if s == 0 else csh[s - 1, lb, rows, :]) * wts[k]
                acc = tap if acc is None else acc + tap
            hbuf[pl.ds(r0, CONV_CHUNK), lanes] = acc
            return carry_

        lax.fori_loop(0, tt // CONV_CHUNK, conv_chunk, 0)
    hc = hbuf[...] + cb_ref[...]
    mu = jnp.mean(hc, axis=-1, keepdims=True)
    var = jnp.mean(jnp.square(hc - mu), axis=-1, keepdims=True)
    yln = (hc - mu) * lax.rsqrt(var + EPS) * lng_ref[...] + lnb_ref[...]
    yc = yln * jax.nn.sigmoid(yln)

    o = (jnp.dot(yp.astype(BF16), wout_ref[0:D_POOL, :], preferred_element_type=F32)
         + jnp.dot(yc.astype(BF16), wout_ref[D_POOL:, :], preferred_element_type=F32))
    x1 = x + (o + bout_ref[...])
    x1_ref[...] = x1

    ms2 = jnp.mean(x1 * x1, axis=-1, keepdims=True)
    hmb = (x1 * lax.rsqrt(ms2 + EPS) * g2_ref[...]).astype(BF16)
    packed = _pack_bf16_pairs(hmb.astype(F32))
    for p in range(HM_PIECES):
        hm_ref[pl.ds(p, tt, stride=HM_PIECES), :] = packed[:, p * LANES:(p + 1) * LANES]

    lt = lax.dot_general(wrt_ref[...], hmb, (((1,), (1,)), ((), ())),
                         preferred_element_type=F32) + br_ref[...]
    row8 = _iota_f32((SUBLANES, tt), 0)
    neg = jnp.float32(-jnp.inf)
    lg = jnp.where(row8 < N_EXPERT_GROUPS, lt[0:SUBLANES, :], neg)
    mg = jnp.max(lg, axis=0, keepdims=True)
    p_grp = 1.0 / jnp.sum(jnp.exp(lg - mg), axis=0, keepdims=True)
    gsel = jnp.min(jnp.where(lg == mg, row8, float(SUBLANES)), axis=0, keepdims=True)
    le = jnp.zeros((SUBLANES, tt), F32)
    for g in range(N_EXPERT_GROUPS):
        r0 = EXPERT_ROW0 + g * EXPERTS_PER_GROUP
        le = jnp.where(gsel == float(g), lt[r0:r0 + EXPERTS_PER_GROUP, :], le)
    m1 = jnp.max(le, axis=0, keepdims=True)
    i1 = jnp.min(jnp.where(le == m1, row8, float(SUBLANES)), axis=0, keepdims=True)
    le2 = jnp.where(row8 == i1, neg, le)
    m2 = jnp.max(le2, axis=0, keepdims=True)
    i2 = jnp.min(jnp.where(le2 == m2, row8, float(SUBLANES)), axis=0, keepdims=True)
    e2 = jnp.exp(m2 - m1)
    den = 1.0 + e2
    gate0 = p_grp * (1.0 / den)
    gate1 = p_grp * (e2 / den)
    eid0 = gsel * float(EXPERTS_PER_GROUP) + i1
    eid1 = gsel * float(EXPERTS_PER_GROUP) + i2

    rowe = _iota_f32((N_EXPERTS, tt), 0)
    oh0 = rowe == eid0
    oh1 = rowe == eid1
    ohf = jnp.where(jnp.logical_or(oh0, oh1), 1.0, 0.0)
    before = jnp.dot(ohf.astype(BF16), tri_ref[...], preferred_element_type=F32) + carry[:, 0:1]
    rank0 = jnp.sum(jnp.where(oh0, before, 0.0), axis=0, keepdims=True)
    rank1 = jnp.sum(jnp.where(oh1, before, 0.0), axis=0, keepdims=True)
    carry[...] = carry[...] + jnp.sum(ohf, axis=1, keepdims=True)
    cnt_ref[...] = carry[...]

    route = jnp.where(row8 == 0.0, eid0,
                      jnp.where(row8 == 1.0, eid1,
                                jnp.where(row8 == 2.0, rank0,
                                          jnp.where(row8 == 3.0, rank1, 0.0))))
    route_ref[...] = route.astype(I32)

    tpad = -(-tt // LANES) * LANES
    rowr = _iota_f32((ROUTER_ROWS, tt), 0)
    gt = jnp.where(rowr == 0.0, gate0, jnp.where(rowr == 1.0, gate1, 0.0))
    if tpad != tt:
        gt = jnp.concatenate([gt, jnp.zeros((ROUTER_ROWS, tpad - tt), F32)], axis=1)
    gates_ref[...] = gt.T[0:tt, :]

    @pl.when(j == n_t - 1)
    def _():
        spool_ref[...] = pext[POOL_PAD + tt - POOL_STATE:POOL_PAD + tt, :]
        sconv_ref[...] = jnp.concatenate(
            [cext[lb, CONV_PAD + tt - CONV_STATE:CONV_PAD + tt, :] for lb in range(CONV_BLOCKS)], axis=-1)


def _mixer_call(x, pool_prefix, conv_prefix, pos0, tt, weights, name):
    bsz, seq, _ = x.shape
    n_t = seq // tt
    n_blk = bsz * n_t
    n_rows = bsz * seq
    tri = (lax.broadcasted_iota(I32, (tt, tt), 0) < lax.broadcasted_iota(I32, (tt, tt), 1)).astype(BF16)

    def const(shape):
        return pl.BlockSpec(shape, lambda b, j: (0,) * len(shape))

    def rows(width):
        return pl.BlockSpec((tt, width), lambda b, j: (b * n_t + j, 0))

    in_specs = [
        pl.BlockSpec((None, tt, D_MODEL), lambda b, j: (b, j, 0)),
        pl.BlockSpec((None, POOL_PAD, D_POOL), lambda b, j: (b, 0, 0)),
        pl.BlockSpec((None, CONV_PAD, D_CONV), lambda b, j: (b, 0, 0)),
    ] + [const(w.shape) for w in weights] + [const((tt, tt))]
    out_shape = (
        jax.ShapeDtypeStruct((n_rows, D_MODEL), F32),
        jax.ShapeDtypeStruct((n_rows * HM_PIECES, LANES), U32),
        jax.ShapeDtypeStruct((n_rows, LANES), F32),
        jax.ShapeDtypeStruct((n_blk, SUBLANES, tt), I32),
        jax.ShapeDtypeStruct((N_EXPERTS, LANES), F32),
        jax.ShapeDtypeStruct((1, bsz, POOL_STATE, D_POOL), F32),
        jax.ShapeDtypeStruct((1, bsz, CONV_STATE, D_CONV), F32),
    )
    out_specs = (
        rows(D_MODEL),
        pl.BlockSpec((tt * HM_PIECES, LANES), lambda b, j: (b * n_t + j, 0)),
        rows(LANES),
        pl.BlockSpec((None, SUBLANES, tt), lambda b, j: (b * n_t + j, 0, 0)),
        pl.BlockSpec((N_EXPERTS, LANES), lambda b, j: (0, 0)),
        pl.BlockSpec((None, None, POOL_STATE, D_POOL), lambda b, j: (0, b, 0, 0)),
        pl.BlockSpec((None, None, CONV_STATE, D_CONV), lambda b, j: (0, b, 0, 0)),
    )
    return pl.pallas_call(
        functools.partial(_mixer_kernel, pos0, tt, n_t),
        out_shape=out_shape,
        grid=(bsz, n_t),
        in_specs=in_specs,
        out_specs=out_specs,
        scratch_shapes=[
            pltpu.VMEM((POOL_PAD + tt, D_POOL), F32),
            pltpu.VMEM((CONV_BLOCKS, CONV_PAD + tt, LANES), F32),
            pltpu.VMEM((tt, D_CONV), F32),
            pltpu.VMEM((N_EXPERTS, LANES), F32),
            pltpu.VMEM((POOL_PAD + tt, D_POOL), F32),
            pltpu.VMEM((POOL_PAD + tt, D_POOL), F32),
            pltpu.VMEM((POOL_PAD + tt, D_POOL), F32),
            pltpu.VMEM((SUBLANES - 1, CONV_BLOCKS, CONV_PAD + tt, LANES), F32),
        ],
        compiler_params=pltpu.CompilerParams(
            dimension_semantics=("arbitrary", "arbitrary"), vmem_limit_bytes=VMEM_LIMIT),
        name=name,
    )(x, pool_prefix, conv_prefix, *weights, tri)


def _row_copy_out(hm_ref, xb_ref, sem, t, d):
    src = hm_ref.at[pl.ds(pl.multiple_of(t * HM_PIECES, HM_PIECES), HM_PIECES), :]
    dst = xb_ref.at[pl.ds(pl.multiple_of(d * HM_PIECES, HM_PIECES), HM_PIECES), :]
    return pltpu.make_async_copy(src, dst, sem)


def _dispatch_kernel(dest_ref, hm_ref, xb_in_ref, xb_ref, sem):
    del xb_in_ref
    tc = hm_ref.shape[0] // HM_PIECES

    def start(t, c):
        for k in range(TOP_K):
            _row_copy_out(hm_ref, xb_ref, sem, t, dest_ref[0, k, t]).start(priority=k)
        return c

    for t in range(tc):
        start(t, 0)
    for k in range(TOP_K):
        pltpu.make_async_copy(hm_ref, xb_ref.at[pl.ds(0, tc * HM_PIECES), :], sem).wait()


def _dispatch_call(dest_blocks, hm, xb, name):
    n_rows = hm.shape[0] // HM_PIECES
    tc = TOKEN_TILE
    return pl.pallas_call(
        _dispatch_kernel,
        out_shape=jax.ShapeDtypeStruct(xb.shape, xb.dtype),
        grid=(n_rows // tc,),
        in_specs=[
            pl.BlockSpec((1, TOP_K, tc), lambda i: (i, 0, 0), memory_space=pltpu.SMEM),
            pl.BlockSpec((tc * HM_PIECES, LANES), lambda i: (i, 0)),
            pl.BlockSpec(memory_space=pl.ANY),
        ],
        out_specs=pl.BlockSpec(memory_space=pl.ANY),
        scratch_shapes=[pltpu.SemaphoreType.DMA(())],
        input_output_aliases={2: 0},
        compiler_params=pltpu.CompilerParams(dimension_semantics=("arbitrary",)),
        name=name,
    )(dest_blocks, hm, xb)


def _pack_bf16_pairs(x):
    w = x.shape[1] // 2
    rounded = x.astype(BF16).astype(F32)
    lo = lax.bitcast_convert_type(rounded[:, :w], U32)
    hi = lax.bitcast_convert_type(rounded[:, w:], U32)
    return (hi & jnp.uint32(0xFFFF0000)) | lax.shift_right_logical(lo, jnp.uint32(16))


def _unpack_bf16_pairs(words):
    lo = lax.bitcast_convert_type(lax.shift_left(words, jnp.uint32(16)), F32)
    hi = lax.bitcast_convert_type(words & jnp.uint32(0xFFFF0000), F32)
    return lo, hi


def _expert_kernel(ord_ref, act_ref, used_ref, nact_ref, wg_ref, wu_ref, wd_ref, xb_ref, yb_ref,
                   wgf, wuf, wdf, wgb, wub, wdb, xbuf, ybuf, sem_w, sem_in, sem_out):
    tile_rows = xbuf.shape[1]
    tm = tile_rows // HM_PIECES
    n = used_ref[0]
    n_act = nact_ref[0]

    def weight_copies(k, slot):
        e = act_ref[k]
        return [pltpu.make_async_copy(src.at[e], dst.at[slot], sem_w.at[slot])
                for src, dst in ((wg_ref, wgf), (wu_ref, wuf), (wd_ref, wdf))]

    def tile(ref, t):
        return ref.at[pl.ds(pl.multiple_of(t * tile_rows, tile_rows), tile_rows), :]

    def fetch(t, slot):
        return pltpu.make_async_copy(tile(xb_ref, t), xbuf.at[slot], sem_in.at[slot])

    def put(t, slot):
        return pltpu.make_async_copy(ybuf.at[slot], tile(yb_ref, t), sem_out.at[slot])

    for cp in weight_copies(0, 0):
        cp.start()
    for ahead in range(X_SLOTS - 1):
        @pl.when(ahead < n)
        def _(ahead=ahead):
            fetch(ahead, ahead).start()

    def body(t, carry):
        k = ord_ref[t]

        @pl.when((t == 0) | (k != ord_ref[jnp.maximum(t - 1, 0)]))
        def _():
            wslot = lax.rem(k, 2)
            for cp in weight_copies(k, wslot):
                cp.wait()
            wgb[...] = wgf[wslot].astype(BF16)
            wub[...] = wuf[wslot].astype(BF16)
            wdb[...] = wdf[wslot].astype(BF16)

            @pl.when(k + 1 < n_act)
            def _():
                for cp in weight_copies(k + 1, 1 - wslot):
                    cp.start()

        slot = lax.rem(t, X_SLOTS)
        fetch(t, slot).wait()

        @pl.when(t + X_SLOTS - 1 < n)
        def _():
            fetch(t + X_SLOTS - 1, lax.rem(t + X_SLOTS - 1, X_SLOTS)).start()

        oslot = lax.rem(t, 2)

        @pl.when(t >= 2)
        def _():
            put(t - 2, oslot).wait()

        pieces = [_unpack_bf16_pairs(xbuf[slot, pl.ds(p, tm, stride=HM_PIECES), :]) for p in range(HM_PIECES)]
        xt = jnp.concatenate([lo.astype(BF16) for lo, _ in pieces] + [hi.astype(BF16) for _, hi in pieces],
                             axis=-1)
        g = jnp.dot(xt, wgb[...], preferred_element_type=F32)
        up = jnp.dot(xt, wub[...], preferred_element_type=F32)
        a = (g * jax.nn.sigmoid(g) * up).astype(BF16)
        packed = _pack_bf16_pairs(jnp.dot(a, wdb[...], preferred_element_type=F32))
        for p in range(HM_PIECES):
            ybuf[oslot, pl.ds(p, tm, stride=HM_PIECES), :] = packed[:, p * LANES:(p + 1) * LANES]
        put(t, oslot).start()
        return carry

    lax.fori_loop(0, n, body, 0)

    @pl.when(n >= 2)
    def _():
        put(n - 2, lax.rem(n, 2)).wait()

    @pl.when(n >= 1)
    def _():
        put(n - 1, lax.rem(n - 1, 2)).wait()

    ybuf[0] = jnp.zeros(ybuf.shape[1:], ybuf.dtype)
    n_all = yb_ref.shape[0] // tile_rows

    def zero_tile(t, carry):
        cp = pltpu.make_async_copy(ybuf.at[0], tile(yb_ref, t), sem_out.at[0])
        cp.start()
        cp.wait()
        return carry

    lax.fori_loop(n, n_all, zero_tile, 0)


def _expert_call(tile_ord, active, n_used, n_active, xb, w_eg, w_eu, w_ed):
    tile_rows = EXPERT_TILE * HM_PIECES
    any_spec = pl.BlockSpec(memory_space=pl.ANY)
    grid_spec = pltpu.PrefetchScalarGridSpec(
        num_scalar_prefetch=4,
        grid=(1,),
        in_specs=[any_spec] * 4,
        out_specs=any_spec,
        scratch_shapes=[
            pltpu.VMEM((2, D_MODEL, D_FF), F32),
            pltpu.VMEM((2, D_MODEL, D_FF), F32),
            pltpu.VMEM((2, D_FF, D_MODEL), F32),
            pltpu.VMEM((D_MODEL, D_FF), BF16),
            pltpu.VMEM((D_MODEL, D_FF), BF16),
            pltpu.VMEM((D_FF, D_MODEL), BF16),
            pltpu.VMEM((X_SLOTS, tile_rows, LANES), U32),
            pltpu.VMEM((2, tile_rows, LANES), U32),
            pltpu.SemaphoreType.DMA((2,)),
            pltpu.SemaphoreType.DMA((X_SLOTS,)),
            pltpu.SemaphoreType.DMA((2,)),
        ],
    )
    return pl.pallas_call(
        _expert_kernel,
        out_shape=jax.ShapeDtypeStruct(xb.shape, U32),
        grid_spec=grid_spec,
        compiler_params=pltpu.CompilerParams(
            dimension_semantics=("arbitrary",), vmem_limit_bytes=VMEM_LIMIT),
        name="experts",
    )(tile_ord, active, n_used, n_active, w_eg, w_eu, w_ed, xb)


def _gather_rows(dest_ref, yb_ref, buf_ref, sem, tokens):
    for t in tokens:
        for k in range(TOP_K):
            d = dest_ref[0, k, t]
            src = yb_ref.at[pl.ds(pl.multiple_of(d * HM_PIECES, HM_PIECES), HM_PIECES), :]
            dst = buf_ref.at[k, pl.ds(t * HM_PIECES, HM_PIECES), :]
            pltpu.make_async_copy(src, dst, sem).start(priority=k)


def _wait_rows(yb_ref, buf_ref, sem):
    for k in range(TOP_K):
        pltpu.make_async_copy(yb_ref.at[pl.ds(0, buf_ref.shape[1]), :], buf_ref.at[k], sem).wait()


def _combine_kernel(dest0_ref, dest1_ref, dest2_ref, x1_ref, gates_ref, gf_ref, yb_ref, out_ref, buf, sems):
    i = pl.program_id(0)
    tc = TOKEN_TILE
    n_chunks = tc // COMBINE_CHUNK

    @pl.when(i == 0)
    def _():
        _gather_rows(dest0_ref, yb_ref, buf.at[0], sems.at[0], range(tc))

    gf = gf_ref[...]
    for half, next_dest in ((0, dest1_ref), (1, dest2_ref)):
        cur, nxt = buf.at[half], buf.at[1 - half]
        _wait_rows(yb_ref, cur, sems.at[half])
        for c in range(n_chunks):
            rows = pl.ds(half * tc + c * COMBINE_CHUNK, COMBINE_CHUNK)
            gates = gates_ref[rows, :]
            lows, highs = [], []
            for p in range(HM_PIECES):
                piece = pl.ds(c * COMBINE_CHUNK * HM_PIECES + p, COMBINE_CHUNK, stride=HM_PIECES)
                lo0, hi0 = _unpack_bf16_pairs(cur[0, piece, :])
                lo1, hi1 = _unpack_bf16_pairs(cur[1, piece, :])
                lows.append(lo0 * gates[:, 0:1] + lo1 * gates[:, 1:2])
                highs.append(hi0 * gates[:, 0:1] + hi1 * gates[:, 1:2])
            xo = x1_ref[rows, :] + jnp.concatenate(lows + highs, axis=-1)
            _gather_rows(next_dest, yb_ref, nxt, sems.at[1 - half],
                         range(c * COMBINE_CHUNK, (c + 1) * COMBINE_CHUNK))
            ms = jnp.mean(xo * xo, axis=-1, keepdims=True)
            out_ref[rows, :] = xo * lax.rsqrt(ms + EPS) * gf

    @pl.when(i == pl.num_programs(0) - 1)
    def _():
        _wait_rows(yb_ref, buf.at[0], sems.at[0])


def _combine_call(dest_blocks, x1, gates, gf, yb, name):
    n_rows = x1.shape[0]
    tc = TOKEN_TILE
    n_tiles = n_rows // tc
    assert n_tiles % 2 == 0

    def dest_spec(index_map):
        return pl.BlockSpec((1, TOP_K, tc), index_map, memory_space=pltpu.SMEM)

    return pl.pallas_call(
        _combine_kernel,
        out_shape=jax.ShapeDtypeStruct((n_rows, D_MODEL), F32),
        grid=(n_tiles // 2,),
        in_specs=[
            dest_spec(lambda i: (0, 0, 0)),
            dest_spec(lambda i: (2 * i + 1, 0, 0)),
            dest_spec(lambda i: (jnp.minimum(2 * i + 2, n_tiles - 1), 0, 0)),
            pl.BlockSpec((2 * tc, D_MODEL), lambda i: (i, 0)),
            pl.BlockSpec((2 * tc, LANES), lambda i: (i, 0)),
            pl.BlockSpec((1, D_MODEL), lambda i: (0, 0)),
            pl.BlockSpec(memory_space=pl.ANY),
        ],
        out_specs=pl.BlockSpec((2 * tc, D_MODEL), lambda i: (i, 0)),
        scratch_shapes=[pltpu.VMEM((2, TOP_K, tc * HM_PIECES, LANES), U32), pltpu.SemaphoreType.DMA((2,))],
        compiler_params=pltpu.CompilerParams(
            dimension_semantics=("arbitrary",), vmem_limit_bytes=VMEM_LIMIT),
        name=name,
    )(dest_blocks, dest_blocks, dest_blocks, x1, gates, gf, yb)


def _block_diag2(a, b):
    z = jnp.zeros_like(a)
    return jnp.concatenate([jnp.concatenate([a, z], axis=1), jnp.concatenate([z, b], axis=1)], axis=0)


def kernel(x_prompt, x_sample, state_pool, state_conv, norm1_g, w_in, pool_lin, pool_scale, conv_dw, conv_dw_b, conv_ln_g, conv_ln_b, w_out, b_out, norm2_g, w_rg, b_rg, w_re, b_re, w_eg, w_eu, w_ed, norm_f_g):
    assert norm1_g.shape[0] == 1, "single-layer trunk"
    bsz, seq, _ = x_prompt.shape
    dbsz, dseq, _ = x_sample.shape
    past_len = 1024
    n_prompt = bsz * seq
    n_sample = dbsz * dseq
    n_tokens = n_prompt + n_sample
    assert seq % PROMPT_TILE == 0 and n_prompt % TOKEN_TILE == 0 and n_sample % TOKEN_TILE == 0

    row = lambda a: a.reshape(1, -1)
    router_w = jnp.zeros((ROUTER_ROWS, D_MODEL), F32)
    router_w = router_w.at[0:N_EXPERT_GROUPS].set(w_rg[0].T)
    router_w = router_w.at[EXPERT_ROW0:EXPERT_ROW0 + N_EXPERTS].set(
        jnp.transpose(w_re[0], (0, 2, 1)).reshape(N_EXPERTS, D_MODEL))
    router_b = jnp.zeros((ROUTER_ROWS,), F32)
    router_b = router_b.at[0:N_EXPERT_GROUPS].set(b_rg[0])
    router_b = router_b.at[EXPERT_ROW0:EXPERT_ROW0 + N_EXPERTS].set(b_re[0].reshape(-1))
    conv_w = jnp.concatenate([conv_dw[0], jnp.zeros((1, D_CONV), F32)], axis=0)
    weights = [
        row(norm1_g[0]), w_in[0].astype(BF16),
        _block_diag2(pool_lin[0, 0], pool_lin[0, 1]).astype(BF16),
        _block_diag2(pool_lin[0, 2], pool_lin[0, 3]).astype(BF16),
        row(pool_scale[0]), conv_w, row(conv_dw_b[0]), row(conv_ln_g[0]), row(conv_ln_b[0]),
        w_out[0].astype(BF16), row(b_out[0]), row(norm2_g[0]),
        router_w.astype(BF16), router_b.reshape(ROUTER_ROWS, 1),
    ]

    zpool = jnp.zeros((bsz, POOL_PAD, D_POOL), F32)
    zconv = jnp.zeros((bsz, CONV_PAD, D_CONV), F32)
    ppool = jnp.pad(state_pool[0], ((0, 0), (POOL_PAD - POOL_STATE, 0), (0, 0)))
    pconv = jnp.pad(state_conv[0], ((0, 0), (CONV_PAD - CONV_STATE, 0), (0, 0)))

    x1_p, hm_p, gates_p, route_p, cnt_p, sp_p, sc_p = _mixer_call(
        x_prompt, zpool, zconv, 0, PROMPT_TILE, weights, "mixer_prompt")
    x1_s, hm_s, gates_s, route_s, cnt_s, sp_s, sc_s = _mixer_call(
        x_sample, ppool, pconv, past_len, dseq, weights, "mixer_sample")

    tm = EXPERT_TILE
    cnt_p = cnt_p[:, 0].astype(I32)
    cnt_s = cnt_s[:, 0].astype(I32)
    padded = (cnt_p + cnt_s + tm - 1) // tm * tm
    pad_end = jnp.cumsum(padded)
    pad_start = pad_end - padded

    def dest_blocks(route, base):
        eid = jnp.stack([route[:, k, :].reshape(-1) for k in range(TOP_K)])
        rank = jnp.stack([route[:, TOP_K + k, :].reshape(-1) for k in range(TOP_K)])
        experts = jnp.arange(N_EXPERTS, dtype=I32)
        dest = rank + jnp.sum(jnp.where(eid[..., None] == experts, base, 0), axis=-1)
        return dest.reshape(TOP_K, -1, TOKEN_TILE).transpose(1, 0, 2)

    dest_p = dest_blocks(route_p, pad_start)
    dest_s = dest_blocks(route_s, pad_start + cnt_p)
    n_tiles = -(-(n_tokens * TOP_K + N_EXPERTS * (tm - 1)) // tm)
    n_used = (pad_end[-1:] // tm).astype(I32)

    xb = jnp.zeros((n_tiles * tm * HM_PIECES, LANES), U32)
    xb = _dispatch_call(dest_p, hm_p, xb, "dispatch_prompt")
    xb = _dispatch_call(dest_s, hm_s, xb, "dispatch_sample")
    owns = padded > 0
    active = jnp.argsort(jnp.logical_not(owns), stable=True).astype(I32)
    tile_expert = jnp.minimum(
        jnp.sum(pad_end[None, :] <= (jnp.arange(n_tiles, dtype=I32) * tm)[:, None], axis=1), N_EXPERTS - 1)
    ordinal = jnp.cumsum(owns.astype(I32)) - 1
    tile_ord = jnp.sum(jnp.where(tile_expert[:, None] == jnp.arange(N_EXPERTS, dtype=I32), ordinal, 0),
                       axis=1).astype(I32)
    n_active = jnp.sum(owns.astype(I32)).reshape(1)
    yb = _expert_call(tile_ord, active, n_used, n_active, xb, w_eg[0], w_eu[0], w_ed[0])
    gf = row(norm_f_g)
    y_p = _combine_call(dest_p, x1_p, gates_p, gf, yb, "combine_prompt")
    y_s = _combine_call(dest_s, x1_s, gates_s, gf, yb, "combine_sample")

    return (y_p.reshape(bsz, seq, D_MODEL), y_s.reshape(dbsz, dseq, D_MODEL),
            sp_p, sc_p, sp_s, sc_s)
```

```python
import functools

import jax
import jax.numpy as jnp
from jax import lax
from jax.experimental import pallas as pl
from jax.experimental.pallas import tpu as pltpu

F32 = jnp.float32
BF16 = jnp.bfloat16
U32 = jnp.uint32
I32 = jnp.int32

D_MODEL = 1024
D_POOL = 512
D_CONV = 512
D_IN = D_POOL + 2 * D_CONV
POOL_WINDOWS = (2, 4, 8, 16)
POOL_GROUP_DIM = D_POOL // len(POOL_WINDOWS)
POOL_STATE = max(POOL_WINDOWS) - 1
CONV_WIDTH = 31
CONV_STATE = CONV_WIDTH - 1
N_EXPERT_GROUPS = 4
EXPERTS_PER_GROUP = 8
N_EXPERTS = N_EXPERT_GROUPS * EXPERTS_PER_GROUP
TOP_K = 2
D_FF = D_MODEL // 2
EPS = 1e-6

SUBLANES = 8
LANES = 128
POOL_PAD = 32
CONV_PAD = 32
ROUTER_ROWS = LANES
EXPERT_ROW0 = SUBLANES
CONV_CHUNK = 64
CONV_BLOCKS = D_CONV // LANES
MIXER_PART = 256
HM_WORDS = D_MODEL // 2
HM_PIECES = HM_WORDS // LANES

PROMPT_TILE = 512
TOKEN_TILE = 512
EXPERT_TILE = 512
X_SLOTS = 3
COMBINE_CHUNK = 64
VMEM_LIMIT = 48 * 1024 * 1024


def _iota_f32(shape, axis):
    return lax.broadcasted_iota(I32, shape, axis).astype(F32)


def _mixer_kernel(pos0, tt, n_t,
                  x_ref, pp_ref, cp_ref, g1_ref, win_ref, bd01_ref, bd23_ref, pscale_ref,
                  cw_ref, cb_ref, lng_ref, lnb_ref, wout_ref, bout_ref, g2_ref, wrt_ref, br_ref,
                  tri_ref,
                  x1_ref, hm_ref, gates_ref, route_ref, cnt_ref, spool_ref, sconv_ref,
                  pext, cext, hbuf, carry, lv2, lv4, lv8, csh):
    b = pl.program_id(0)
    j = pl.program_id(1)

    @pl.when((b == 0) & (j == 0))
    def _():
        carry[...] = jnp.zeros_like(carry)

    @pl.when(j == 0)
    def _():
        pext[0:POOL_PAD, :] = pp_ref[...]
        for lb in range(CONV_BLOCKS):
            cext[lb, 0:CONV_PAD, :] = cp_ref[:, lb * LANES:(lb + 1) * LANES]

    @pl.when(j > 0)
    def _():
        pext[0:POOL_PAD, :] = pext[tt:tt + POOL_PAD, :]
        cext[:, 0:CONV_PAD, :] = cext[:, tt:tt + CONV_PAD, :]

    n_parts = max(1, tt // MIXER_PART)
    rp = tt // n_parts
    gd = POOL_GROUP_DIM
    neg = jnp.float32(-jnp.inf)

    for part in range(n_parts):
        r0 = part * rp
        x = x_ref[r0:r0 + rp, :]
        ms = jnp.mean(x * x, axis=-1, keepdims=True)
        h = (x * lax.rsqrt(ms + EPS) * g1_ref[...]).astype(BF16)
        z = jnp.dot(h, win_ref[...], preferred_element_type=F32)
        pext[POOL_PAD + r0:POOL_PAD + r0 + rp, :] = z[:, :D_POOL]
        v = z[:, D_POOL:D_POOL + D_CONV] * jax.nn.sigmoid(z[:, D_POOL + D_CONV:])
        for lb in range(CONV_BLOCKS):
            cext[lb, CONV_PAD + r0:CONV_PAD + r0 + rp, :] = v[:, lb * LANES:(lb + 1) * LANES]

    for part in range(n_parts):
        r0 = part * rp
        x = x_ref[r0:r0 + rp, :]
        pos = pos0 + j * tt + r0 + lax.broadcasted_iota(I32, (rp, 1), 0)
        top = r0 + POOL_PAD
        end = top + rp
        lv2[r0 + 8:end, :] = pext[r0 + 8:end, :] + pext[r0 + 7:end - 1, :]
        lv4[r0 + 16:end, gd:] = lv2[r0 + 16:end, gd:] + lv2[r0 + 14:end - 2, gd:]
        lv8[r0 + 24:end, 2 * gd:] = lv4[r0 + 24:end, 2 * gd:] + lv4[r0 + 20:end - 4, 2 * gd:]
        sums = [lv2[top:end, 0:gd], lv4[top:end, gd:2 * gd], lv8[top:end, 2 * gd:3 * gd],
                lv8[top:end, 3 * gd:] + lv8[top - 8:end - 8, 3 * gd:]]
        u = pext[top:end, :]
        parts = []
        for g, w in enumerate(POOL_WINDOWS):
            sl = slice(g * gd, (g + 1) * gd)
            cnt = jnp.minimum(pos + 1, w).astype(F32)
            parts.append(sums[g] / cnt - u[:, sl])
        d = jnp.concatenate(parts, axis=-1).astype(BF16)
        half = D_POOL // 2
        yp = jnp.concatenate(
            [jnp.dot(d[:, :half], bd01_ref[...], preferred_element_type=F32),
             jnp.dot(d[:, half:], bd23_ref[...], preferred_element_type=F32)], axis=-1)
        yp = yp * pscale_ref[...]

        n_sh = CONV_PAD + rp - SUBLANES
        for s in range(1, SUBLANES):
            csh[s - 1, :, r0:r0 + n_sh, :] = cext[:, r0 + s:r0 + s + n_sh, :]
        for lb in range(CONV_BLOCKS):
            lanes = slice(lb * LANES, (lb + 1) * LANES)
            wts = [jnp.broadcast_to(cw_ref[k:k + 1, lanes], (CONV_CHUNK, LANES)) for k in range(CONV_WIDTH)]
            for c in range(rp // CONV_CHUNK):
                acc = None
                for k in range(CONV_WIDTH):
                    q, s = divmod(CONV_PAD - CONV_STATE + k, SUBLANES)
                    rows = pl.ds(r0 + c * CONV_CHUNK + q * SUBLANES, CONV_CHUNK)
                    tap = (cext[lb, rows, :] if s == 0 else csh[s - 1, lb, rows, :]) * wts[k]
                    acc = tap if acc is None else acc + tap
                hbuf[r0 + c * CONV_CHUNK:r0 + (c + 1) * CONV_CHUNK, lanes] = acc
        hc = hbuf[r0:r0 + rp, :] + cb_ref[...]
        mu = jnp.mean(hc, axis=-1, keepdims=True)
        var = jnp.mean(jnp.square(hc - mu), axis=-1, keepdims=True)
        yln = (hc - mu) * lax.rsqrt(var + EPS) * lng_ref[...] + lnb_ref[...]
        yc = yln * jax.nn.sigmoid(yln)

        o = (jnp.dot(yp.astype(BF16), wout_ref[0:D_POOL, :], preferred_element_type=F32)
             + jnp.dot(yc.astype(BF16), wout_ref[D_POOL:, :], preferred_element_type=F32))
        x1 = x + (o + bout_ref[...])
        x1_ref[r0:r0 + rp, :] = x1

        ms2 = jnp.mean(x1 * x1, axis=-1, keepdims=True)
        hmb = (x1 * lax.rsqrt(ms2 + EPS) * g2_ref[...]).astype(BF16)
        packed = _pack_bf16_pairs(hmb.astype(F32))
        for p in range(HM_PIECES):
            hm_ref[pl.ds(r0 * HM_PIECES + p, rp, stride=HM_PIECES), :] = packed[:, p * LANES:(p + 1) * LANES]

        lt = lax.dot_general(wrt_ref[...], hmb, (((1,), (1,)), ((), ())),
                             preferred_element_type=F32) + br_ref[...]
        row8 = _iota_f32((SUBLANES, rp), 0)
        lg = jnp.where(row8 < N_EXPERT_GROUPS, lt[0:SUBLANES, :], neg)
        mg = jnp.max(lg, axis=0, keepdims=True)
        p_grp = 1.0 / jnp.sum(jnp.exp(lg - mg), axis=0, keepdims=True)
        gsel = jnp.min(jnp.where(lg == mg, row8, float(SUBLANES)), axis=0, keepdims=True)
        le = jnp.zeros((SUBLANES, rp), F32)
        for g in range(N_EXPERT_GROUPS):
            e0 = EXPERT_ROW0 + g * EXPERTS_PER_GROUP
            le = jnp.where(gsel == float(g), lt[e0:e0 + EXPERTS_PER_GROUP, :], le)
        m1 = jnp.max(le, axis=0, keepdims=True)
        i1 = jnp.min(jnp.where(le == m1, row8, float(SUBLANES)), axis=0, keepdims=True)
        le2 = jnp.where(row8 == i1, neg, le)
        m2 = jnp.max(le2, axis=0, keepdims=True)
        i2 = jnp.min(jnp.where(le2 == m2, row8, float(SUBLANES)), axis=0, keepdims=True)
        e2 = jnp.exp(m2 - m1)
        den = 1.0 + e2
        gate0 = p_grp * (1.0 / den)
        gate1 = p_grp * (e2 / den)
        eid0 = gsel * float(EXPERTS_PER_GROUP) + i1
        eid1 = gsel * float(EXPERTS_PER_GROUP) + i2

        rowe = _iota_f32((N_EXPERTS, rp), 0)
        oh0 = rowe == eid0
        oh1 = rowe == eid1
        ohf = jnp.where(jnp.logical_or(oh0, oh1), 1.0, 0.0)
        before = jnp.dot(ohf.astype(BF16), tri_ref[0:rp, 0:rp], preferred_element_type=F32) + carry[:, 0:1]
        rank0 = jnp.sum(jnp.where(oh0, before, 0.0), axis=0, keepdims=True)
        rank1 = jnp.sum(jnp.where(oh1, before, 0.0), axis=0, keepdims=True)
        carry[...] = carry[...] + jnp.sum(ohf, axis=1, keepdims=True)

        route = jnp.where(row8 == 0.0, eid0,
                          jnp.where(row8 == 1.0, eid1,
                                    jnp.where(row8 == 2.0, rank0,
                                              jnp.where(row8 == 3.0, rank1, 0.0))))
        route_ref[:, r0:r0 + rp] = route.astype(I32)

        tpad = -(-rp // LANES) * LANES
        rowr = _iota_f32((ROUTER_ROWS, rp), 0)
        gt = jnp.where(rowr == 0.0, gate0, jnp.where(rowr == 1.0, gate1, 0.0))
        if tpad != rp:
            gt = jnp.concatenate([gt, jnp.zeros((ROUTER_ROWS, tpad - rp), F32)], axis=1)
        gates_ref[r0:r0 + rp, :] = gt.T[0:rp, :]

    cnt_ref[...] = carry[...]

    @pl.when(j == n_t - 1)
    def _():
        spool_ref[...] = pext[POOL_PAD + tt - POOL_STATE:POOL_PAD + tt, :]
        sconv_ref[...] = jnp.concatenate(
            [cext[lb, CONV_PAD + tt - CONV_STATE:CONV_PAD + tt, :] for lb in range(CONV_BLOCKS)], axis=-1)


def _mixer_call(x, pool_prefix, conv_prefix, pos0, tt, weights, name):
    bsz, seq, _ = x.shape
    n_t = seq // tt
    n_blk = bsz * n_t
    n_rows = bsz * seq
    tri = (lax.broadcasted_iota(I32, (tt, tt), 0) < lax.broadcasted_iota(I32, (tt, tt), 1)).astype(BF16)

    def const(shape):
        return pl.BlockSpec(shape, lambda b, j: (0,) * len(shape))

    def rows(width):
        return pl.BlockSpec((tt, width), lambda b, j: (b * n_t + j, 0))

    in_specs = [
        pl.BlockSpec((None, tt, D_MODEL), lambda b, j: (b, j, 0)),
        pl.BlockSpec((None, POOL_PAD, D_POOL), lambda b, j: (b, 0, 0)),
        pl.BlockSpec((None, CONV_PAD, D_CONV), lambda b, j: (b, 0, 0)),
    ] + [const(w.shape) for w in weights] + [const((tt, tt))]
    out_shape = (
        jax.ShapeDtypeStruct((n_rows, D_MODEL), F32),
        jax.ShapeDtypeStruct((n_rows * HM_PIECES, LANES), U32),
        jax.ShapeDtypeStruct((n_rows, LANES), F32),
        jax.ShapeDtypeStruct((n_blk, SUBLANES, tt), I32),
        jax.ShapeDtypeStruct((N_EXPERTS, LANES), F32),
        jax.ShapeDtypeStruct((1, bsz, POOL_STATE, D_POOL), F32),
        jax.ShapeDtypeStruct((1, bsz, CONV_STATE, D_CONV), F32),
    )
    out_specs = (
        rows(D_MODEL),
        pl.BlockSpec((tt * HM_PIECES, LANES), lambda b, j: (b * n_t + j, 0)),
        rows(LANES),
        pl.BlockSpec((None, SUBLANES, tt), lambda b, j: (b * n_t + j, 0, 0)),
        pl.BlockSpec((N_EXPERTS, LANES), lambda b, j: (0, 0)),
        pl.BlockSpec((None, None, POOL_STATE, D_POOL), lambda b, j: (0, b, 0, 0)),
        pl.BlockSpec((None, None, CONV_STATE, D_CONV), lambda b, j: (0, b, 0, 0)),
    )
    return pl.pallas_call(
        functools.partial(_mixer_kernel, pos0, tt, n_t),
        out_shape=out_shape,
        grid=(bsz, n_t),
        in_specs=in_specs,
        out_specs=out_specs,
        scratch_shapes=[
            pltpu.VMEM((POOL_PAD + tt, D_POOL), F32),
            pltpu.VMEM((CONV_BLOCKS, CONV_PAD + tt, LANES), F32),
            pltpu.VMEM((tt, D_CONV), F32),
            pltpu.VMEM((N_EXPERTS, LANES), F32),
            pltpu.VMEM((POOL_PAD + tt, D_POOL), F32),
            pltpu.VMEM((POOL_PAD + tt, D_POOL), F32),
            pltpu.VMEM((POOL_PAD + tt, D_POOL), F32),
            pltpu.VMEM((SUBLANES - 1, CONV_BLOCKS, CONV_PAD + tt, LANES), F32),
        ],
        compiler_params=pltpu.CompilerParams(
            dimension_semantics=("arbitrary", "arbitrary"), vmem_limit_bytes=VMEM_LIMIT),
        name=name,
    )(x, pool_prefix, conv_prefix, *weights, tri)


def _row_copy_out(hm_ref, xb_ref, sem, t, d):
    src = hm_ref.at[pl.ds(pl.multiple_of(t * HM_PIECES, HM_PIECES), HM_PIECES), :]
    dst = xb_ref.at[pl.ds(pl.multiple_of(d * HM_PIECES, HM_PIECES), HM_PIECES), :]
    return pltpu.make_async_copy(src, dst, sem)


def _dispatch_kernel(dest_ref, hm_ref, xb_in_ref, xb_ref, sem):
    del xb_in_ref
    tc = hm_ref.shape[0] // HM_PIECES

    def start(t, c):
        for k in range(TOP_K):
            _row_copy_out(hm_ref, xb_ref, sem, t, dest_ref[0, k, t]).start(priority=k)
        return c

    for t in range(tc):
        start(t, 0)
    for k in range(TOP_K):
        pltpu.make_async_copy(hm_ref, xb_ref.at[pl.ds(0, tc * HM_PIECES), :], sem).wait()


def _dispatch_call(dest_blocks, hm, xb, name):
    n_rows = hm.shape[0] // HM_PIECES
    tc = TOKEN_TILE
    return pl.pallas_call(
        _dispatch_kernel,
        out_shape=jax.ShapeDtypeStruct(xb.shape, xb.dtype),
        grid=(n_rows // tc,),
        in_specs=[
            pl.BlockSpec((1, TOP_K, tc), lambda i: (i, 0, 0), memory_space=pltpu.SMEM),
            pl.BlockSpec((tc * HM_PIECES, LANES), lambda i: (i, 0)),
            pl.BlockSpec(memory_space=pl.ANY),
        ],
        out_specs=pl.BlockSpec(memory_space=pl.ANY),
        scratch_shapes=[pltpu.SemaphoreType.DMA(())],
        input_output_aliases={2: 0},
        compiler_params=pltpu.CompilerParams(dimension_semantics=("arbitrary",)),
        name=name,
    )(dest_blocks, hm, xb)


def _pack_bf16_pairs(x):
    w = x.shape[1] // 2
    rounded = x.astype(BF16).astype(F32)
    lo = lax.bitcast_convert_type(rounded[:, :w], U32)
    hi = lax.bitcast_convert_type(rounded[:, w:], U32)
    return (hi & jnp.uint32(0xFFFF0000)) | lax.shift_right_logical(lo, jnp.uint32(16))


def _unpack_bf16_pairs(words):
    lo = lax.bitcast_convert_type(lax.shift_left(words, jnp.uint32(16)), F32)
    hi = lax.bitcast_convert_type(words & jnp.uint32(0xFFFF0000), F32)
    return lo, hi


def _expert_kernel(ord_ref, act_ref, used_ref, nact_ref, wg_ref, wu_ref, wd_ref, xb_ref, yb_ref,
                   wgf, wuf, wdf, wgb, wub, wdb, xbuf, ybuf, sem_w, sem_in, sem_out):
    tile_rows = xbuf.shape[1]
    tm = tile_rows // HM_PIECES
    n = used_ref[0]
    n_act = nact_ref[0]

    def weight_copies(k, slot):
        e = act_ref[k]
        return [pltpu.make_async_copy(src.at[e], dst.at[slot], sem_w.at[slot])
                for src, dst in ((wg_ref, wgf), (wu_ref, wuf), (wd_ref, wdf))]

    def tile(ref, t):
        return ref.at[pl.ds(pl.multiple_of(t * tile_rows, tile_rows), tile_rows), :]

    def fetch(t, slot):
        return pltpu.make_async_copy(tile(xb_ref, t), xbuf.at[slot], sem_in.at[slot])

    def put(t, slot):
        return pltpu.make_async_copy(ybuf.at[slot], tile(yb_ref, t), sem_out.at[slot])

    for cp in weight_copies(0, 0):
        cp.start()
    for ahead in range(X_SLOTS - 1):
        @pl.when(ahead < n)
        def _(ahead=ahead):
            fetch(ahead, ahead).start()

    def body(t, carry):
        k = ord_ref[t]

        @pl.when((t == 0) | (k != ord_ref[jnp.maximum(t - 1, 0)]))
        def _():
            wslot = lax.rem(k, 2)
            for cp in weight_copies(k, wslot):
                cp.wait()
            wgb[...] = wgf[wslot].astype(BF16)
            wub[...] = wuf[wslot].astype(BF16)
            wdb[...] = wdf[wslot].astype(BF16)

            @pl.when(k + 1 < n_act)
            def _():
                for cp in weight_copies(k + 1, 1 - wslot):
                    cp.start()

        slot = lax.rem(t, X_SLOTS)
        fetch(t, slot).wait()

        @pl.when(t + X_SLOTS - 1 < n)
        def _():
            fetch(t + X_SLOTS - 1, lax.rem(t + X_SLOTS - 1, X_SLOTS)).start()

        oslot = lax.rem(t, 2)

        @pl.when(t >= 2)
        def _():
            put(t - 2, oslot).wait()

        pieces = [_unpack_bf16_pairs(xbuf[slot, pl.ds(p, tm, stride=HM_PIECES), :]) for p in range(HM_PIECES)]
        xt = jnp.concatenate([lo.astype(BF16) for lo, _ in pieces] + [hi.astype(BF16) for _, hi in pieces],
                             axis=-1)
        g = jnp.dot(xt, wgb[...], preferred_element_type=F32)
        up = jnp.dot(xt, wub[...], preferred_element_type=F32)
        a = (g * jax.nn.sigmoid(g) * up).astype(BF16)
        packed = _pack_bf16_pairs(jnp.dot(a, wdb[...], preferred_element_type=F32))
        for p in range(HM_PIECES):
            ybuf[oslot, pl.ds(p, tm, stride=HM_PIECES), :] = packed[:, p * LANES:(p + 1) * LANES]
        put(t, oslot).start()
        return carry

    lax.fori_loop(0, n, body, 0)

    @pl.when(n >= 2)
    def _():
        put(n - 2, lax.rem(n, 2)).wait()

    @pl.when(n >= 1)
    def _():
        put(n - 1, lax.rem(n - 1, 2)).wait()

    ybuf[0] = jnp.zeros(ybuf.shape[1:], ybuf.dtype)
    n_all = yb_ref.shape[0] // tile_rows

    def zero_tile(t, carry):
        cp = pltpu.make_async_copy(ybuf.at[0], tile(yb_ref, t), sem_out.at[0])
        cp.start()
        cp.wait()
        return carry

    lax.fori_loop(n, n_all, zero_tile, 0)


def _expert_call(tile_ord, active, n_used, n_active, xb, w_eg, w_eu, w_ed):
    tile_rows = EXPERT_TILE * HM_PIECES
    any_spec = pl.BlockSpec(memory_space=pl.ANY)
    grid_spec = pltpu.PrefetchScalarGridSpec(
        num_scalar_prefetch=4,
        grid=(1,),
        in_specs=[any_spec] * 4,
        out_specs=any_spec,
        scratch_shapes=[
            pltpu.VMEM((2, D_MODEL, D_FF), F32),
            pltpu.VMEM((2, D_MODEL, D_FF), F32),
            pltpu.VMEM((2, D_FF, D_MODEL), F32),
            pltpu.VMEM((D_MODEL, D_FF), BF16),
            pltpu.VMEM((D_MODEL, D_FF), BF16),
            pltpu.VMEM((D_FF, D_MODEL), BF16),
            pltpu.VMEM((X_SLOTS, tile_rows, LANES), U32),
            pltpu.VMEM((2, tile_rows, LANES), U32),
            pltpu.SemaphoreType.DMA((2,)),
            pltpu.SemaphoreType.DMA((X_SLOTS,)),
            pltpu.SemaphoreType.DMA((2,)),
        ],
    )
    return pl.pallas_call(
        _expert_kernel,
        out_shape=jax.ShapeDtypeStruct(xb.shape, U32),
        grid_spec=grid_spec,
        compiler_params=pltpu.CompilerParams(
            dimension_semantics=("arbitrary",), vmem_limit_bytes=VMEM_LIMIT),
        name="experts",
    )(tile_ord, active, n_used, n_active, w_eg, w_eu, w_ed, xb)


def _gather_rows(dest_ref, yb_ref, buf_ref, sem, tokens):
    for t in tokens:
        for k in range(TOP_K):
            d = dest_ref[0, k, t]
            src = yb_ref.at[pl.ds(pl.multiple_of(d * HM_PIECES, HM_PIECES), HM_PIECES), :]
            dst = buf_ref.at[k, pl.ds(t * HM_PIECES, HM_PIECES), :]
            pltpu.make_async_copy(src, dst, sem).start(priority=k)


def _wait_rows(yb_ref, buf_ref, sem):
    for k in range(TOP_K):
        pltpu.make_async_copy(yb_ref.at[pl.ds(0, buf_ref.shape[1]), :], buf_ref.at[k], sem).wait()


def _combine_kernel(dest0_ref, dest1_ref, dest2_ref, x1_ref, gates_ref, gf_ref, yb_ref, out_ref, buf, sems):
    i = pl.program_id(0)
    tc = TOKEN_TILE
    n_chunks = tc // COMBINE_CHUNK

    @pl.when(i == 0)
    def _():
        _gather_rows(dest0_ref, yb_ref, buf.at[0], sems.at[0], range(tc))

    gf = gf_ref[...]
    for half, next_dest in ((0, dest1_ref), (1, dest2_ref)):
        cur, nxt = buf.at[half], buf.at[1 - half]
        _wait_rows(yb_ref, cur, sems.at[half])
        for c in range(n_chunks):
            rows = pl.ds(half * tc + c * COMBINE_CHUNK, COMBINE_CHUNK)
            gates = gates_ref[rows, :]
            lows, highs = [], []
            for p in range(HM_PIECES):
                piece = pl.ds(c * COMBINE_CHUNK * HM_PIECES + p, COMBINE_CHUNK, stride=HM_PIECES)
                lo0, hi0 = _unpack_bf16_pairs(cur[0, piece, :])
                lo1, hi1 = _unpack_bf16_pairs(cur[1, piece, :])
                lows.append(lo0 * gates[:, 0:1] + lo1 * gates[:, 1:2])
                highs.append(hi0 * gates[:, 0:1] + hi1 * gates[:, 1:2])
            xo = x1_ref[rows, :] + jnp.concatenate(lows + highs, axis=-1)
            _gather_rows(next_dest, yb_ref, nxt, sems.at[1 - half],
                         range(c * COMBINE_CHUNK, (c + 1) * COMBINE_CHUNK))
            ms = jnp.mean(xo * xo, axis=-1, keepdims=True)
            out_ref[rows, :] = xo * lax.rsqrt(ms + EPS) * gf

    @pl.when(i == pl.num_programs(0) - 1)
    def _():
        _wait_rows(yb_ref, buf.at[0], sems.at[0])


def _combine_call(dest_blocks, x1, gates, gf, yb, name):
    n_rows = x1.shape[0]
    tc = TOKEN_TILE
    n_tiles = n_rows // tc
    assert n_tiles % 2 == 0

    def dest_spec(index_map):
        return pl.BlockSpec((1, TOP_K, tc), index_map, memory_space=pltpu.SMEM)

    return pl.pallas_call(
        _combine_kernel,
        out_shape=jax.ShapeDtypeStruct((n_rows, D_MODEL), F32),
        grid=(n_tiles // 2,),
        in_specs=[
            dest_spec(lambda i: (0, 0, 0)),
            dest_spec(lambda i: (2 * i + 1, 0, 0)),
            dest_spec(lambda i: (jnp.minimum(2 * i + 2, n_tiles - 1), 0, 0)),
            pl.BlockSpec((2 * tc, D_MODEL), lambda i: (i, 0)),
            pl.BlockSpec((2 * tc, LANES), lambda i: (i, 0)),
            pl.BlockSpec((1, D_MODEL), lambda i: (0, 0)),
            pl.BlockSpec(memory_space=pl.ANY),
        ],
        out_specs=pl.BlockSpec((2 * tc, D_MODEL), lambda i: (i, 0)),
        scratch_shapes=[pltpu.VMEM((2, TOP_K, tc * HM_PIECES, LANES), U32), pltpu.SemaphoreType.DMA((2,))],
        compiler_params=pltpu.CompilerParams(
            dimension_semantics=("arbitrary",), vmem_limit_bytes=VMEM_LIMIT),
        name=name,
    )(dest_blocks, dest_blocks, dest_blocks, x1, gates, gf, yb)


def _block_diag2(a, b):
    z = jnp.zeros_like(a)
    return jnp.concatenate([jnp.concatenate([a, z], axis=1), jnp.concatenate([z, b], axis=1)], axis=0)


def kernel(x_prompt, x_sample, state_pool, state_conv, norm1_g, w_in, pool_lin, pool_scale, conv_dw, conv_dw_b, conv_ln_g, conv_ln_b, w_out, b_out, norm2_g, w_rg, b_rg, w_re, b_re, w_eg, w_eu, w_ed, norm_f_g):
    assert norm1_g.shape[0] == 1, "single-layer trunk"
    bsz, seq, _ = x_prompt.shape
    dbsz, dseq, _ = x_sample.shape
    past_len = 1024
    n_prompt = bsz * seq
    n_sample = dbsz * dseq
    n_tokens = n_prompt + n_sample
    assert seq % PROMPT_TILE == 0 and n_prompt % TOKEN_TILE == 0 and n_sample % TOKEN_TILE == 0

    row = lambda a: a.reshape(1, -1)
    router_w = jnp.zeros((ROUTER_ROWS, D_MODEL), F32)
    router_w = router_w.at[0:N_EXPERT_GROUPS].set(w_rg[0].T)
    router_w = router_w.at[EXPERT_ROW0:EXPERT_ROW0 + N_EXPERTS].set(
        jnp.transpose(w_re[0], (0, 2, 1)).reshape(N_EXPERTS, D_MODEL))
    router_b = jnp.zeros((ROUTER_ROWS,), F32)
    router_b = router_b.at[0:N_EXPERT_GROUPS].set(b_rg[0])
    router_b = router_b.at[EXPERT_ROW0:EXPERT_ROW0 + N_EXPERTS].set(b_re[0].reshape(-1))
    conv_w = jnp.concatenate([conv_dw[0], jnp.zeros((1, D_CONV), F32)], axis=0)
    weights = [
        row(norm1_g[0]), w_in[0].astype(BF16),
        _block_diag2(pool_lin[0, 0], pool_lin[0, 1]).astype(BF16),
        _block_diag2(pool_lin[0, 2], pool_lin[0, 3]).astype(BF16),
        row(pool_scale[0]), conv_w, row(conv_dw_b[0]), row(conv_ln_g[0]), row(conv_ln_b[0]),
        w_out[0].astype(BF16), row(b_out[0]), row(norm2_g[0]),
        router_w.astype(BF16), router_b.reshape(ROUTER_ROWS, 1),
    ]

    zpool = jnp.zeros((bsz, POOL_PAD, D_POOL), F32)
    zconv = jnp.zeros((bsz, CONV_PAD, D_CONV), F32)
    ppool = jnp.pad(state_pool[0], ((0, 0), (POOL_PAD - POOL_STATE, 0), (0, 0)))
    pconv = jnp.pad(state_conv[0], ((0, 0), (CONV_PAD - CONV_STATE, 0), (0, 0)))

    x1_p, hm_p, gates_p, route_p, cnt_p, sp_p, sc_p = _mixer_call(
        x_prompt, zpool, zconv, 0, PROMPT_TILE, weights, "mixer_prompt")
    x1_s, hm_s, gates_s, route_s, cnt_s, sp_s, sc_s = _mixer_call(
        x_sample, ppool, pconv, past_len, dseq, weights, "mixer_sample")

    tm = EXPERT_TILE
    cnt_p = cnt_p[:, 0].astype(I32)
    cnt_s = cnt_s[:, 0].astype(I32)
    padded = (cnt_p + cnt_s + tm - 1) // tm * tm
    pad_end = jnp.cumsum(padded)
    pad_start = pad_end - padded

    def dest_blocks(route, base):
        eid = jnp.stack([route[:, k, :].reshape(-1) for k in range(TOP_K)])
        rank = jnp.stack([route[:, TOP_K + k, :].reshape(-1) for k in range(TOP_K)])
        experts = jnp.arange(N_EXPERTS, dtype=I32)
        dest = rank + jnp.sum(jnp.where(eid[..., None] == experts, base, 0), axis=-1)
        return dest.reshape(TOP_K, -1, TOKEN_TILE).transpose(1, 0, 2)

    dest_p = dest_blocks(route_p, pad_start)
    dest_s = dest_blocks(route_s, pad_start + cnt_p)
    n_tiles = -(-(n_tokens * TOP_K + N_EXPERTS * (tm - 1)) // tm)
    n_used = (pad_end[-1:] // tm).astype(I32)

    xb = jnp.zeros((n_tiles * tm * HM_PIECES, LANES), U32)
    xb = _dispatch_call(dest_p, hm_p, xb, "dispatch_prompt")
    xb = _dispatch_call(dest_s, hm_s, xb, "dispatch_sample")
    owns = padded > 0
    active = jnp.argsort(jnp.logical_not(owns), stable=True).astype(I32)
    tile_expert = jnp.minimum(
        jnp.sum(pad_end[None, :] <= (jnp.arange(n_tiles, dtype=I32) * tm)[:, None], axis=1), N_EXPERTS - 1)
    ordinal = jnp.cumsum(owns.astype(I32)) - 1
    tile_ord = jnp.sum(jnp.where(tile_expert[:, None] == jnp.arange(N_EXPERTS, dtype=I32), ordinal, 0),
                       axis=1).astype(I32)
    n_active = jnp.sum(owns.astype(I32)).reshape(1)
    yb = _expert_call(tile_ord, active, n_used, n_active, xb, w_eg[0], w_eu[0], w_ed[0])
    gf = row(norm_f_g)
    y_p = _combine_call(dest_p, x1_p, gates_p, gf, yb, "combine_prompt")
    y_s = _combine_call(dest_s, x1_s, gates_s, gf, yb, "combine_sample")

    return (y_p.reshape(bsz, seq, D_MODEL), y_s.reshape(dbsz, dseq, D_MODEL),
            sp_p, sc_p, sp_s, sc_s)
```

```python
import functools

import jax
import jax.numpy as jnp
from jax import lax
from jax.experimental import pallas as pl
from jax.experimental.pallas import tpu as pltpu

F32 = jnp.float32
BF16 = jnp.bfloat16
U32 = jnp.uint32
I32 = jnp.int32

D_MODEL = 1024
D_POOL = 512
D_CONV = 512
D_IN = D_POOL + 2 * D_CONV
POOL_WINDOWS = (2, 4, 8, 16)
POOL_GROUP_DIM = D_POOL // len(POOL_WINDOWS)
POOL_STATE = max(POOL_WINDOWS) - 1
CONV_WIDTH = 31
CONV_STATE = CONV_WIDTH - 1
N_EXPERT_GROUPS = 4
EXPERTS_PER_GROUP = 8
N_EXPERTS = N_EXPERT_GROUPS * EXPERTS_PER_GROUP
TOP_K = 2
D_FF = D_MODEL // 2
EPS = 1e-6

SUBLANES = 8
LANES = 128
POOL_PAD = 32
CONV_PAD = 32
ROUTER_ROWS = LANES
EXPERT_ROW0 = SUBLANES
CONV_CHUNK = 64
CONV_STRIDE = 2
CONV_BLOCKS = D_CONV // LANES
MIXER_PART = 256
HM_WORDS = D_MODEL // 2
HM_PIECES = HM_WORDS // LANES

PROMPT_TILE = 512
TOKEN_TILE = 512
EXPERT_TILE = 512
X_SLOTS = 3
COMBINE_CHUNK = 64
VMEM_LIMIT = 48 * 1024 * 1024


def _iota_f32(shape, axis):
    return lax.broadcasted_iota(I32, shape, axis).astype(F32)


def _mixer_kernel(pos0, tt, n_t,
                  x_ref, pp_ref, cp_ref, g1_ref, win_ref, bd01_ref, bd23_ref, pscale_ref,
                  cw_ref, cb_ref, lng_ref, lnb_ref, wout_ref, bout_ref, g2_ref, wrt_ref, br_ref,
                  tri_ref,
                  x1_ref, hm_ref, gates_ref, route_ref, cnt_ref, spool_ref, sconv_ref,
                  pext, cext, hbuf, carry, lv2, lv4, lv8):
    b = pl.program_id(0)
    j = pl.program_id(1)

    @pl.when((b == 0) & (j == 0))
    def _():
        carry[...] = jnp.zeros_like(carry)

    @pl.when(j == 0)
    def _():
        pext[0:POOL_PAD, :] = pp_ref[...]
        for lb in range(CONV_BLOCKS):
            cext[lb, 0:CONV_PAD, :] = cp_ref[:, lb * LANES:(lb + 1) * LANES]

    @pl.when(j > 0)
    def _():
        pext[0:POOL_PAD, :] = pext[tt:tt + POOL_PAD, :]
        cext[:, 0:CONV_PAD, :] = cext[:, tt:tt + CONV_PAD, :]

    n_parts = max(1, tt // MIXER_PART)
    rp = tt // n_parts
    gd = POOL_GROUP_DIM
    neg = jnp.float32(-jnp.inf)

    def stage_in(part):
        r0 = part * rp
        x = x_ref[r0:r0 + rp, :]
        ms = jnp.mean(x * x, axis=-1, keepdims=True)
        h = (x * lax.rsqrt(ms + EPS) * g1_ref[...]).astype(BF16)
        z = jnp.dot(h, win_ref[...], preferred_element_type=F32)
        pext[POOL_PAD + r0:POOL_PAD + r0 + rp, :] = z[:, :D_POOL]
        v = z[:, D_POOL:D_POOL + D_CONV] * jax.nn.sigmoid(z[:, D_POOL + D_CONV:])
        for lb in range(CONV_BLOCKS):
            cext[lb, CONV_PAD + r0:CONV_PAD + r0 + rp, :] = v[:, lb * LANES:(lb + 1) * LANES]

    def stage_mix(part):
        r0 = part * rp
        pos = pos0 + j * tt + r0 + lax.broadcasted_iota(I32, (rp, 1), 0)
        top = r0 + POOL_PAD
        end = top + rp
        lv2[r0 + 8:end, :] = pext[r0 + 8:end, :] + pext[r0 + 7:end - 1, :]
        lv4[r0 + 16:end, gd:] = lv2[r0 + 16:end, gd:] + lv2[r0 + 14:end - 2, gd:]
        lv8[r0 + 24:end, 2 * gd:] = lv4[r0 + 24:end, 2 * gd:] + lv4[r0 + 20:end - 4, 2 * gd:]
        sums = [lv2[top:end, 0:gd], lv4[top:end, gd:2 * gd], lv8[top:end, 2 * gd:3 * gd],
                lv8[top:end, 3 * gd:] + lv8[top - 8:end - 8, 3 * gd:]]
        u = pext[top:end, :]
        parts = []
        for g, w in enumerate(POOL_WINDOWS):
            sl = slice(g * gd, (g + 1) * gd)
            cnt = jnp.minimum(pos + 1, w).astype(F32)
            parts.append(sums[g] / cnt - u[:, sl])
        d = jnp.concatenate(parts, axis=-1).astype(BF16)
        half = D_POOL // 2
        yp = jnp.concatenate(
            [jnp.dot(d[:, :half], bd01_ref[...], preferred_element_type=F32),
             jnp.dot(d[:, half:], bd23_ref[...], preferred_element_type=F32)], axis=-1)
        yp = yp * pscale_ref[...]

        chunk = min(CONV_CHUNK, rp // CONV_STRIDE)
        for lb in range(CONV_BLOCKS):
            lanes = slice(lb * LANES, (lb + 1) * LANES)
            wts = [jnp.broadcast_to(cw_ref[k:k + 1, lanes], (chunk, LANES)) for k in range(CONV_WIDTH)]
            for first in range(r0, r0 + rp, CONV_STRIDE * chunk):
                for phase in range(CONV_STRIDE):
                    acc = None
                    for k in range(CONV_WIDTH):
                        start = first + phase + CONV_PAD - CONV_STATE + k
                        tap = cext[lb, pl.ds(start, chunk, stride=CONV_STRIDE), :] * wts[k]
                        acc = tap if acc is None else acc + tap
                    hbuf[lb, pl.ds(first + phase, chunk, stride=CONV_STRIDE), :] = acc
        return yp

    def stage_out(part, yp):
        r0 = part * rp
        x = x_ref[r0:r0 + rp, :]
        hc = jnp.concatenate([hbuf[lb, r0:r0 + rp, :] for lb in range(CONV_BLOCKS)], axis=-1) + cb_ref[...]
        mu = jnp.mean(hc, axis=-1, keepdims=True)
        var = jnp.mean(jnp.square(hc - mu), axis=-1, keepdims=True)
        yln = (hc - mu) * lax.rsqrt(var + EPS) * lng_ref[...] + lnb_ref[...]
        yc = yln * jax.nn.sigmoid(yln)

        o = (jnp.dot(yp.astype(BF16), wout_ref[0:D_POOL, :], preferred_element_type=F32)
             + jnp.dot(yc.astype(BF16), wout_ref[D_POOL:, :], preferred_element_type=F32))
        x1 = x + (o + bout_ref[...])
        x1_ref[r0:r0 + rp, :] = x1

        ms2 = jnp.mean(x1 * x1, axis=-1, keepdims=True)
        hmb = (x1 * lax.rsqrt(ms2 + EPS) * g2_ref[...]).astype(BF16)
        packed = _pack_bf16_pairs(hmb.astype(F32))
        for p in range(HM_PIECES):
            hm_ref[pl.ds(r0 * HM_PIECES + p, rp, stride=HM_PIECES), :] = packed[:, p * LANES:(p + 1) * LANES]

        lt = lax.dot_general(wrt_ref[...], hmb, (((1,), (1,)), ((), ())),
                             preferred_element_type=F32) + br_ref[...]
        row8 = _iota_f32((SUBLANES, rp), 0)
        lg = jnp.where(row8 < N_EXPERT_GROUPS, lt[0:SUBLANES, :], neg)
        mg = jnp.max(lg, axis=0, keepdims=True)
        p_grp = 1.0 / jnp.sum(jnp.exp(lg - mg), axis=0, keepdims=True)
        gsel = jnp.min(jnp.where(lg == mg, row8, float(SUBLANES)), axis=0, keepdims=True)
        le = jnp.zeros((SUBLANES, rp), F32)
        for g in range(N_EXPERT_GROUPS):
            e0 = EXPERT_ROW0 + g * EXPERTS_PER_GROUP
            le = jnp.where(gsel == float(g), lt[e0:e0 + EXPERTS_PER_GROUP, :], le)
        m1 = jnp.max(le, axis=0, keepdims=True)
        i1 = jnp.min(jnp.where(le == m1, row8, float(SUBLANES)), axis=0, keepdims=True)
        le2 = jnp.where(row8 == i1, neg, le)
        m2 = jnp.max(le2, axis=0, keepdims=True)
        i2 = jnp.min(jnp.where(le2 == m2, row8, float(SUBLANES)), axis=0, keepdims=True)
        e2 = jnp.exp(m2 - m1)
        den = 1.0 + e2
        gate0 = p_grp * (1.0 / den)
        gate1 = p_grp * (e2 / den)
        eid0 = gsel * float(EXPERTS_PER_GROUP) + i1
        eid1 = gsel * float(EXPERTS_PER_GROUP) + i2

        rowe = _iota_f32((N_EXPERTS, rp), 0)
        oh0 = rowe == eid0
        oh1 = rowe == eid1
        ohf = jnp.where(jnp.logical_or(oh0, oh1), 1.0, 0.0)
        before = jnp.dot(ohf.astype(BF16), tri_ref[0:rp, 0:rp], preferred_element_type=F32) + carry[:, 0:1]
        rank0 = jnp.sum(jnp.where(oh0, before, 0.0), axis=0, keepdims=True)
        rank1 = jnp.sum(jnp.where(oh1, before, 0.0), axis=0, keepdims=True)
        carry[...] = carry[...] + jnp.sum(ohf, axis=1, keepdims=True)

        route = jnp.where(row8 == 0.0, eid0,
                          jnp.where(row8 == 1.0, eid1,
                                    jnp.where(row8 == 2.0, rank0,
                                              jnp.where(row8 == 3.0, rank1, 0.0))))
        route_ref[:, r0:r0 + rp] = route.astype(I32)

        tpad = -(-rp // LANES) * LANES
        rowr = _iota_f32((ROUTER_ROWS, rp), 0)
        gt = jnp.where(rowr == 0.0, gate0, jnp.where(rowr == 1.0, gate1, 0.0))
        if tpad != rp:
            gt = jnp.concatenate([gt, jnp.zeros((ROUTER_ROWS, tpad - rp), F32)], axis=1)
        gates_ref[r0:r0 + rp, :] = gt.T[0:rp, :]

    stage_in(0)
    for part in range(n_parts):
        yp = stage_mix(part)
        if part + 1 < n_parts:
            stage_in(part + 1)
        stage_out(part, yp)
    cnt_ref[...] = carry[...]

    @pl.when(j == n_t - 1)
    def _():
        spool_ref[...] = pext[POOL_PAD + tt - POOL_STATE:POOL_PAD + tt, :]
        sconv_ref[...] = jnp.concatenate(
            [cext[lb, CONV_PAD + tt - CONV_STATE:CONV_PAD + tt, :] for lb in range(CONV_BLOCKS)], axis=-1)


def _mixer_call(x, pool_prefix, conv_prefix, pos0, tt, weights, name):
    bsz, seq, _ = x.shape
    n_t = seq // tt
    n_blk = bsz * n_t
    n_rows = bsz * seq
    tri = (lax.broadcasted_iota(I32, (tt, tt), 0) < lax.broadcasted_iota(I32, (tt, tt), 1)).astype(BF16)

    def const(shape):
        return pl.BlockSpec(shape, lambda b, j: (0,) * len(shape))

    def rows(width):
        return pl.BlockSpec((tt, width), lambda b, j: (b * n_t + j, 0))

    in_specs = [
        pl.BlockSpec((None, tt, D_MODEL), lambda b, j: (b, j, 0)),
        pl.BlockSpec((None, POOL_PAD, D_POOL), lambda b, j: (b, 0, 0)),
        pl.BlockSpec((None, CONV_PAD, D_CONV), lambda b, j: (b, 0, 0)),
    ] + [const(w.shape) for w in weights] + [const((tt, tt))]
    out_shape = (
        jax.ShapeDtypeStruct((n_rows, D_MODEL), F32),
        jax.ShapeDtypeStruct((n_rows * HM_PIECES, LANES), U32),
        jax.ShapeDtypeStruct((n_rows, LANES), F32),
        jax.ShapeDtypeStruct((n_blk, SUBLANES, tt), I32),
        jax.ShapeDtypeStruct((N_EXPERTS, LANES), F32),
        jax.ShapeDtypeStruct((1, bsz, POOL_STATE, D_POOL), F32),
        jax.ShapeDtypeStruct((1, bsz, CONV_STATE, D_CONV), F32),
    )
    out_specs = (
        rows(D_MODEL),
        pl.BlockSpec((tt * HM_PIECES, LANES), lambda b, j: (b * n_t + j, 0)),
        rows(LANES),
        pl.BlockSpec((None, SUBLANES, tt), lambda b, j: (b * n_t + j, 0, 0)),
        pl.BlockSpec((N_EXPERTS, LANES), lambda b, j: (0, 0)),
        pl.BlockSpec((None, None, POOL_STATE, D_POOL), lambda b, j: (0, b, 0, 0)),
        pl.BlockSpec((None, None, CONV_STATE, D_CONV), lambda b, j: (0, b, 0, 0)),
    )
    return pl.pallas_call(
        functools.partial(_mixer_kernel, pos0, tt, n_t),
        out_shape=out_shape,
        grid=(bsz, n_t),
        in_specs=in_specs,
        out_specs=out_specs,
        scratch_shapes=[
            pltpu.VMEM((POOL_PAD + tt, D_POOL), F32),
            pltpu.VMEM((CONV_BLOCKS, CONV_PAD + tt, LANES), F32),
            pltpu.VMEM((CONV_BLOCKS, tt, LANES), F32),
            pltpu.VMEM((N_EXPERTS, LANES), F32),
            pltpu.VMEM((POOL_PAD + tt, D_POOL), F32),
            pltpu.VMEM((POOL_PAD + tt, D_POOL), F32),
            pltpu.VMEM((POOL_PAD + tt, D_POOL), F32),
        ],
        compiler_params=pltpu.CompilerParams(
            dimension_semantics=("arbitrary", "arbitrary"), vmem_limit_bytes=VMEM_LIMIT),
        name=name,
    )(x, pool_prefix, conv_prefix, *weights, tri)


def _dispatch_kernel(n_sample_tiles, last_ref, owns_ref, used_ref, dest_ref, hmp_ref, hms_ref, xb_ref,
                     zbuf, sem, zsem):
    i = pl.program_id(0)
    tc = dest_ref.shape[2]
    n_prompt_tiles = pl.num_programs(0) - n_sample_tiles
    tile_rows = zbuf.shape[0]
    n_all = xb_ref.shape[0] // tile_rows

    def zero_copy(t):
        dst = xb_ref.at[pl.ds(pl.multiple_of(t * tile_rows, tile_rows), tile_rows), :]
        return pltpu.make_async_copy(zbuf, dst, zsem)

    @pl.when(i == 0)
    def _():
        zbuf[...] = jnp.zeros_like(zbuf)
        for e in range(N_EXPERTS):
            @pl.when(owns_ref[e] > 0)
            def _(e=e):
                zero_copy(last_ref[e]).start()
        lax.fori_loop(used_ref[0], n_all, lambda t, c: (zero_copy(t).start(), c)[1], 0)
        for e in range(N_EXPERTS):
            @pl.when(owns_ref[e] > 0)
            def _(e=e):
                zero_copy(last_ref[e]).wait()
        lax.fori_loop(used_ref[0], n_all, lambda t, c: (zero_copy(t).wait(), c)[1], 0)

    def scatter(hm_ref):
        for t in range(tc):
            for k in range(TOP_K):
                d = dest_ref[0, k, t]
                src = hm_ref.at[pl.ds(t * HM_PIECES, HM_PIECES), :]
                dst = xb_ref.at[pl.ds(pl.multiple_of(d * HM_PIECES, HM_PIECES), HM_PIECES), :]
                pltpu.make_async_copy(src, dst, sem).start(priority=k)
        for k in range(TOP_K):
            pltpu.make_async_copy(hm_ref, xb_ref.at[pl.ds(0, tc * HM_PIECES), :], sem).wait()

    @pl.when(i < n_prompt_tiles)
    def _():
        scatter(hmp_ref)

    @pl.when(i >= n_prompt_tiles)
    def _():
        scatter(hms_ref)


def _dispatch_call(last_tile, owns, n_used, dest_blocks, hm_p, hm_s, n_slots):
    tc = TOKEN_TILE
    n_p = hm_p.shape[0] // (tc * HM_PIECES)
    n_s = hm_s.shape[0] // (tc * HM_PIECES)
    grid_spec = pltpu.PrefetchScalarGridSpec(
        num_scalar_prefetch=3,
        grid=(n_p + n_s,),
        in_specs=[
            pl.BlockSpec((1, TOP_K, tc), lambda i, *_: (i, 0, 0), memory_space=pltpu.SMEM),
            pl.BlockSpec((tc * HM_PIECES, LANES), lambda i, *_: (jnp.minimum(i, n_p - 1), 0)),
            pl.BlockSpec((tc * HM_PIECES, LANES), lambda i, *_: (jnp.maximum(i - n_p, 0), 0)),
        ],
        out_specs=pl.BlockSpec(memory_space=pl.ANY),
        scratch_shapes=[
            pltpu.VMEM((EXPERT_TILE * HM_PIECES, LANES), U32),
            pltpu.SemaphoreType.DMA(()),
            pltpu.SemaphoreType.DMA(()),
        ],
    )
    return pl.pallas_call(
        functools.partial(_dispatch_kernel, n_s),
        out_shape=jax.ShapeDtypeStruct((n_slots * HM_PIECES, LANES), U32),
        grid_spec=grid_spec,
        compiler_params=pltpu.CompilerParams(dimension_semantics=("arbitrary",)),
        name="dispatch",
    )(last_tile, owns, n_used, dest_blocks, hm_p, hm_s)


def _pack_bf16_pairs(x):
    w = x.shape[1] // 2
    rounded = x.astype(BF16).astype(F32)
    lo = lax.bitcast_convert_type(rounded[:, :w], U32)
    hi = lax.bitcast_convert_type(rounded[:, w:], U32)
    return (hi & jnp.uint32(0xFFFF0000)) | lax.shift_right_logical(lo, jnp.uint32(16))


def _unpack_bf16_pairs(words):
    lo = lax.bitcast_convert_type(lax.shift_left(words, jnp.uint32(16)), F32)
    hi = lax.bitcast_convert_type(words & jnp.uint32(0xFFFF0000), F32)
    return lo, hi


def _expert_kernel(ord_ref, act_ref, used_ref, nact_ref, wg_ref, wu_ref, wd_ref, xb_ref, yb_ref,
                   wgf, wuf, wdf, wgb, wub, wdb, xbuf, ybuf, sem_w, sem_in, sem_out):
    tile_rows = xbuf.shape[1]
    tm = tile_rows // HM_PIECES
    n = used_ref[0]
    n_act = nact_ref[0]

    def weight_copies(k, slot):
        e = act_ref[k]
        return [pltpu.make_async_copy(src.at[e], dst.at[slot], sem_w.at[slot])
                for src, dst in ((wg_ref, wgf), (wu_ref, wuf), (wd_ref, wdf))]

    def tile(ref, t):
        return ref.at[pl.ds(pl.multiple_of(t * tile_rows, tile_rows), tile_rows), :]

    def fetch(t, slot):
        return pltpu.make_async_copy(tile(xb_ref, t), xbuf.at[slot], sem_in.at[slot])

    def put(t, slot):
        return pltpu.make_async_copy(ybuf.at[slot], tile(yb_ref, t), sem_out.at[slot])

    for cp in weight_copies(0, 0):
        cp.start()
    for ahead in range(X_SLOTS - 1):
        @pl.when(ahead < n)
        def _(ahead=ahead):
            fetch(ahead, ahead).start()

    def body(t, carry):
        k = ord_ref[t]

        @pl.when((t == 0) | (k != ord_ref[jnp.maximum(t - 1, 0)]))
        def _():
            wslot = lax.rem(k, 2)
            for cp in weight_copies(k, wslot):
                cp.wait()
            wgb[...] = wgf[wslot].astype(BF16)
            wub[...] = wuf[wslot].astype(BF16)
            wdb[...] = wdf[wslot].astype(BF16)

            @pl.when(k + 1 < n_act)
            def _():
                for cp in weight_copies(k + 1, 1 - wslot):
                    cp.start()

        slot = lax.rem(t, X_SLOTS)
        fetch(t, slot).wait()

        @pl.when(t + X_SLOTS - 1 < n)
        def _():
            fetch(t + X_SLOTS - 1, lax.rem(t + X_SLOTS - 1, X_SLOTS)).start()

        oslot = lax.rem(t, 2)

        @pl.when(t >= 2)
        def _():
            put(t - 2, oslot).wait()

        pieces = [_unpack_bf16_pairs(xbuf[slot, pl.ds(p, tm, stride=HM_PIECES), :]) for p in range(HM_PIECES)]
        xt = jnp.concatenate([lo.astype(BF16) for lo, _ in pieces] + [hi.astype(BF16) for _, hi in pieces],
                             axis=-1)
        g = jnp.dot(xt, wgb[...], preferred_element_type=F32)
        up = jnp.dot(xt, wub[...], preferred_element_type=F32)
        a = (g * jax.nn.sigmoid(g) * up).astype(BF16)
        packed = _pack_bf16_pairs(jnp.dot(a, wdb[...], preferred_element_type=F32))
        for p in range(HM_PIECES):
            ybuf[oslot, pl.ds(p, tm, stride=HM_PIECES), :] = packed[:, p * LANES:(p + 1) * LANES]
        put(t, oslot).start()
        return carry

    lax.fori_loop(0, n, body, 0)

    @pl.when(n >= 2)
    def _():
        put(n - 2, lax.rem(n, 2)).wait()

    @pl.when(n >= 1)
    def _():
        put(n - 1, lax.rem(n - 1, 2)).wait()

    ybuf[0] = jnp.zeros(ybuf.shape[1:], ybuf.dtype)
    n_all = yb_ref.shape[0] // tile_rows

    def zero_tile(t, carry):
        cp = pltpu.make_async_copy(ybuf.at[0], tile(yb_ref, t), sem_out.at[0])
        cp.start()
        cp.wait()
        return carry

    lax.fori_loop(n, n_all, zero_tile, 0)


def _expert_call(tile_ord, active, n_used, n_active, xb, w_eg, w_eu, w_ed):
    tile_rows = EXPERT_TILE * HM_PIECES
    any_spec = pl.BlockSpec(memory_space=pl.ANY)
    grid_spec = pltpu.PrefetchScalarGridSpec(
        num_scalar_prefetch=4,
        grid=(1,),
        in_specs=[any_spec] * 4,
        out_specs=any_spec,
        scratch_shapes=[
            pltpu.VMEM((2, D_MODEL, D_FF), F32),
            pltpu.VMEM((2, D_MODEL, D_FF), F32),
            pltpu.VMEM((2, D_FF, D_MODEL), F32),
            pltpu.VMEM((D_MODEL, D_FF), BF16),
            pltpu.VMEM((D_MODEL, D_FF), BF16),
            pltpu.VMEM((D_FF, D_MODEL), BF16),
            pltpu.VMEM((X_SLOTS, tile_rows, LANES), U32),
            pltpu.VMEM((2, tile_rows, LANES), U32),
            pltpu.SemaphoreType.DMA((2,)),
            pltpu.SemaphoreType.DMA((X_SLOTS,)),
            pltpu.SemaphoreType.DMA((2,)),
        ],
    )
    return pl.pallas_call(
        _expert_kernel,
        out_shape=jax.ShapeDtypeStruct(xb.shape, U32),
        grid_spec=grid_spec,
        compiler_params=pltpu.CompilerParams(
            dimension_semantics=("arbitrary",), vmem_limit_bytes=VMEM_LIMIT),
        name="experts",
    )(tile_ord, active, n_used, n_active, w_eg, w_eu, w_ed, xb)


def _gather_rows(dest_ref, yb_ref, buf_ref, sem, tokens):
    for t in tokens:
        for k in range(TOP_K):
            d = dest_ref[0, k, t]
            src = yb_ref.at[pl.ds(pl.multiple_of(d * HM_PIECES, HM_PIECES), HM_PIECES), :]
            dst = buf_ref.at[k, pl.ds(t * HM_PIECES, HM_PIECES), :]
            pltpu.make_async_copy(src, dst, sem).start(priority=k)


def _wait_rows(yb_ref, buf_ref, sem):
    for k in range(TOP_K):
        pltpu.make_async_copy(yb_ref.at[pl.ds(0, buf_ref.shape[1]), :], buf_ref.at[k], sem).wait()


def _combine_kernel(dest0_ref, dest1_ref, dest2_ref, x1_ref, gates_ref, gf_ref, yb_ref, out_ref, buf, sems):
    i = pl.program_id(0)
    tc = TOKEN_TILE
    n_chunks = tc // COMBINE_CHUNK

    @pl.when(i == 0)
    def _():
        _gather_rows(dest0_ref, yb_ref, buf.at[0], sems.at[0], range(tc))

    gf = gf_ref[...]
    for half, next_dest in ((0, dest1_ref), (1, dest2_ref)):
        cur, nxt = buf.at[half], buf.at[1 - half]
        _wait_rows(yb_ref, cur, sems.at[half])
        for c in range(n_chunks):
            rows = pl.ds(half * tc + c * COMBINE_CHUNK, COMBINE_CHUNK)
            gates = gates_ref[rows, :]
            lows, highs = [], []
            for p in range(HM_PIECES):
                piece = pl.ds(c * COMBINE_CHUNK * HM_PIECES + p, COMBINE_CHUNK, stride=HM_PIECES)
                lo0, hi0 = _unpack_bf16_pairs(cur[0, piece, :])
                lo1, hi1 = _unpack_bf16_pairs(cur[1, piece, :])
                lows.append(lo0 * gates[:, 0:1] + lo1 * gates[:, 1:2])
                highs.append(hi0 * gates[:, 0:1] + hi1 * gates[:, 1:2])
            xo = x1_ref[rows, :] + jnp.concatenate(lows + highs, axis=-1)
            _gather_rows(next_dest, yb_ref, nxt, sems.at[1 - half],
                         range(c * COMBINE_CHUNK, (c + 1) * COMBINE_CHUNK))
            ms = jnp.mean(xo * xo, axis=-1, keepdims=True)
            out_ref[rows, :] = xo * lax.rsqrt(ms + EPS) * gf

    @pl.when(i == pl.num_programs(0) - 1)
    def _():
        _wait_rows(yb_ref, buf.at[0], sems.at[0])


def _combine_call(dest_blocks, x1, gates, gf, yb, name):
    n_rows = x1.shape[0]
    tc = TOKEN_TILE
    n_tiles = n_rows // tc
    assert n_tiles % 2 == 0

    def dest_spec(index_map):
        return pl.BlockSpec((1, TOP_K, tc), index_map, memory_space=pltpu.SMEM)

    return pl.pallas_call(
        _combine_kernel,
        out_shape=jax.ShapeDtypeStruct((n_rows, D_MODEL), F32),
        grid=(n_tiles // 2,),
        in_specs=[
            dest_spec(lambda i: (0, 0, 0)),
            dest_spec(lambda i: (2 * i + 1, 0, 0)),
            dest_spec(lambda i: (jnp.minimum(2 * i + 2, n_tiles - 1), 0, 0)),
            pl.BlockSpec((2 * tc, D_MODEL), lambda i: (i, 0)),
            pl.BlockSpec((2 * tc, LANES), lambda i: (i, 0)),
            pl.BlockSpec((1, D_MODEL), lambda i: (0, 0)),
            pl.BlockSpec(memory_space=pl.ANY),
        ],
        out_specs=pl.BlockSpec((2 * tc, D_MODEL), lambda i: (i, 0)),
        scratch_shapes=[pltpu.VMEM((2, TOP_K, tc * HM_PIECES, LANES), U32), pltpu.SemaphoreType.DMA((2,))],
        compiler_params=pltpu.CompilerParams(
            dimension_semantics=("arbitrary",), vmem_limit_bytes=VMEM_LIMIT),
        name=name,
    )(dest_blocks, dest_blocks, dest_blocks, x1, gates, gf, yb)


def _block_diag2(a, b):
    z = jnp.zeros_like(a)
    return jnp.concatenate([jnp.concatenate([a, z], axis=1), jnp.concatenate([z, b], axis=1)], axis=0)


def kernel(x_prompt, x_sample, state_pool, state_conv, norm1_g, w_in, pool_lin, pool_scale, conv_dw, conv_dw_b, conv_ln_g, conv_ln_b, w_out, b_out, norm2_g, w_rg, b_rg, w_re, b_re, w_eg, w_eu, w_ed, norm_f_g):
    assert norm1_g.shape[0] == 1, "single-layer trunk"
    bsz, seq, _ = x_prompt.shape
    dbsz, dseq, _ = x_sample.shape
    past_len = 1024
    n_prompt = bsz * seq
    n_sample = dbsz * dseq
    n_tokens = n_prompt + n_sample
    assert seq % PROMPT_TILE == 0 and n_prompt % TOKEN_TILE == 0 and n_sample % TOKEN_TILE == 0

    row = lambda a: a.reshape(1, -1)
    router_w = jnp.zeros((ROUTER_ROWS, D_MODEL), F32)
    router_w = router_w.at[0:N_EXPERT_GROUPS].set(w_rg[0].T)
    router_w = router_w.at[EXPERT_ROW0:EXPERT_ROW0 + N_EXPERTS].set(
        jnp.transpose(w_re[0], (0, 2, 1)).reshape(N_EXPERTS, D_MODEL))
    router_b = jnp.zeros((ROUTER_ROWS,), F32)
    router_b = router_b.at[0:N_EXPERT_GROUPS].set(b_rg[0])
    router_b = router_b.at[EXPERT_ROW0:EXPERT_ROW0 + N_EXPERTS].set(b_re[0].reshape(-1))
    conv_w = jnp.concatenate([conv_dw[0], jnp.zeros((1, D_CONV), F32)], axis=0)
    weights = [
        row(norm1_g[0]), w_in[0].astype(BF16),
        _block_diag2(pool_lin[0, 0], pool_lin[0, 1]).astype(BF16),
        _block_diag2(pool_lin[0, 2], pool_lin[0, 3]).astype(BF16),
        row(pool_scale[0]), conv_w, row(conv_dw_b[0]), row(conv_ln_g[0]), row(conv_ln_b[0]),
        w_out[0].astype(BF16), row(b_out[0]), row(norm2_g[0]),
        router_w.astype(BF16), router_b.reshape(ROUTER_ROWS, 1),
    ]

    zpool = jnp.zeros((bsz, POOL_PAD, D_POOL), F32)
    zconv = jnp.zeros((bsz, CONV_PAD, D_CONV), F32)
    ppool = jnp.pad(state_pool[0], ((0, 0), (POOL_PAD - POOL_STATE, 0), (0, 0)))
    pconv = jnp.pad(state_conv[0], ((0, 0), (CONV_PAD - CONV_STATE, 0), (0, 0)))

    x1_p, hm_p, gates_p, route_p, cnt_p, sp_p, sc_p = _mixer_call(
        x_prompt, zpool, zconv, 0, PROMPT_TILE, weights, "mixer_prompt")
    x1_s, hm_s, gates_s, route_s, cnt_s, sp_s, sc_s = _mixer_call(
        x_sample, ppool, pconv, past_len, dseq, weights, "mixer_sample")

    tm = EXPERT_TILE
    cnt_p = cnt_p[:, 0].astype(I32)
    cnt_s = cnt_s[:, 0].astype(I32)
    padded = (cnt_p + cnt_s + tm - 1) // tm * tm
    pad_end = jnp.cumsum(padded)
    pad_start = pad_end - padded

    def dest_blocks(route, base):
        eid = jnp.stack([route[:, k, :].reshape(-1) for k in range(TOP_K)])
        rank = jnp.stack([route[:, TOP_K + k, :].reshape(-1) for k in range(TOP_K)])
        experts = jnp.arange(N_EXPERTS, dtype=I32)
        dest = rank + jnp.sum(jnp.where(eid[..., None] == experts, base, 0), axis=-1)
        return dest.reshape(TOP_K, -1, TOKEN_TILE).transpose(1, 0, 2)

    dest_p = dest_blocks(route_p, pad_start)
    dest_s = dest_blocks(route_s, pad_start + cnt_p)
    n_tiles = -(-(n_tokens * TOP_K + N_EXPERTS * (tm - 1)) // tm)
    n_used = (pad_end[-1:] // tm).astype(I32)

    owns = padded > 0
    xb = _dispatch_call((pad_end // tm - 1).astype(I32), owns.astype(I32), n_used,
                        jnp.concatenate([dest_p, dest_s], axis=0), hm_p, hm_s, n_tiles * tm)
    active = jnp.argsort(jnp.logical_not(owns), stable=True).astype(I32)
    tile_expert = jnp.minimum(
        jnp.sum(pad_end[None, :] <= (jnp.arange(n_tiles, dtype=I32) * tm)[:, None], axis=1), N_EXPERTS - 1)
    ordinal = jnp.cumsum(owns.astype(I32)) - 1
    tile_ord = jnp.sum(jnp.where(tile_expert[:, None] == jnp.arange(N_EXPERTS, dtype=I32), ordinal, 0),
                       axis=1).astype(I32)
    n_active = jnp.sum(owns.astype(I32)).reshape(1)
    yb = _expert_call(tile_ord, active, n_used, n_active, xb, w_eg[0], w_eu[0], w_ed[0])
    gf = row(norm_f_g)
    y_p = _combine_call(dest_p, x1_p, gates_p, gf, yb, "combine_prompt")
    y_s = _combine_call(dest_s, x1_s, gates_s, gf, yb, "combine_sample")

    return (y_p.reshape(bsz, seq, D_MODEL), y_s.reshape(dbsz, dseq, D_MODEL),
            sp_p, sc_p, sp_s, sc_s)
```

```python
import functools

import jax
import jax.numpy as jnp
from jax import lax
from jax.experimental import pallas as pl
from jax.experimental.pallas import tpu as pltpu

F32 = jnp.float32
BF16 = jnp.bfloat16
U32 = jnp.uint32
I32 = jnp.int32

D_MODEL = 1024
D_POOL = 512
D_CONV = 512
D_IN = D_POOL + 2 * D_CONV
POOL_WINDOWS = (2, 4, 8, 16)
POOL_GROUP_DIM = D_POOL // len(POOL_WINDOWS)
POOL_STATE = max(POOL_WINDOWS) - 1
CONV_WIDTH = 31
CONV_STATE = CONV_WIDTH - 1
N_EXPERT_GROUPS = 4
EXPERTS_PER_GROUP = 8
N_EXPERTS = N_EXPERT_GROUPS * EXPERTS_PER_GROUP
TOP_K = 2
D_FF = D_MODEL // 2
EPS = 1e-6

SUBLANES = 8
LANES = 128
POOL_PAD = 32
CONV_PAD = 32
ROUTER_ROWS = LANES
EXPERT_ROW0 = SUBLANES
CONV_CHUNK = 64
CONV_STRIDE = 2
CONV_BLOCKS = D_CONV // LANES
MIXER_PART = 256
HM_WORDS = D_MODEL // 2
HM_PIECES = HM_WORDS // LANES

PROMPT_TILE = 512
TOKEN_TILE = 512
DISPATCH_TILE = 1024
EXPERT_TILE = 512
X_SLOTS = 4
COMBINE_CHUNK = 64
V7X_VMEM_BYTES = 64 * 1024 * 1024
VMEM_LIMIT = V7X_VMEM_BYTES * 3 // 4


def _iota_f32(shape, axis):
    return lax.broadcasted_iota(I32, shape, axis).astype(F32)


def _mixer_kernel(pos0, tt, n_t,
                  x_ref, pp_ref, cp_ref, g1_ref, win_ref, bd01_ref, bd23_ref, pscale_ref,
                  cw_ref, cb_ref, lng_ref, lnb_ref, wout_ref, bout_ref, g2_ref, wrt_ref, br_ref,
                  tri_ref,
                  x1_ref, hm_ref, gates_ref, route_ref, cnt_ref, spool_ref, sconv_ref,
                  pext, cext, hbuf, carry, lv2, lv4, lv8):
    b = pl.program_id(0)
    j = pl.program_id(1)

    @pl.when((b == 0) & (j == 0))
    def _():
        carry[...] = jnp.zeros_like(carry)

    @pl.when(j == 0)
    def _():
        pext[0:POOL_PAD, :] = pp_ref[...]
        for lb in range(CONV_BLOCKS):
            cext[lb, 0:CONV_PAD, :] = cp_ref[:, lb * LANES:(lb + 1) * LANES]

    @pl.when(j > 0)
    def _():
        pext[0:POOL_PAD, :] = pext[tt:tt + POOL_PAD, :]
        cext[:, 0:CONV_PAD, :] = cext[:, tt:tt + CONV_PAD, :]

    n_parts = max(1, tt // MIXER_PART)
    rp = tt // n_parts
    gd = POOL_GROUP_DIM
    neg = jnp.float32(-jnp.inf)

    def stage_in(part):
        r0 = part * rp
        x = x_ref[r0:r0 + rp, :]
        ms = jnp.mean(x * x, axis=-1, keepdims=True)
        h = (x * lax.rsqrt(ms + EPS) * g1_ref[...]).astype(BF16)
        z = jnp.dot(h, win_ref[...], preferred_element_type=F32)
        pext[POOL_PAD + r0:POOL_PAD + r0 + rp, :] = z[:, :D_POOL]
        v = z[:, D_POOL:D_POOL + D_CONV] * jax.nn.sigmoid(z[:, D_POOL + D_CONV:])
        for lb in range(CONV_BLOCKS):
            cext[lb, CONV_PAD + r0:CONV_PAD + r0 + rp, :] = v[:, lb * LANES:(lb + 1) * LANES]

    def stage_mix(part):
        r0 = part * rp
        pos = pos0 + j * tt + r0 + lax.broadcasted_iota(I32, (rp, 1), 0)
        top = r0 + POOL_PAD
        end = top + rp
        lv2[r0 + 8:end, :] = pext[r0 + 8:end, :] + pext[r0 + 7:end - 1, :]
        lv4[r0 + 16:end, gd:] = lv2[r0 + 16:end, gd:] + lv2[r0 + 14:end - 2, gd:]
        lv8[r0 + 24:end, 2 * gd:] = lv4[r0 + 24:end, 2 * gd:] + lv4[r0 + 20:end - 4, 2 * gd:]
        sums = [lv2[top:end, 0:gd], lv4[top:end, gd:2 * gd], lv8[top:end, 2 * gd:3 * gd],
                lv8[top:end, 3 * gd:] + lv8[top - 8:end - 8, 3 * gd:]]
        u = pext[top:end, :]
        parts = []
        for g, w in enumerate(POOL_WINDOWS):
            sl = slice(g * gd, (g + 1) * gd)
            cnt = jnp.minimum(pos + 1, w).astype(F32)
            parts.append(sums[g] / cnt - u[:, sl])
        d = jnp.concatenate(parts, axis=-1).astype(BF16)
        half = D_POOL // 2
        yp = jnp.concatenate(
            [jnp.dot(d[:, :half], bd01_ref[...], preferred_element_type=F32),
             jnp.dot(d[:, half:], bd23_ref[...], preferred_element_type=F32)], axis=-1)
        yp = yp * pscale_ref[...]

        chunk = min(CONV_CHUNK, rp // CONV_STRIDE)
        for lb in range(CONV_BLOCKS):
            lanes = slice(lb * LANES, (lb + 1) * LANES)
            wts = [jnp.broadcast_to(cw_ref[k:k + 1, lanes], (chunk, LANES)) for k in range(CONV_WIDTH)]
            for first in range(r0, r0 + rp, CONV_STRIDE * chunk):
                for phase in range(CONV_STRIDE):
                    acc = None
                    for k in range(CONV_WIDTH):
                        start = first + phase + CONV_PAD - CONV_STATE + k
                        tap = cext[lb, pl.ds(start, chunk, stride=CONV_STRIDE), :] * wts[k]
                        acc = tap if acc is None else acc + tap
                    hbuf[lb, pl.ds(first + phase, chunk, stride=CONV_STRIDE), :] = acc
        return yp

    def stage_out(part, yp):
        r0 = part * rp
        x = x_ref[r0:r0 + rp, :]
        hc = jnp.concatenate([hbuf[lb, r0:r0 + rp, :] for lb in range(CONV_BLOCKS)], axis=-1) + cb_ref[...]
        mu = jnp.mean(hc, axis=-1, keepdims=True)
        var = jnp.mean(jnp.square(hc - mu), axis=-1, keepdims=True)
        yln = (hc - mu) * lax.rsqrt(var + EPS) * lng_ref[...] + lnb_ref[...]
        yc = yln * jax.nn.sigmoid(yln)

        o = (jnp.dot(yp.astype(BF16), wout_ref[0:D_POOL, :], preferred_element_type=F32)
             + jnp.dot(yc.astype(BF16), wout_ref[D_POOL:, :], preferred_element_type=F32))
        x1 = x + (o + bout_ref[...])
        x1_ref[r0:r0 + rp, :] = x1

        ms2 = jnp.mean(x1 * x1, axis=-1, keepdims=True)
        hmb = (x1 * lax.rsqrt(ms2 + EPS) * g2_ref[...]).astype(BF16)
        packed = _pack_bf16_pairs(hmb.astype(F32))
        for p in range(HM_PIECES):
            hm_ref[pl.ds(r0 * HM_PIECES + p, rp, stride=HM_PIECES), :] = packed[:, p * LANES:(p + 1) * LANES]

        lt = lax.dot_general(wrt_ref[...], hmb, (((1,), (1,)), ((), ())),
                             preferred_element_type=F32) + br_ref[...]
        row8 = _iota_f32((SUBLANES, rp), 0)
        lg = jnp.where(row8 < N_EXPERT_GROUPS, lt[0:SUBLANES, :], neg)
        mg = jnp.max(lg, axis=0, keepdims=True)
        p_grp = 1.0 / jnp.sum(jnp.exp(lg - mg), axis=0, keepdims=True)
        gsel = jnp.min(jnp.where(lg == mg, row8, float(SUBLANES)), axis=0, keepdims=True)
        le = jnp.zeros((SUBLANES, rp), F32)
        for g in range(N_EXPERT_GROUPS):
            e0 = EXPERT_ROW0 + g * EXPERTS_PER_GROUP
            le = jnp.where(gsel == float(g), lt[e0:e0 + EXPERTS_PER_GROUP, :], le)
        m1 = jnp.max(le, axis=0, keepdims=True)
        i1 = jnp.min(jnp.where(le == m1, row8, float(SUBLANES)), axis=0, keepdims=True)
        le2 = jnp.where(row8 == i1, neg, le)
        m2 = jnp.max(le2, axis=0, keepdims=True)
        i2 = jnp.min(jnp.where(le2 == m2, row8, float(SUBLANES)), axis=0, keepdims=True)
        e2 = jnp.exp(m2 - m1)
        den = 1.0 + e2
        gate0 = p_grp * (1.0 / den)
        gate1 = p_grp * (e2 / den)
        eid0 = gsel * float(EXPERTS_PER_GROUP) + i1
        eid1 = gsel * float(EXPERTS_PER_GROUP) + i2

        rowe = _iota_f32((N_EXPERTS, rp), 0)
        oh0 = rowe == eid0
        oh1 = rowe == eid1
        ohf = jnp.where(jnp.logical_or(oh0, oh1), 1.0, 0.0)
        before = jnp.dot(ohf.astype(BF16), tri_ref[0:rp, 0:rp], preferred_element_type=F32) + carry[:, 0:1]
        rank0 = jnp.sum(jnp.where(oh0, before, 0.0), axis=0, keepdims=True)
        rank1 = jnp.sum(jnp.where(oh1, before, 0.0), axis=0, keepdims=True)
        carry[...] = carry[...] + jnp.sum(ohf, axis=1, keepdims=True)

        route = jnp.where(row8 == 0.0, eid0,
                          jnp.where(row8 == 1.0, eid1,
                                    jnp.where(row8 == 2.0, rank0,
                                              jnp.where(row8 == 3.0, rank1, 0.0))))
        route_ref[:, r0:r0 + rp] = route.astype(I32)

        tpad = -(-rp // LANES) * LANES
        rowr = _iota_f32((ROUTER_ROWS, rp), 0)
        gt = jnp.where(rowr == 0.0, gate0, jnp.where(rowr == 1.0, gate1, 0.0))
        if tpad != rp:
            gt = jnp.concatenate([gt, jnp.zeros((ROUTER_ROWS, tpad - rp), F32)], axis=1)
        gates_ref[r0:r0 + rp, :] = gt.T[0:rp, :]

    stage_in(0)
    for part in range(n_parts):
        yp = stage_mix(part)
        if part + 1 < n_parts:
            stage_in(part + 1)
        stage_out(part, yp)
    cnt_ref[...] = carry[...]

    @pl.when(j == n_t - 1)
    def _():
        spool_ref[...] = pext[POOL_PAD + tt - POOL_STATE:POOL_PAD + tt, :]
        sconv_ref[...] = jnp.concatenate(
            [cext[lb, CONV_PAD + tt - CONV_STATE:CONV_PAD + tt, :] for lb in range(CONV_BLOCKS)], axis=-1)


def _mixer_call(x, pool_prefix, conv_prefix, pos0, tt, weights, name):
    bsz, seq, _ = x.shape
    n_t = seq // tt
    n_blk = bsz * n_t
    n_rows = bsz * seq
    tri = (lax.broadcasted_iota(I32, (tt, tt), 0) < lax.broadcasted_iota(I32, (tt, tt), 1)).astype(BF16)

    def const(shape):
        return pl.BlockSpec(shape, lambda b, j: (0,) * len(shape))

    def rows(width):
        return pl.BlockSpec((tt, width), lambda b, j: (b * n_t + j, 0))

    in_specs = [
        pl.BlockSpec((None, tt, D_MODEL), lambda b, j: (b, j, 0)),
        pl.BlockSpec((None, POOL_PAD, D_POOL), lambda b, j: (b, 0, 0)),
        pl.BlockSpec((None, CONV_PAD, D_CONV), lambda b, j: (b, 0, 0)),
    ] + [const(w.shape) for w in weights] + [const((tt, tt))]
    out_shape = (
        jax.ShapeDtypeStruct((n_rows, D_MODEL), F32),
        jax.ShapeDtypeStruct((n_rows * HM_PIECES, LANES), U32),
        jax.ShapeDtypeStruct((n_rows, LANES), F32),
        jax.ShapeDtypeStruct((n_blk, SUBLANES, tt), I32),
        jax.ShapeDtypeStruct((N_EXPERTS, LANES), F32),
        jax.ShapeDtypeStruct((1, bsz, POOL_STATE, D_POOL), F32),
        jax.ShapeDtypeStruct((1, bsz, CONV_STATE, D_CONV), F32),
    )
    out_specs = (
        rows(D_MODEL),
        pl.BlockSpec((tt * HM_PIECES, LANES), lambda b, j: (b * n_t + j, 0)),
        rows(LANES),
        pl.BlockSpec((None, SUBLANES, tt), lambda b, j: (b * n_t + j, 0, 0)),
        pl.BlockSpec((N_EXPERTS, LANES), lambda b, j: (0, 0)),
        pl.BlockSpec((None, None, POOL_STATE, D_POOL), lambda b, j: (0, b, 0, 0)),
        pl.BlockSpec((None, None, CONV_STATE, D_CONV), lambda b, j: (0, b, 0, 0)),
    )
    return pl.pallas_call(
        functools.partial(_mixer_kernel, pos0, tt, n_t),
        out_shape=out_shape,
        grid=(bsz, n_t),
        in_specs=in_specs,
        out_specs=out_specs,
        scratch_shapes=[
            pltpu.VMEM((POOL_PAD + tt, D_POOL), F32),
            pltpu.VMEM((CONV_BLOCKS, CONV_PAD + tt, LANES), F32),
            pltpu.VMEM((CONV_BLOCKS, tt, LANES), F32),
            pltpu.VMEM((N_EXPERTS, LANES), F32),
            pltpu.VMEM((POOL_PAD + tt, D_POOL), F32),
            pltpu.VMEM((POOL_PAD + tt, D_POOL), F32),
            pltpu.VMEM((POOL_PAD + tt, D_POOL), F32),
        ],
        compiler_params=pltpu.CompilerParams(
            dimension_semantics=("arbitrary", "arbitrary"), vmem_limit_bytes=VMEM_LIMIT),
        name=name,
    )(x, pool_prefix, conv_prefix, *weights, tri)


def _dispatch_kernel(n_sample_tiles, last_ref, owns_ref, used_ref, dest_ref, hmp_ref, hms_ref, xb_ref,
                     zbuf, sem, zsem):
    i = pl.program_id(0)
    tc = dest_ref.shape[2]
    n_prompt_tiles = pl.num_programs(0) - n_sample_tiles
    tile_rows = zbuf.shape[0]
    n_all = xb_ref.shape[0] // tile_rows

    def zero_copy(t):
        dst = xb_ref.at[pl.ds(pl.multiple_of(t * tile_rows, tile_rows), tile_rows), :]
        return pltpu.make_async_copy(zbuf, dst, zsem)

    @pl.when(i == 0)
    def _():
        zbuf[...] = jnp.zeros_like(zbuf)
        for e in range(N_EXPERTS):
            @pl.when(owns_ref[e] > 0)
            def _(e=e):
                zero_copy(last_ref[e]).start()
        lax.fori_loop(used_ref[0], n_all, lambda t, c: (zero_copy(t).start(), c)[1], 0)
        for e in range(N_EXPERTS):
            @pl.when(owns_ref[e] > 0)
            def _(e=e):
                zero_copy(last_ref[e]).wait()
        lax.fori_loop(used_ref[0], n_all, lambda t, c: (zero_copy(t).wait(), c)[1], 0)

    def scatter(hm_ref):
        for t in range(tc):
            for k in range(TOP_K):
                d = dest_ref[0, k, t]
                src = hm_ref.at[pl.ds(t * HM_PIECES, HM_PIECES), :]
                dst = xb_ref.at[pl.ds(pl.multiple_of(d * HM_PIECES, HM_PIECES), HM_PIECES), :]
                pltpu.make_async_copy(src, dst, sem).start(priority=k)
        for k in range(TOP_K):
            pltpu.make_async_copy(hm_ref, xb_ref.at[pl.ds(0, tc * HM_PIECES), :], sem).wait()

    @pl.when(i < n_prompt_tiles)
    def _():
        scatter(hmp_ref)

    @pl.when(i >= n_prompt_tiles)
    def _():
        scatter(hms_ref)


def _dispatch_call(last_tile, owns, n_used, dest_blocks, hm_p, hm_s, n_slots):
    tc = DISPATCH_TILE
    n_p = hm_p.shape[0] // (tc * HM_PIECES)
    n_s = hm_s.shape[0] // (tc * HM_PIECES)
    grid_spec = pltpu.PrefetchScalarGridSpec(
        num_scalar_prefetch=3,
        grid=(n_p + n_s,),
        in_specs=[
            pl.BlockSpec((1, TOP_K, tc), lambda i, *_: (i, 0, 0), memory_space=pltpu.SMEM),
            pl.BlockSpec((tc * HM_PIECES, LANES), lambda i, *_: (jnp.minimum(i, n_p - 1), 0)),
            pl.BlockSpec((tc * HM_PIECES, LANES), lambda i, *_: (jnp.maximum(i - n_p, 0), 0)),
        ],
        out_specs=pl.BlockSpec(memory_space=pl.ANY),
        scratch_shapes=[
            pltpu.VMEM((EXPERT_TILE * HM_PIECES, LANES), U32),
            pltpu.SemaphoreType.DMA(()),
            pltpu.SemaphoreType.DMA(()),
        ],
    )
    return pl.pallas_call(
        functools.partial(_dispatch_kernel, n_s),
        out_shape=jax.ShapeDtypeStruct((n_slots * HM_PIECES, LANES), U32),
        grid_spec=grid_spec,
        compiler_params=pltpu.CompilerParams(dimension_semantics=("arbitrary",)),
        name="dispatch",
    )(last_tile, owns, n_used, dest_blocks, hm_p, hm_s)


def _pack_bf16_pairs(x):
    w = x.shape[1] // 2
    rounded = x.astype(BF16).astype(F32)
    lo = lax.bitcast_convert_type(rounded[:, :w], U32)
    hi = lax.bitcast_convert_type(rounded[:, w:], U32)
    return (hi & jnp.uint32(0xFFFF0000)) | lax.shift_right_logical(lo, jnp.uint32(16))


def _unpack_bf16_pairs(words):
    lo = lax.bitcast_convert_type(lax.shift_left(words, jnp.uint32(16)), F32)
    hi = lax.bitcast_convert_type(words & jnp.uint32(0xFFFF0000), F32)
    return lo, hi


def _expert_kernel(ord_ref, act_ref, used_ref, nact_ref, wg_ref, wu_ref, wd_ref, xb_ref, yb_ref,
                   wgf, wuf, wdf, wgb, wub, wdb, xbuf, ybuf, sem_w, sem_in, sem_out):
    tile_rows = xbuf.shape[1]
    tm = tile_rows // HM_PIECES
    n = used_ref[0]
    n_act = nact_ref[0]

    def weight_copies(k, slot):
        e = act_ref[k]
        return [pltpu.make_async_copy(src.at[e], dst.at[slot], sem_w.at[slot])
                for src, dst in ((wg_ref, wgf), (wu_ref, wuf), (wd_ref, wdf))]

    def tile(ref, t):
        return ref.at[pl.ds(pl.multiple_of(t * tile_rows, tile_rows), tile_rows), :]

    def fetch(t, slot):
        return pltpu.make_async_copy(tile(xb_ref, t), xbuf.at[slot], sem_in.at[slot])

    def put(t, slot):
        return pltpu.make_async_copy(ybuf.at[slot], tile(yb_ref, t), sem_out.at[slot])

    for cp in weight_copies(0, 0):
        cp.start()
    for ahead in range(X_SLOTS - 1):
        @pl.when(ahead < n)
        def _(ahead=ahead):
            fetch(ahead, ahead).start()

    def body(t, carry):
        k = ord_ref[t]

        @pl.when((t == 0) | (k != ord_ref[jnp.maximum(t - 1, 0)]))
        def _():
            wslot = lax.rem(k, 2)
            for cp in weight_copies(k, wslot):
                cp.wait()
            wgb[...] = wgf[wslot].astype(BF16)
            wub[...] = wuf[wslot].astype(BF16)
            wdb[...] = wdf[wslot].astype(BF16)

            @pl.when(k + 1 < n_act)
            def _():
                for cp in weight_copies(k + 1, 1 - wslot):
                    cp.start()

        slot = lax.rem(t, X_SLOTS)
        fetch(t, slot).wait()

        @pl.when(t + X_SLOTS - 1 < n)
        def _():
            fetch(t + X_SLOTS - 1, lax.rem(t + X_SLOTS - 1, X_SLOTS)).start()

        oslot = lax.rem(t, 2)

        @pl.when(t >= 2)
        def _():
            put(t - 2, oslot).wait()

        pieces = [_unpack_bf16_pairs(xbuf[slot, pl.ds(p, tm, stride=HM_PIECES), :]) for p in range(HM_PIECES)]
        xt = jnp.concatenate([lo.astype(BF16) for lo, _ in pieces] + [hi.astype(BF16) for _, hi in pieces],
                             axis=-1)
        g = jnp.dot(xt, wgb[...], preferred_element_type=F32)
        up = jnp.dot(xt, wub[...], preferred_element_type=F32)
        a = (g * jax.nn.sigmoid(g) * up).astype(BF16)
        packed = _pack_bf16_pairs(jnp.dot(a, wdb[...], preferred_element_type=F32))
        for p in range(HM_PIECES):
            ybuf[oslot, pl.ds(p, tm, stride=HM_PIECES), :] = packed[:, p * LANES:(p + 1) * LANES]
        put(t, oslot).start()
        return carry

    lax.fori_loop(0, n, body, 0)

    @pl.when(n >= 2)
    def _():
        put(n - 2, lax.rem(n, 2)).wait()

    @pl.when(n >= 1)
    def _():
        put(n - 1, lax.rem(n - 1, 2)).wait()

    ybuf[0] = jnp.zeros(ybuf.shape[1:], ybuf.dtype)
    n_all = yb_ref.shape[0] // tile_rows

    def zero_tile(t, carry):
        cp = pltpu.make_async_copy(ybuf.at[0], tile(yb_ref, t), sem_out.at[0])
        cp.start()
        cp.wait()
        return carry

    lax.fori_loop(n, n_all, zero_tile, 0)


def _expert_call(tile_ord, active, n_used, n_active, xb, w_eg, w_eu, w_ed):
    tile_rows = EXPERT_TILE * HM_PIECES
    any_spec = pl.BlockSpec(memory_space=pl.ANY)
    grid_spec = pltpu.PrefetchScalarGridSpec(
        num_scalar_prefetch=4,
        grid=(1,),
        in_specs=[any_spec] * 4,
        out_specs=any_spec,
        scratch_shapes=[
            pltpu.VMEM((2, D_MODEL, D_FF), F32),
            pltpu.VMEM((2, D_MODEL, D_FF), F32),
            pltpu.VMEM((2, D_FF, D_MODEL), F32),
            pltpu.VMEM((D_MODEL, D_FF), BF16),
            pltpu.VMEM((D_MODEL, D_FF), BF16),
            pltpu.VMEM((D_FF, D_MODEL), BF16),
            pltpu.VMEM((X_SLOTS, tile_rows, LANES), U32),
            pltpu.VMEM((2, tile_rows, LANES), U32),
            pltpu.SemaphoreType.DMA((2,)),
            pltpu.SemaphoreType.DMA((X_SLOTS,)),
            pltpu.SemaphoreType.DMA((2,)),
        ],
    )
    return pl.pallas_call(
        _expert_kernel,
        out_shape=jax.ShapeDtypeStruct(xb.shape, U32),
        grid_spec=grid_spec,
        compiler_params=pltpu.CompilerParams(
            dimension_semantics=("arbitrary",), vmem_limit_bytes=VMEM_LIMIT),
        name="experts",
    )(tile_ord, active, n_used, n_active, w_eg, w_eu, w_ed, xb)


def _gather_rows(dest_ref, yb_ref, buf_ref, sem, tokens):
    for t in tokens:
        for k in range(TOP_K):
            d = dest_ref[0, k, t]
            src = yb_ref.at[pl.ds(pl.multiple_of(d * HM_PIECES, HM_PIECES), HM_PIECES), :]
            dst = buf_ref.at[k, pl.ds(t * HM_PIECES, HM_PIECES), :]
            pltpu.make_async_copy(src, dst, sem).start(priority=k)


def _wait_rows(yb_ref, buf_ref, sem):
    for k in range(TOP_K):
        pltpu.make_async_copy(yb_ref.at[pl.ds(0, buf_ref.shape[1]), :], buf_ref.at[k], sem).wait()


def _combine_kernel(dest0_ref, dest1_ref, dest2_ref, x1_ref, gates_ref, gf_ref, yb_ref, out_ref, buf, sems):
    i = pl.program_id(0)
    tc = TOKEN_TILE
    n_chunks = tc // COMBINE_CHUNK

    @pl.when(i == 0)
    def _():
        _gather_rows(dest0_ref, yb_ref, buf.at[0], sems.at[0], range(tc))

    gf = gf_ref[...]
    for half, next_dest in ((0, dest1_ref), (1, dest2_ref)):
        cur, nxt = buf.at[half], buf.at[1 - half]
        _wait_rows(yb_ref, cur, sems.at[half])
        for c in range(n_chunks):
            rows = pl.ds(half * tc + c * COMBINE_CHUNK, COMBINE_CHUNK)
            gates = gates_ref[rows, :]
            lows, highs = [], []
            for p in range(HM_PIECES):
                piece = pl.ds(c * COMBINE_CHUNK * HM_PIECES + p, COMBINE_CHUNK, stride=HM_PIECES)
                lo0, hi0 = _unpack_bf16_pairs(cur[0, piece, :])
                lo1, hi1 = _unpack_bf16_pairs(cur[1, piece, :])
                lows.append(lo0 * gates[:, 0:1] + lo1 * gates[:, 1:2])
                highs.append(hi0 * gates[:, 0:1] + hi1 * gates[:, 1:2])
            xo = x1_ref[rows, :] + jnp.concatenate(lows + highs, axis=-1)
            _gather_rows(next_dest, yb_ref, nxt, sems.at[1 - half],
                         range(c * COMBINE_CHUNK, (c + 1) * COMBINE_CHUNK))
            ms = jnp.mean(xo * xo, axis=-1, keepdims=True)
            out_ref[rows, :] = xo * lax.rsqrt(ms + EPS) * gf

    @pl.when(i == pl.num_programs(0) - 1)
    def _():
        _wait_rows(yb_ref, buf.at[0], sems.at[0])


def _combine_call(dest_blocks, x1, gates, gf, yb, name):
    n_rows = x1.shape[0]
    tc = TOKEN_TILE
    n_tiles = n_rows // tc
    assert n_tiles % 2 == 0

    def dest_spec(index_map):
        return pl.BlockSpec((1, TOP_K, tc), index_map, memory_space=pltpu.SMEM)

    return pl.pallas_call(
        _combine_kernel,
        out_shape=jax.ShapeDtypeStruct((n_rows, D_MODEL), F32),
        grid=(n_tiles // 2,),
        in_specs=[
            dest_spec(lambda i: (0, 0, 0)),
            dest_spec(lambda i: (2 * i + 1, 0, 0)),
            dest_spec(lambda i: (jnp.minimum(2 * i + 2, n_tiles - 1), 0, 0)),
            pl.BlockSpec((2 * tc, D_MODEL), lambda i: (i, 0)),
            pl.BlockSpec((2 * tc, LANES), lambda i: (i, 0)),
            pl.BlockSpec((1, D_MODEL), lambda i: (0, 0)),
            pl.BlockSpec(memory_space=pl.ANY),
        ],
        out_specs=pl.BlockSpec((2 * tc, D_MODEL), lambda i: (i, 0)),
        scratch_shapes=[pltpu.VMEM((2, TOP_K, tc * HM_PIECES, LANES), U32), pltpu.SemaphoreType.DMA((2,))],
        compiler_params=pltpu.CompilerParams(
            dimension_semantics=("arbitrary",), vmem_limit_bytes=VMEM_LIMIT),
        name=name,
    )(dest_blocks, dest_blocks, dest_blocks, x1, gates, gf, yb)


def _block_diag2(a, b):
    z = jnp.zeros_like(a)
    return jnp.concatenate([jnp.concatenate([a, z], axis=1), jnp.concatenate([z, b], axis=1)], axis=0)


def kernel(x_prompt, x_sample, state_pool, state_conv, norm1_g, w_in, pool_lin, pool_scale, conv_dw, conv_dw_b, conv_ln_g, conv_ln_b, w_out, b_out, norm2_g, w_rg, b_rg, w_re, b_re, w_eg, w_eu, w_ed, norm_f_g):
    assert norm1_g.shape[0] == 1, "single-layer trunk"
    bsz, seq, _ = x_prompt.shape
    dbsz, dseq, _ = x_sample.shape
    past_len = 1024
    n_prompt = bsz * seq
    n_sample = dbsz * dseq
    n_tokens = n_prompt + n_sample
    assert seq % PROMPT_TILE == 0
    assert all(n % t == 0 for n in (n_prompt, n_sample) for t in (TOKEN_TILE, DISPATCH_TILE))

    row = lambda a: a.reshape(1, -1)
    router_w = jnp.zeros((ROUTER_ROWS, D_MODEL), F32)
    router_w = router_w.at[0:N_EXPERT_GROUPS].set(w_rg[0].T)
    router_w = router_w.at[EXPERT_ROW0:EXPERT_ROW0 + N_EXPERTS].set(
        jnp.transpose(w_re[0], (0, 2, 1)).reshape(N_EXPERTS, D_MODEL))
    router_b = jnp.zeros((ROUTER_ROWS,), F32)
    router_b = router_b.at[0:N_EXPERT_GROUPS].set(b_rg[0])
    router_b = router_b.at[EXPERT_ROW0:EXPERT_ROW0 + N_EXPERTS].set(b_re[0].reshape(-1))
    conv_w = jnp.concatenate([conv_dw[0], jnp.zeros((1, D_CONV), F32)], axis=0)
    weights = [
        row(norm1_g[0]), w_in[0].astype(BF16),
        _block_diag2(pool_lin[0, 0], pool_lin[0, 1]).astype(BF16),
        _block_diag2(pool_lin[0, 2], pool_lin[0, 3]).astype(BF16),
        row(pool_scale[0]), conv_w, row(conv_dw_b[0]), row(conv_ln_g[0]), row(conv_ln_b[0]),
        w_out[0].astype(BF16), row(b_out[0]), row(norm2_g[0]),
        router_w.astype(BF16), router_b.reshape(ROUTER_ROWS, 1),
    ]

    zpool = jnp.zeros((bsz, POOL_PAD, D_POOL), F32)
    zconv = jnp.zeros((bsz, CONV_PAD, D_CONV), F32)
    ppool = jnp.pad(state_pool[0], ((0, 0), (POOL_PAD - POOL_STATE, 0), (0, 0)))
    pconv = jnp.pad(state_conv[0], ((0, 0), (CONV_PAD - CONV_STATE, 0), (0, 0)))

    x1_p, hm_p, gates_p, route_p, cnt_p, sp_p, sc_p = _mixer_call(
        x_prompt, zpool, zconv, 0, PROMPT_TILE, weights, "mixer_prompt")
    x1_s, hm_s, gates_s, route_s, cnt_s, sp_s, sc_s = _mixer_call(
        x_sample, ppool, pconv, past_len, dseq, weights, "mixer_sample")

    tm = EXPERT_TILE
    cnt_p = cnt_p[:, 0].astype(I32)
    cnt_s = cnt_s[:, 0].astype(I32)
    padded = (cnt_p + cnt_s + tm - 1) // tm * tm
    pad_end = jnp.cumsum(padded)
    pad_start = pad_end - padded

    def dest_blocks(route, base, tile):
        eid = jnp.stack([route[:, k, :].reshape(-1) for k in range(TOP_K)])
        rank = jnp.stack([route[:, TOP_K + k, :].reshape(-1) for k in range(TOP_K)])
        experts = jnp.arange(N_EXPERTS, dtype=I32)
        dest = rank + jnp.sum(jnp.where(eid[..., None] == experts, base, 0), axis=-1)
        return dest.reshape(TOP_K, -1, tile).transpose(1, 0, 2)

    dest_p = dest_blocks(route_p, pad_start, TOKEN_TILE)
    dest_s = dest_blocks(route_s, pad_start + cnt_p, TOKEN_TILE)
    dest_all = jnp.concatenate([dest_blocks(route_p, pad_start, DISPATCH_TILE),
                                dest_blocks(route_s, pad_start + cnt_p, DISPATCH_TILE)], axis=0)
    n_tiles = -(-(n_tokens * TOP_K + N_EXPERTS * (tm - 1)) // tm)
    n_used = (pad_end[-1:] // tm).astype(I32)

    owns = padded > 0
    xb = _dispatch_call((pad_end // tm - 1).astype(I32), owns.astype(I32), n_used,
                        dest_all, hm_p, hm_s, n_tiles * tm)
    active = jnp.argsort(jnp.logical_not(owns), stable=True).astype(I32)
    tile_expert = jnp.minimum(
        jnp.sum(pad_end[None, :] <= (jnp.arange(n_tiles, dtype=I32) * tm)[:, None], axis=1), N_EXPERTS - 1)
    ordinal = jnp.cumsum(owns.astype(I32)) - 1
    tile_ord = jnp.sum(jnp.where(tile_expert[:, None] == jnp.arange(N_EXPERTS, dtype=I32), ordinal, 0),
                       axis=1).astype(I32)
    n_active = jnp.sum(owns.astype(I32)).reshape(1)
    yb = _expert_call(tile_ord, active, n_used, n_active, xb, w_eg[0], w_eu[0], w_ed[0])
    gf = row(norm_f_g)
    y_p = _combine_call(dest_p, x1_p, gates_p, gf, yb, "combine_prompt")
    y_s = _combine_call(dest_s, x1_s, gates_s, gf, yb, "combine_sample")

    return (y_p.reshape(bsz, seq, D_MODEL), y_s.reshape(dbsz, dseq, D_MODEL),
            sp_p, sc_p, sp_s, sc_s)
```

```python
import functools

import jax
import jax.numpy as jnp
from jax import lax
from jax.experimental import pallas as pl
from jax.experimental.pallas import tpu as pltpu

F32 = jnp.float32
BF16 = jnp.bfloat16
U32 = jnp.uint32
I32 = jnp.int32

D_MODEL = 1024
D_POOL = 512
D_CONV = 512
D_IN = D_POOL + 2 * D_CONV
POOL_WINDOWS = (2, 4, 8, 16)
POOL_GROUP_DIM = D_POOL // len(POOL_WINDOWS)
POOL_STATE = max(POOL_WINDOWS) - 1
CONV_WIDTH = 31
CONV_STATE = CONV_WIDTH - 1
N_EXPERT_GROUPS = 4
EXPERTS_PER_GROUP = 8
N_EXPERTS = N_EXPERT_GROUPS * EXPERTS_PER_GROUP
TOP_K = 2
D_FF = D_MODEL // 2
EPS = 1e-6

SUBLANES = 8
LANES = 128
PAD = 32
ROUTER_ROWS = LANES
EXPERT_ROW0 = SUBLANES
CONV_CHUNK = 64
CONV_STRIDE = 2
CONV_BLOCKS = D_CONV // LANES
MIXER_PART = 256
HM_WORDS = D_MODEL // 2
HM_PIECES = HM_WORDS // LANES

PROMPT_TILE = 512
SAMPLE_TILE = 512
TOKEN_TILE = 512
DISPATCH_TILE = 1024
EXPERT_TILE = 512
X_SLOTS = 4
COMBINE_CHUNK = 32
V7X_VMEM_BYTES = 64 * 1024 * 1024
VMEM_LIMIT = V7X_VMEM_BYTES * 3 // 4


def _iota_f32(shape, axis):
    return lax.broadcasted_iota(I32, shape, axis).astype(F32)


def _mixer_kernel(pos0, tt, n_t, seg,
                  x_ref, pp_ref, cp_ref, g1_ref, win_ref, bd01_ref, bd23_ref, pscale_ref,
                  cw_ref, cb_ref, lng_ref, lnb_ref, wout_ref, bout_ref, g2_ref, wrt_ref, br_ref,
                  tri_ref,
                  x1_ref, hm_ref, gates_ref, route_ref, cnt_ref, spool_ref, sconv_ref,
                  pext, cext, hbuf, carry, lv2, lv4, lv8):
    b = pl.program_id(0)
    j = pl.program_id(1)

    @pl.when((b == 0) & (j == 0))
    def _():
        carry[...] = jnp.zeros_like(carry)

    n_seg = tt // seg
    seg_stride = PAD + seg
    assert n_seg == 1 or n_t == 1, "several segments per tile: each one is a whole sequence"

    @pl.when(j == 0)
    def _():
        for g in range(n_seg):
            pext[g * seg_stride:g * seg_stride + PAD, :] = pp_ref[g]
            for lb in range(CONV_BLOCKS):
                cext[lb, g * seg_stride:g * seg_stride + PAD, :] = cp_ref[g, :, lb * LANES:(lb + 1) * LANES]

    @pl.when(j > 0)
    def _():
        pext[0:PAD, :] = pext[tt:tt + PAD, :]
        cext[:, 0:PAD, :] = cext[:, tt:tt + PAD, :]

    n_parts = max(1, tt // MIXER_PART)
    rp = tt // n_parts
    gd = POOL_GROUP_DIM
    neg = jnp.float32(-jnp.inf)

    def runs_of(part):
        r0 = part * rp
        if n_seg == 1:
            return [(r0, rp, r0)]
        return [(t0, seg, t0 // seg * seg_stride) for t0 in range(r0, r0 + rp, seg)]

    def stage_in(part):
        r0 = part * rp
        x = x_ref[r0:r0 + rp, :]
        ms = jnp.mean(x * x, axis=-1, keepdims=True)
        h = (x * lax.rsqrt(ms + EPS) * g1_ref[...]).astype(BF16)
        z = jnp.dot(h, win_ref[...], preferred_element_type=F32)
        v = z[:, D_POOL:D_POOL + D_CONV] * jax.nn.sigmoid(z[:, D_POOL + D_CONV:])
        for t0, n, v0 in runs_of(part):
            pext[PAD + v0:PAD + v0 + n, :] = z[t0 - r0:t0 - r0 + n, :D_POOL]
            for lb in range(CONV_BLOCKS):
                cext[lb, PAD + v0:PAD + v0 + n, :] = v[t0 - r0:t0 - r0 + n, lb * LANES:(lb + 1) * LANES]

    def stage_mix(part):
        return jnp.concatenate([mix_run(*run) for run in runs_of(part)], axis=0)

    def mix_run(t0, rp, r0):
        first_pos = pos0 + (j * tt + t0 if n_seg == 1 else 0)
        pos = first_pos + lax.broadcasted_iota(I32, (rp, 1), 0)
        top = r0 + PAD
        end = top + rp
        lv2[r0 + 8:end, :] = pext[r0 + 8:end, :] + pext[r0 + 7:end - 1, :]
        lv4[r0 + 16:end, gd:] = lv2[r0 + 16:end, gd:] + lv2[r0 + 14:end - 2, gd:]
        lv8[r0 + 24:end, 2 * gd:] = lv4[r0 + 24:end, 2 * gd:] + lv4[r0 + 20:end - 4, 2 * gd:]
        sums = [lv2[top:end, 0:gd], lv4[top:end, gd:2 * gd], lv8[top:end, 2 * gd:3 * gd],
                lv8[top:end, 3 * gd:] + lv8[top - 8:end - 8, 3 * gd:]]
        u = pext[top:end, :]
        parts = []
        for g, w in enumerate(POOL_WINDOWS):
            sl = slice(g * gd, (g + 1) * gd)
            cnt = jnp.minimum(pos + 1, w).astype(F32)
            parts.append(sums[g] / cnt - u[:, sl])
        d = jnp.concatenate(parts, axis=-1).astype(BF16)
        half = D_POOL // 2
        yp = jnp.concatenate(
            [jnp.dot(d[:, :half], bd01_ref[...], preferred_element_type=F32),
             jnp.dot(d[:, half:], bd23_ref[...], preferred_element_type=F32)], axis=-1)
        yp = yp * pscale_ref[...]

        chunk = min(CONV_CHUNK, rp // CONV_STRIDE)
        for lb in range(CONV_BLOCKS):
            lanes = slice(lb * LANES, (lb + 1) * LANES)
            wts = [jnp.broadcast_to(cw_ref[k:k + 1, lanes], (chunk, LANES)) for k in range(CONV_WIDTH)]
            for first in range(r0, r0 + rp, CONV_STRIDE * chunk):
                for phase in range(CONV_STRIDE):
                    acc = None
                    for k in range(CONV_WIDTH):
                        start = first + phase + PAD - CONV_STATE + k
                        tap = cext[lb, pl.ds(start, chunk, stride=CONV_STRIDE), :] * wts[k]
                        acc = tap if acc is None else acc + tap
                    hbuf[lb, pl.ds(first + phase, chunk, stride=CONV_STRIDE), :] = acc
        return yp

    def stage_out(part, yp):
        r0 = part * rp
        x = x_ref[r0:r0 + rp, :]
        hc = jnp.concatenate(
            [jnp.concatenate([hbuf[lb, v0:v0 + n, :] for _, n, v0 in runs_of(part)], axis=0)
             for lb in range(CONV_BLOCKS)], axis=-1) + cb_ref[...]
        mu = jnp.mean(hc, axis=-1, keepdims=True)
        var = jnp.mean(jnp.square(hc - mu), axis=-1, keepdims=True)
        yln = (hc - mu) * lax.rsqrt(var + EPS) * lng_ref[...] + lnb_ref[...]
        yc = yln * jax.nn.sigmoid(yln)

        o = (jnp.dot(yp.astype(BF16), wout_ref[0:D_POOL, :], preferred_element_type=F32)
             + jnp.dot(yc.astype(BF16), wout_ref[D_POOL:, :], preferred_element_type=F32))
        x1 = x + (o + bout_ref[...])
        x1_ref[r0:r0 + rp, :] = x1

        ms2 = jnp.mean(x1 * x1, axis=-1, keepdims=True)
        hmb = (x1 * lax.rsqrt(ms2 + EPS) * g2_ref[...]).astype(BF16)
        packed = _pack_bf16_pairs(hmb.astype(F32))
        for p in range(HM_PIECES):
            hm_ref[pl.ds(r0 * HM_PIECES + p, rp, stride=HM_PIECES), :] = packed[:, p * LANES:(p + 1) * LANES]

        lt = lax.dot_general(wrt_ref[...], hmb, (((1,), (1,)), ((), ())),
                             preferred_element_type=F32) + br_ref[...]
        row8 = _iota_f32((SUBLANES, rp), 0)
        lg = jnp.where(row8 < N_EXPERT_GROUPS, lt[0:SUBLANES, :], neg)
        mg = jnp.max(lg, axis=0, keepdims=True)
        p_grp = 1.0 / jnp.sum(jnp.exp(lg - mg), axis=0, keepdims=True)
        gsel = jnp.min(jnp.where(lg == mg, row8, float(SUBLANES)), axis=0, keepdims=True)
        le = jnp.zeros((SUBLANES, rp), F32)
        for g in range(N_EXPERT_GROUPS):
            e0 = EXPERT_ROW0 + g * EXPERTS_PER_GROUP
            le = jnp.where(gsel == float(g), lt[e0:e0 + EXPERTS_PER_GROUP, :], le)
        m1 = jnp.max(le, axis=0, keepdims=True)
        i1 = jnp.min(jnp.where(le == m1, row8, float(SUBLANES)), axis=0, keepdims=True)
        le2 = jnp.where(row8 == i1, neg, le)
        m2 = jnp.max(le2, axis=0, keepdims=True)
        i2 = jnp.min(jnp.where(le2 == m2, row8, float(SUBLANES)), axis=0, keepdims=True)
        e2 = jnp.exp(m2 - m1)
        den = 1.0 + e2
        gate0 = p_grp * (1.0 / den)
        gate1 = p_grp * (e2 / den)
        eid0 = gsel * float(EXPERTS_PER_GROUP) + i1
        eid1 = gsel * float(EXPERTS_PER_GROUP) + i2

        rowe = _iota_f32((N_EXPERTS, rp), 0)
        oh0 = rowe == eid0
        oh1 = rowe == eid1
        ohf = jnp.where(jnp.logical_or(oh0, oh1), 1.0, 0.0)
        before = jnp.dot(ohf.astype(BF16), tri_ref[0:rp, 0:rp], preferred_element_type=F32) + carry[:, 0:1]
        rank0 = jnp.sum(jnp.where(oh0, before, 0.0), axis=0, keepdims=True)
        rank1 = jnp.sum(jnp.where(oh1, before, 0.0), axis=0, keepdims=True)
        carry[...] = carry[...] + jnp.sum(ohf, axis=1, keepdims=True)

        route = jnp.where(row8 == 0.0, eid0,
                          jnp.where(row8 == 1.0, eid1,
                                    jnp.where(row8 == 2.0, rank0,
                                              jnp.where(row8 == 3.0, rank1, 0.0))))
        route_ref[:, r0:r0 + rp] = route.astype(I32)

        tpad = -(-rp // LANES) * LANES
        rowr = _iota_f32((ROUTER_ROWS, rp), 0)
        gt = jnp.where(rowr == 0.0, gate0, jnp.where(rowr == 1.0, gate1, 0.0))
        if tpad != rp:
            gt = jnp.concatenate([gt, jnp.zeros((ROUTER_ROWS, tpad - rp), F32)], axis=1)
        gates_ref[r0:r0 + rp, :] = gt.T[0:rp, :]

    stage_in(0)
    for part in range(n_parts):
        yp = stage_mix(part)
        if part + 1 < n_parts:
            stage_in(part + 1)
        stage_out(part, yp)
    cnt_ref[...] = carry[...]

    @pl.when(j == n_t - 1)
    def _():
        for g in range(n_seg):
            last = g * seg_stride + PAD + seg
            spool_ref[g] = pext[last - POOL_STATE:last, :]
            sconv_ref[g] = jnp.concatenate(
                [cext[lb, last - CONV_STATE:last, :] for lb in range(CONV_BLOCKS)], axis=-1)


def _mixer_call(x, pool_prefix, conv_prefix, pos0, tt, seg, weights, name):
    bsz, seq, _ = x.shape
    n_t = seq // tt
    n_seg = tt // seg
    ext_rows = n_seg * (PAD + seg)
    n_blk = bsz * n_t
    n_rows = bsz * seq
    tri = (lax.broadcasted_iota(I32, (tt, tt), 0) < lax.broadcasted_iota(I32, (tt, tt), 1)).astype(BF16)

    def const(shape):
        return pl.BlockSpec(shape, lambda b, j: (0,) * len(shape))

    def rows(width):
        return pl.BlockSpec((tt, width), lambda b, j: (b * n_t + j, 0))

    in_specs = [
        pl.BlockSpec((None, tt, D_MODEL), lambda b, j: (b, j, 0)),
        pl.BlockSpec((n_seg, PAD, D_POOL), lambda b, j: (b, 0, 0)),
        pl.BlockSpec((n_seg, PAD, D_CONV), lambda b, j: (b, 0, 0)),
    ] + [const(w.shape) for w in weights] + [const((tt, tt))]
    out_shape = (
        jax.ShapeDtypeStruct((n_rows, D_MODEL), F32),
        jax.ShapeDtypeStruct((n_rows * HM_PIECES, LANES), U32),
        jax.ShapeDtypeStruct((n_rows, LANES), F32),
        jax.ShapeDtypeStruct((n_blk, SUBLANES, tt), I32),
        jax.ShapeDtypeStruct((N_EXPERTS, LANES), F32),
        jax.ShapeDtypeStruct((1, bsz * n_seg, POOL_STATE, D_POOL), F32),
        jax.ShapeDtypeStruct((1, bsz * n_seg, CONV_STATE, D_CONV), F32),
    )
    out_specs = (
        rows(D_MODEL),
        pl.BlockSpec((tt * HM_PIECES, LANES), lambda b, j: (b * n_t + j, 0)),
        rows(LANES),
        pl.BlockSpec((None, SUBLANES, tt), lambda b, j: (b * n_t + j, 0, 0)),
        pl.BlockSpec((N_EXPERTS, LANES), lambda b, j: (0, 0)),
        pl.BlockSpec((None, n_seg, POOL_STATE, D_POOL), lambda b, j: (0, b, 0, 0)),
        pl.BlockSpec((None, n_seg, CONV_STATE, D_CONV), lambda b, j: (0, b, 0, 0)),
    )
    return pl.pallas_call(
        functools.partial(_mixer_kernel, pos0, tt, n_t, seg),
        out_shape=out_shape,
        grid=(bsz, n_t),
        in_specs=in_specs,
        out_specs=out_specs,
        scratch_shapes=[
            pltpu.VMEM((ext_rows, D_POOL), F32),
            pltpu.VMEM((CONV_BLOCKS, ext_rows, LANES), F32),
            pltpu.VMEM((CONV_BLOCKS, ext_rows, LANES), F32),
            pltpu.VMEM((N_EXPERTS, LANES), F32),
            pltpu.VMEM((ext_rows, D_POOL), F32),
            pltpu.VMEM((ext_rows, D_POOL), F32),
            pltpu.VMEM((ext_rows, D_POOL), F32),
        ],
        compiler_params=pltpu.CompilerParams(
            dimension_semantics=("arbitrary", "arbitrary"), vmem_limit_bytes=VMEM_LIMIT),
        name=name,
    )(x, pool_prefix, conv_prefix, *weights, tri)


def _dispatch_kernel(n_sample_tiles, last_ref, owns_ref, used_ref, dest_ref, hmp_ref, hms_ref, xb_ref,
                     zbuf, sem, zsem):
    i = pl.program_id(0)
    tc = dest_ref.shape[2]
    n_prompt_tiles = pl.num_programs(0) - n_sample_tiles
    tile_rows = zbuf.shape[0]
    n_all = xb_ref.shape[0] // tile_rows

    def zero_copy(t):
        dst = xb_ref.at[pl.ds(pl.multiple_of(t * tile_rows, tile_rows), tile_rows), :]
        return pltpu.make_async_copy(zbuf, dst, zsem)

    @pl.when(i == 0)
    def _():
        zbuf[...] = jnp.zeros_like(zbuf)
        for e in range(N_EXPERTS):
            @pl.when(owns_ref[e] > 0)
            def _(e=e):
                zero_copy(last_ref[e]).start()
        lax.fori_loop(used_ref[0], n_all, lambda t, c: (zero_copy(t).start(), c)[1], 0)
        for e in range(N_EXPERTS):
            @pl.when(owns_ref[e] > 0)
            def _(e=e):
                zero_copy(last_ref[e]).wait()
        lax.fori_loop(used_ref[0], n_all, lambda t, c: (zero_copy(t).wait(), c)[1], 0)

    def scatter(hm_ref):
        for t in range(tc):
            for k in range(TOP_K):
                d = dest_ref[0, k, t]
                src = hm_ref.at[pl.ds(t * HM_PIECES, HM_PIECES), :]
                dst = xb_ref.at[pl.ds(pl.multiple_of(d * HM_PIECES, HM_PIECES), HM_PIECES), :]
                pltpu.make_async_copy(src, dst, sem).start(priority=k)
        for k in range(TOP_K):
            pltpu.make_async_copy(hm_ref, xb_ref.at[pl.ds(0, tc * HM_PIECES), :], sem).wait()

    @pl.when(i < n_prompt_tiles)
    def _():
        scatter(hmp_ref)

    @pl.when(i >= n_prompt_tiles)
    def _():
        scatter(hms_ref)


def _dispatch_call(last_tile, owns, n_used, dest_blocks, hm_p, hm_s, n_slots):
    tc = DISPATCH_TILE
    n_p = hm_p.shape[0] // (tc * HM_PIECES)
    n_s = hm_s.shape[0] // (tc * HM_PIECES)
    grid_spec = pltpu.PrefetchScalarGridSpec(
        num_scalar_prefetch=3,
        grid=(n_p + n_s,),
        in_specs=[
            pl.BlockSpec((1, TOP_K, tc), lambda i, *_: (i, 0, 0), memory_space=pltpu.SMEM),
            pl.BlockSpec((tc * HM_PIECES, LANES), lambda i, *_: (jnp.minimum(i, n_p - 1), 0)),
            pl.BlockSpec((tc * HM_PIECES, LANES), lambda i, *_: (jnp.maximum(i - n_p, 0), 0)),
        ],
        out_specs=pl.BlockSpec(memory_space=pl.ANY),
        scratch_shapes=[
            pltpu.VMEM((EXPERT_TILE * HM_PIECES, LANES), U32),
            pltpu.SemaphoreType.DMA(()),
            pltpu.SemaphoreType.DMA(()),
        ],
    )
    return pl.pallas_call(
        functools.partial(_dispatch_kernel, n_s),
        out_shape=jax.ShapeDtypeStruct((n_slots * HM_PIECES, LANES), U32),
        grid_spec=grid_spec,
        compiler_params=pltpu.CompilerParams(dimension_semantics=("arbitrary",)),
        name="dispatch",
    )(last_tile, owns, n_used, dest_blocks, hm_p, hm_s)


def _pack_bf16_pairs(x):
    w = x.shape[1] // 2
    rounded = x.astype(BF16).astype(F32)
    lo = lax.bitcast_convert_type(rounded[:, :w], U32)
    hi = lax.bitcast_convert_type(rounded[:, w:], U32)
    return (hi & jnp.uint32(0xFFFF0000)) | lax.shift_right_logical(lo, jnp.uint32(16))


def _unpack_bf16_pairs(words):
    lo = lax.bitcast_convert_type(lax.shift_left(words, jnp.uint32(16)), F32)
    hi = lax.bitcast_convert_type(words & jnp.uint32(0xFFFF0000), F32)
    return lo, hi


def _expert_kernel(ord_ref, act_ref, used_ref, nact_ref, wg_ref, wu_ref, wd_ref, xb_ref, yb_ref,
                   wgf, wuf, wdf, wgb, wub, wdb, xbuf, ybuf, sem_w, sem_in, sem_out):
    tile_rows = xbuf.shape[1]
    tm = tile_rows // HM_PIECES
    n = used_ref[0]
    n_act = nact_ref[0]

    def weight_copies(k, slot):
        e = act_ref[k]
        return [pltpu.make_async_copy(src.at[e], dst.at[slot], sem_w.at[slot])
                for src, dst in ((wg_ref, wgf), (wu_ref, wuf), (wd_ref, wdf))]

    def tile(ref, t):
        return ref.at[pl.ds(pl.multiple_of(t * tile_rows, tile_rows), tile_rows), :]

    def fetch(t, slot):
        return pltpu.make_async_copy(tile(xb_ref, t), xbuf.at[slot], sem_in.at[slot])

    def put(t, slot):
        return pltpu.make_async_copy(ybuf.at[slot], tile(yb_ref, t), sem_out.at[slot])

    for cp in weight_copies(0, 0):
        cp.start()
    for ahead in range(X_SLOTS - 1):
        @pl.when(ahead < n)
        def _(ahead=ahead):
            fetch(ahead, ahead).start()

    def body(t, carry):
        k = ord_ref[t]

        @pl.when((t == 0) | (k != ord_ref[jnp.maximum(t - 1, 0)]))
        def _():
            wslot = lax.rem(k, 2)
            for cp in weight_copies(k, wslot):
                cp.wait()
            wgb[...] = wgf[wslot].astype(BF16)
            wub[...] = wuf[wslot].astype(BF16)
            wdb[...] = wdf[wslot].astype(BF16)

            @pl.when(k + 1 < n_act)
            def _():
                for cp in weight_copies(k + 1, 1 - wslot):
                    cp.start()

        slot = lax.rem(t, X_SLOTS)
        fetch(t, slot).wait()

        @pl.when(t + X_SLOTS - 1 < n)
        def _():
            fetch(t + X_SLOTS - 1, lax.rem(t + X_SLOTS - 1, X_SLOTS)).start()

        oslot = lax.rem(t, 2)

        @pl.when(t >= 2)
        def _():
            put(t - 2, oslot).wait()

        pieces = [_unpack_bf16_pairs(xbuf[slot, pl.ds(p, tm, stride=HM_PIECES), :]) for p in range(HM_PIECES)]
        xt = jnp.concatenate([lo.astype(BF16) for lo, _ in pieces] + [hi.astype(BF16) for _, hi in pieces],
                             axis=-1)
        g = jnp.dot(xt, wgb[...], preferred_element_type=F32)
        up = jnp.dot(xt, wub[...], preferred_element_type=F32)
        a = (g * jax.nn.sigmoid(g) * up).astype(BF16)
        packed = _pack_bf16_pairs(jnp.dot(a, wdb[...], preferred_element_type=F32))
        for p in range(HM_PIECES):
            ybuf[oslot, pl.ds(p, tm, stride=HM_PIECES), :] = packed[:, p * LANES:(p + 1) * LANES]
        put(t, oslot).start()
        return carry

    lax.fori_loop(0, n, body, 0)

    @pl.when(n >= 2)
    def _():
        put(n - 2, lax.rem(n, 2)).wait()

    @pl.when(n >= 1)
    def _():
        put(n - 1, lax.rem(n - 1, 2)).wait()

    ybuf[0] = jnp.zeros(ybuf.shape[1:], ybuf.dtype)
    n_all = yb_ref.shape[0] // tile_rows

    def zero_tile(t, carry):
        cp = pltpu.make_async_copy(ybuf.at[0], tile(yb_ref, t), sem_out.at[0])
        cp.start()
        cp.wait()
        return carry

    lax.fori_loop(n, n_all, zero_tile, 0)


def _expert_call(tile_ord, active, n_used, n_active, xb, w_eg, w_eu, w_ed):
    tile_rows = EXPERT_TILE * HM_PIECES
    any_spec = pl.BlockSpec(memory_space=pl.ANY)
    grid_spec = pltpu.PrefetchScalarGridSpec(
        num_scalar_prefetch=4,
        grid=(1,),
        in_specs=[any_spec] * 4,
        out_specs=any_spec,
        scratch_shapes=[
            pltpu.VMEM((2, D_MODEL, D_FF), F32),
            pltpu.VMEM((2, D_MODEL, D_FF), F32),
            pltpu.VMEM((2, D_FF, D_MODEL), F32),
            pltpu.VMEM((D_MODEL, D_FF), BF16),
            pltpu.VMEM((D_MODEL, D_FF), BF16),
            pltpu.VMEM((D_FF, D_MODEL), BF16),
            pltpu.VMEM((X_SLOTS, tile_rows, LANES), U32),
            pltpu.VMEM((2, tile_rows, LANES), U32),
            pltpu.SemaphoreType.DMA((2,)),
            pltpu.SemaphoreType.DMA((X_SLOTS,)),
            pltpu.SemaphoreType.DMA((2,)),
        ],
    )
    return pl.pallas_call(
        _expert_kernel,
        out_shape=jax.ShapeDtypeStruct(xb.shape, U32),
        grid_spec=grid_spec,
        compiler_params=pltpu.CompilerParams(
            dimension_semantics=("arbitrary",), vmem_limit_bytes=VMEM_LIMIT),
        name="experts",
    )(tile_ord, active, n_used, n_active, w_eg, w_eu, w_ed, xb)


def _gather_rows(dest_ref, yb_ref, buf_ref, sem, tokens):
    for t in tokens:
        for k in range(TOP_K):
            d = dest_ref[0, k, t]
            src = yb_ref.at[pl.ds(pl.multiple_of(d * HM_PIECES, HM_PIECES), HM_PIECES), :]
            dst = buf_ref.at[k, pl.ds(t * HM_PIECES, HM_PIECES), :]
            pltpu.make_async_copy(src, dst, sem).start(priority=k)


def _wait_rows(yb_ref, buf_ref, sem):
    for k in range(TOP_K):
        pltpu.make_async_copy(yb_ref.at[pl.ds(0, buf_ref.shape[1]), :], buf_ref.at[k], sem).wait()


def _combine_kernel(dest0_ref, dest1_ref, dest2_ref, x1_ref, gates_ref, gf_ref, yb_ref, out_ref, buf, sems):
    i = pl.program_id(0)
    tc = TOKEN_TILE
    n_chunks = tc // COMBINE_CHUNK

    @pl.when(i == 0)
    def _():
        _gather_rows(dest0_ref, yb_ref, buf.at[0], sems.at[0], range(tc))

    gf = gf_ref[...]
    for half, next_dest in ((0, dest1_ref), (1, dest2_ref)):
        cur, nxt = buf.at[half], buf.at[1 - half]
        _wait_rows(yb_ref, cur, sems.at[half])
        for c in range(n_chunks):
            rows = pl.ds(half * tc + c * COMBINE_CHUNK, COMBINE_CHUNK)
            gates = gates_ref[rows, :]
            lows, highs = [], []
            for p in range(HM_PIECES):
                piece = pl.ds(c * COMBINE_CHUNK * HM_PIECES + p, COMBINE_CHUNK, stride=HM_PIECES)
                lo0, hi0 = _unpack_bf16_pairs(cur[0, piece, :])
                lo1, hi1 = _unpack_bf16_pairs(cur[1, piece, :])
                lows.append(lo0 * gates[:, 0:1] + lo1 * gates[:, 1:2])
                highs.append(hi0 * gates[:, 0:1] + hi1 * gates[:, 1:2])
            xo = x1_ref[rows, :] + jnp.concatenate(lows + highs, axis=-1)
            _gather_rows(next_dest, yb_ref, nxt, sems.at[1 - half],
                         range(c * COMBINE_CHUNK, (c + 1) * COMBINE_CHUNK))
            ms = jnp.mean(xo * xo, axis=-1, keepdims=True)
            out_ref[rows, :] = xo * lax.rsqrt(ms + EPS) * gf

    @pl.when(i == pl.num_programs(0) - 1)
    def _():
        _wait_rows(yb_ref, buf.at[0], sems.at[0])


def _combine_call(dest_blocks, x1, gates, gf, yb, name):
    n_rows = x1.shape[0]
    tc = TOKEN_TILE
    n_tiles = n_rows // tc
    assert n_tiles % 2 == 0

    def dest_spec(index_map):
        return pl.BlockSpec((1, TOP_K, tc), index_map, memory_space=pltpu.SMEM)

    return pl.pallas_call(
        _combine_kernel,
        out_shape=jax.ShapeDtypeStruct((n_rows, D_MODEL), F32),
        grid=(n_tiles // 2,),
        in_specs=[
            dest_spec(lambda i: (0, 0, 0)),
            dest_spec(lambda i: (2 * i + 1, 0, 0)),
            dest_spec(lambda i: (jnp.minimum(2 * i + 2, n_tiles - 1), 0, 0)),
            pl.BlockSpec((2 * tc, D_MODEL), lambda i: (i, 0)),
            pl.BlockSpec((2 * tc, LANES), lambda i: (i, 0)),
            pl.BlockSpec((1, D_MODEL), lambda i: (0, 0)),
            pl.BlockSpec(memory_space=pl.ANY),
        ],
        out_specs=pl.BlockSpec((2 * tc, D_MODEL), lambda i: (i, 0)),
        scratch_shapes=[pltpu.VMEM((2, TOP_K, tc * HM_PIECES, LANES), U32), pltpu.SemaphoreType.DMA((2,))],
        compiler_params=pltpu.CompilerParams(
            dimension_semantics=("arbitrary",), vmem_limit_bytes=VMEM_LIMIT),
        name=name,
    )(dest_blocks, dest_blocks, dest_blocks, x1, gates, gf, yb)


def _block_diag2(a, b):
    z = jnp.zeros_like(a)
    return jnp.concatenate([jnp.concatenate([a, z], axis=1), jnp.concatenate([z, b], axis=1)], axis=0)


def kernel(x_prompt, x_sample, state_pool, state_conv, norm1_g, w_in, pool_lin, pool_scale, conv_dw, conv_dw_b, conv_ln_g, conv_ln_b, w_out, b_out, norm2_g, w_rg, b_rg, w_re, b_re, w_eg, w_eu, w_ed, norm_f_g):
    assert norm1_g.shape[0] == 1, "single-layer trunk"
    bsz, seq, _ = x_prompt.shape
    dbsz, dseq, _ = x_sample.shape
    past_len = 1024
    n_prompt = bsz * seq
    n_sample = dbsz * dseq
    n_tokens = n_prompt + n_sample
    assert seq % PROMPT_TILE == 0 and SAMPLE_TILE % dseq == 0 and n_sample % SAMPLE_TILE == 0
    assert all(n % t == 0 for n in (n_prompt, n_sample) for t in (TOKEN_TILE, DISPATCH_TILE))

    row = lambda a: a.reshape(1, -1)
    router_w = jnp.zeros((ROUTER_ROWS, D_MODEL), F32)
    router_w = router_w.at[0:N_EXPERT_GROUPS].set(w_rg[0].T)
    router_w = router_w.at[EXPERT_ROW0:EXPERT_ROW0 + N_EXPERTS].set(
        jnp.transpose(w_re[0], (0, 2, 1)).reshape(N_EXPERTS, D_MODEL))
    router_b = jnp.zeros((ROUTER_ROWS,), F32)
    router_b = router_b.at[0:N_EXPERT_GROUPS].set(b_rg[0])
    router_b = router_b.at[EXPERT_ROW0:EXPERT_ROW0 + N_EXPERTS].set(b_re[0].reshape(-1))
    conv_w = jnp.concatenate([conv_dw[0], jnp.zeros((1, D_CONV), F32)], axis=0)
    weights = [
        row(norm1_g[0]), w_in[0].astype(BF16),
        _block_diag2(pool_lin[0, 0], pool_lin[0, 1]).astype(BF16),
        _block_diag2(pool_lin[0, 2], pool_lin[0, 3]).astype(BF16),
        row(pool_scale[0]), conv_w, row(conv_dw_b[0]), row(conv_ln_g[0]), row(conv_ln_b[0]),
        w_out[0].astype(BF16), row(b_out[0]), row(norm2_g[0]),
        router_w.astype(BF16), router_b.reshape(ROUTER_ROWS, 1),
    ]

    zpool = jnp.zeros((bsz, PAD, D_POOL), F32)
    zconv = jnp.zeros((bsz, PAD, D_CONV), F32)
    ppool = jnp.pad(state_pool[0], ((0, 0), (PAD - POOL_STATE, 0), (0, 0)))
    pconv = jnp.pad(state_conv[0], ((0, 0), (PAD - CONV_STATE, 0), (0, 0)))

    x1_p, hm_p, gates_p, route_p, cnt_p, sp_p, sc_p = _mixer_call(
        x_prompt, zpool, zconv, 0, PROMPT_TILE, PROMPT_TILE, weights, "mixer_prompt")
    x1_s, hm_s, gates_s, route_s, cnt_s, sp_s, sc_s = _mixer_call(
        x_sample.reshape(-1, SAMPLE_TILE, D_MODEL), ppool, pconv, past_len, SAMPLE_TILE, dseq, weights,
        "mixer_sample")

    tm = EXPERT_TILE
    cnt_p = cnt_p[:, 0].astype(I32)
    cnt_s = cnt_s[:, 0].astype(I32)
    padded = (cnt_p + cnt_s + tm - 1) // tm * tm
    pad_end = jnp.cumsum(padded)
    pad_start = pad_end - padded

    def dest_blocks(route, base, tile):
        eid = jnp.stack([route[:, k, :].reshape(-1) for k in range(TOP_K)])
        rank = jnp.stack([route[:, TOP_K + k, :].reshape(-1) for k in range(TOP_K)])
        experts = jnp.arange(N_EXPERTS, dtype=I32)
        dest = rank + jnp.sum(jnp.where(eid[..., None] == experts, base, 0), axis=-1)
        return dest.reshape(TOP_K, -1, tile).transpose(1, 0, 2)

    dest_p = dest_blocks(route_p, pad_start, TOKEN_TILE)
    dest_s = dest_blocks(route_s, pad_start + cnt_p, TOKEN_TILE)
    dest_all = jnp.concatenate([dest_blocks(route_p, pad_start, DISPATCH_TILE),
                                dest_blocks(route_s, pad_start + cnt_p, DISPATCH_TILE)], axis=0)
    n_tiles = -(-(n_tokens * TOP_K + N_EXPERTS * (tm - 1)) // tm)
    n_used = (pad_end[-1:] // tm).astype(I32)

    owns = padded > 0
    xb = _dispatch_call((pad_end // tm - 1).astype(I32), owns.astype(I32), n_used,
                        dest_all, hm_p, hm_s, n_tiles * tm)
    active = jnp.argsort(jnp.logical_not(owns), stable=True).astype(I32)
    tile_expert = jnp.minimum(
        jnp.sum(pad_end[None, :] <= (jnp.arange(n_tiles, dtype=I32) * tm)[:, None], axis=1), N_EXPERTS - 1)
    ordinal = jnp.cumsum(owns.astype(I32)) - 1
    tile_ord = jnp.sum(jnp.where(tile_expert[:, None] == jnp.arange(N_EXPERTS, dtype=I32), ordinal, 0),
                       axis=1).astype(I32)
    n_active = jnp.sum(owns.astype(I32)).reshape(1)
    yb = _expert_call(tile_ord, active, n_used, n_active, xb, w_eg[0], w_eu[0], w_ed[0])
    gf = row(norm_f_g)
    y_p = _combine_call(dest_p, x1_p, gates_p, gf, yb, "combine_prompt")
    y_s = _combine_call(dest_s, x1_s, gates_s, gf, yb, "combine_sample")

    return (y_p.reshape(bsz, seq, D_MODEL), y_s.reshape(dbsz, dseq, D_MODEL),
            sp_p, sc_p, sp_s, sc_s)
```

```python
import functools

import jax
import jax.numpy as jnp
from jax import lax
from jax.experimental import pallas as pl
from jax.experimental.pallas import tpu as pltpu

F32 = jnp.float32
BF16 = jnp.bfloat16
U32 = jnp.uint32
I32 = jnp.int32

D_MODEL = 1024
D_POOL = 512
D_CONV = 512
D_IN = D_POOL + 2 * D_CONV
POOL_WINDOWS = (2, 4, 8, 16)
POOL_GROUP_DIM = D_POOL // len(POOL_WINDOWS)
POOL_STATE = max(POOL_WINDOWS) - 1
CONV_WIDTH = 31
CONV_STATE = CONV_WIDTH - 1
N_EXPERT_GROUPS = 4
EXPERTS_PER_GROUP = 8
N_EXPERTS = N_EXPERT_GROUPS * EXPERTS_PER_GROUP
TOP_K = 2
D_FF = D_MODEL // 2
EPS = 1e-6

SUBLANES = 8
LANES = 128
PAD = 32
ROUTER_ROWS = LANES
EXPERT_ROW0 = SUBLANES
CONV_CHUNK = 64
CONV_STRIDE = 2
CONV_BLOCKS = D_CONV // LANES
MIXER_PART = 256
HM_WORDS = D_MODEL // 2
HM_PIECES = HM_WORDS // LANES

PROMPT_TILE = 512
SAMPLE_TILE = 512
TOKEN_TILE = 512
DISPATCH_TILE = 1024
EXPERT_TILE = 512
X_SLOTS = 4
COMBINE_CHUNK = 32
V7X_VMEM_BYTES = 64 * 1024 * 1024
VMEM_LIMIT = V7X_VMEM_BYTES * 3 // 4


def _iota_f32(shape, axis):
    return lax.broadcasted_iota(I32, shape, axis).astype(F32)


def _mixer_kernel(pos0, tt, n_t, seg,
                  x_ref, pp_ref, cp_ref, g1_ref, win_ref, bd01_ref, bd23_ref, pscale_ref,
                  cw_ref, cb_ref, lng_ref, lnb_ref, wout_ref, bout_ref, g2_ref, wrt_ref, br_ref,
                  tri_ref,
                  x1_ref, hm_ref, gates_ref, route_ref, cnt_ref, spool_ref, sconv_ref,
                  pext, cext, hbuf, carry, lv2, lv4, lv8):
    b = pl.program_id(0)
    j = pl.program_id(1)

    @pl.when((b == 0) & (j == 0))
    def _():
        carry[...] = jnp.zeros_like(carry)

    n_seg = tt // seg
    seg_stride = PAD + seg
    assert n_seg == 1 or n_t == 1, "several segments per tile: each one is a whole sequence"

    @pl.when(j == 0)
    def _():
        for g in range(n_seg):
            pext[g * seg_stride:g * seg_stride + PAD, :] = pp_ref[g]
            for lb in range(CONV_BLOCKS):
                cext[lb, g * seg_stride:g * seg_stride + PAD, :] = cp_ref[g, :, lb * LANES:(lb + 1) * LANES]

    @pl.when(j > 0)
    def _():
        pext[0:PAD, :] = pext[tt:tt + PAD, :]
        cext[:, 0:PAD, :] = cext[:, tt:tt + PAD, :]

    n_parts = max(1, tt // MIXER_PART)
    rp = tt // n_parts
    gd = POOL_GROUP_DIM
    neg = jnp.float32(-jnp.inf)

    def runs_of(part):
        r0 = part * rp
        if n_seg == 1:
            return [(r0, rp, r0)]
        return [(t0, seg, t0 // seg * seg_stride) for t0 in range(r0, r0 + rp, seg)]

    def stage_in(part):
        r0 = part * rp
        x = x_ref[r0:r0 + rp, :]
        ms = jnp.mean(x * x, axis=-1, keepdims=True)
        h = (x * lax.rsqrt(ms + EPS) * g1_ref[...]).astype(BF16)
        z = jnp.dot(h, win_ref[...], preferred_element_type=F32)
        v = z[:, D_POOL:D_POOL + D_CONV] * jax.nn.sigmoid(z[:, D_POOL + D_CONV:])
        for t0, n, v0 in runs_of(part):
            pext[PAD + v0:PAD + v0 + n, :] = z[t0 - r0:t0 - r0 + n, :D_POOL]
            for lb in range(CONV_BLOCKS):
                cext[lb, PAD + v0:PAD + v0 + n, :] = v[t0 - r0:t0 - r0 + n, lb * LANES:(lb + 1) * LANES]

    def stage_mix(part):
        return jnp.concatenate([mix_run(*run) for run in runs_of(part)], axis=0)

    def mix_run(t0, rp, r0):
        first_pos = pos0 + (j * tt + t0 if n_seg == 1 else 0)
        pos = first_pos + lax.broadcasted_iota(I32, (rp, 1), 0)
        top = r0 + PAD
        end = top + rp
        lv2[r0 + 8:end, :] = pext[r0 + 8:end, :] + pext[r0 + 7:end - 1, :]
        lv4[r0 + 16:end, gd:] = lv2[r0 + 16:end, gd:] + lv2[r0 + 14:end - 2, gd:]
        lv8[r0 + 24:end, 2 * gd:] = lv4[r0 + 24:end, 2 * gd:] + lv4[r0 + 20:end - 4, 2 * gd:]
        sums = [lv2[top:end, 0:gd], lv4[top:end, gd:2 * gd], lv8[top:end, 2 * gd:3 * gd],
                lv8[top:end, 3 * gd:] + lv8[top - 8:end - 8, 3 * gd:]]
        u = pext[top:end, :]
        parts = []
        for g, w in enumerate(POOL_WINDOWS):
            sl = slice(g * gd, (g + 1) * gd)
            cnt = jnp.minimum(pos + 1, w).astype(F32)
            parts.append(sums[g] / cnt - u[:, sl])
        d = jnp.concatenate(parts, axis=-1).astype(BF16)
        half = D_POOL // 2
        yp = jnp.concatenate(
            [jnp.dot(d[:, :half], bd01_ref[...], preferred_element_type=F32),
             jnp.dot(d[:, half:], bd23_ref[...], preferred_element_type=F32)], axis=-1)
        yp = yp * pscale_ref[...]

        chunk = min(CONV_CHUNK, rp // CONV_STRIDE)
        for lb in range(CONV_BLOCKS):
            lanes = slice(lb * LANES, (lb + 1) * LANES)
            wts = [jnp.broadcast_to(cw_ref[k:k + 1, lanes], (chunk, LANES)) for k in range(CONV_WIDTH)]
            for first in range(r0, r0 + rp, CONV_STRIDE * chunk):
                for phase in range(CONV_STRIDE):
                    acc = None
                    for k in range(CONV_WIDTH):
                        start = first + phase + PAD - CONV_STATE + k
                        tap = cext[lb, pl.ds(start, chunk, stride=CONV_STRIDE), :] * wts[k]
                        acc = tap if acc is None else acc + tap
                    hbuf[lb, pl.ds(first + phase, chunk, stride=CONV_STRIDE), :] = acc
        return yp

    def stage_out(part, yp):
        r0 = part * rp
        x = x_ref[r0:r0 + rp, :]
        hc = jnp.concatenate(
            [jnp.concatenate([hbuf[lb, v0:v0 + n, :] for _, n, v0 in runs_of(part)], axis=0)
             for lb in range(CONV_BLOCKS)], axis=-1) + cb_ref[...]
        mu = jnp.mean(hc, axis=-1, keepdims=True)
        var = jnp.mean(jnp.square(hc - mu), axis=-1, keepdims=True)
        yln = (hc - mu) * lax.rsqrt(var + EPS) * lng_ref[...] + lnb_ref[...]
        yc = yln * jax.nn.sigmoid(yln)

        o = (jnp.dot(yp.astype(BF16), wout_ref[0:D_POOL, :], preferred_element_type=F32)
             + jnp.dot(yc.astype(BF16), wout_ref[D_POOL:, :], preferred_element_type=F32))
        x1 = x + (o + bout_ref[...])
        x1_ref[r0:r0 + rp, :] = x1

        ms2 = jnp.mean(x1 * x1, axis=-1, keepdims=True)
        hmb = (x1 * lax.rsqrt(ms2 + EPS) * g2_ref[...]).astype(BF16)
        packed = _pack_bf16_pairs(hmb.astype(F32))
        for p in range(HM_PIECES):
            hm_ref[pl.ds(r0 * HM_PIECES + p, rp, stride=HM_PIECES), :] = packed[:, p * LANES:(p + 1) * LANES]

        lt = lax.dot_general(wrt_ref[...], hmb, (((1,), (1,)), ((), ())),
                             preferred_element_type=F32) + br_ref[...]
        row8 = _iota_f32((SUBLANES, rp), 0)
        lg = jnp.where(row8 < N_EXPERT_GROUPS, lt[0:SUBLANES, :], neg)
        mg = jnp.max(lg, axis=0, keepdims=True)
        p_grp = 1.0 / jnp.sum(jnp.exp(lg - mg), axis=0, keepdims=True)
        gsel = jnp.min(jnp.where(lg == mg, row8, float(SUBLANES)), axis=0, keepdims=True)
        le = jnp.zeros((SUBLANES, rp), F32)
        for g in range(N_EXPERT_GROUPS):
            e0 = EXPERT_ROW0 + g * EXPERTS_PER_GROUP
            le = jnp.where(gsel == float(g), lt[e0:e0 + EXPERTS_PER_GROUP, :], le)
        m1 = jnp.max(le, axis=0, keepdims=True)
        i1 = jnp.min(jnp.where(le == m1, row8, float(SUBLANES)), axis=0, keepdims=True)
        le2 = jnp.where(row8 == i1, neg, le)
        m2 = jnp.max(le2, axis=0, keepdims=True)
        i2 = jnp.min(jnp.where(le2 == m2, row8, float(SUBLANES)), axis=0, keepdims=True)
        e2 = jnp.exp(m2 - m1)
        den = 1.0 + e2
        gate0 = p_grp * (1.0 / den)
        gate1 = p_grp * (e2 / den)
        eid0 = gsel * float(EXPERTS_PER_GROUP) + i1
        eid1 = gsel * float(EXPERTS_PER_GROUP) + i2

        rowe = _iota_f32((N_EXPERTS, rp), 0)
        oh0 = rowe == eid0
        oh1 = rowe == eid1
        ohf = jnp.where(jnp.logical_or(oh0, oh1), 1.0, 0.0)
        before = jnp.dot(ohf.astype(BF16), tri_ref[0:rp, 0:rp], preferred_element_type=F32) + carry[:, 0:1]
        rank0 = jnp.sum(jnp.where(oh0, before, 0.0), axis=0, keepdims=True)
        rank1 = jnp.sum(jnp.where(oh1, before, 0.0), axis=0, keepdims=True)
        carry[...] = carry[...] + jnp.sum(ohf, axis=1, keepdims=True)

        route = jnp.where(row8 == 0.0, eid0,
                          jnp.where(row8 == 1.0, eid1,
                                    jnp.where(row8 == 2.0, rank0,
                                              jnp.where(row8 == 3.0, rank1, 0.0))))
        route_ref[:, r0:r0 + rp] = route.astype(I32)

        tpad = -(-rp // LANES) * LANES
        rowr = _iota_f32((ROUTER_ROWS, rp), 0)
        gt = jnp.where(rowr == 0.0, gate0, jnp.where(rowr == 1.0, gate1, 0.0))
        if tpad != rp:
            gt = jnp.concatenate([gt, jnp.zeros((ROUTER_ROWS, tpad - rp), F32)], axis=1)
        gates_ref[r0:r0 + rp, :] = gt.T[0:rp, :]

    stage_in(0)
    for part in range(n_parts):
        yp = stage_mix(part)
        if part + 1 < n_parts:
            stage_in(part + 1)
        stage_out(part, yp)
    cnt_ref[...] = carry[...]

    @pl.when(j == n_t - 1)
    def _():
        for g in range(n_seg):
            last = g * seg_stride + PAD + seg
            spool_ref[g] = pext[last - POOL_STATE:last, :]
            sconv_ref[g] = jnp.concatenate(
                [cext[lb, last - CONV_STATE:last, :] for lb in range(CONV_BLOCKS)], axis=-1)


def _mixer_call(x, pool_prefix, conv_prefix, pos0, tt, seg, weights, name):
    bsz, seq, _ = x.shape
    n_t = seq // tt
    n_seg = tt // seg
    ext_rows = n_seg * (PAD + seg)
    n_blk = bsz * n_t
    n_rows = bsz * seq
    tri = (lax.broadcasted_iota(I32, (tt, tt), 0) < lax.broadcasted_iota(I32, (tt, tt), 1)).astype(BF16)

    def const(shape):
        return pl.BlockSpec(shape, lambda b, j: (0,) * len(shape))

    def rows(width):
        return pl.BlockSpec((tt, width), lambda b, j: (b * n_t + j, 0))

    in_specs = [
        pl.BlockSpec((None, tt, D_MODEL), lambda b, j: (b, j, 0)),
        pl.BlockSpec((n_seg, PAD, D_POOL), lambda b, j: (b, 0, 0)),
        pl.BlockSpec((n_seg, PAD, D_CONV), lambda b, j: (b, 0, 0)),
    ] + [const(w.shape) for w in weights] + [const((tt, tt))]
    out_shape = (
        jax.ShapeDtypeStruct((n_rows, D_MODEL), F32),
        jax.ShapeDtypeStruct((n_rows * HM_PIECES, LANES), U32),
        jax.ShapeDtypeStruct((n_rows, LANES), F32),
        jax.ShapeDtypeStruct((n_blk, SUBLANES, tt), I32),
        jax.ShapeDtypeStruct((N_EXPERTS, LANES), F32),
        jax.ShapeDtypeStruct((1, bsz * n_seg, POOL_STATE, D_POOL), F32),
        jax.ShapeDtypeStruct((1, bsz * n_seg, CONV_STATE, D_CONV), F32),
    )
    out_specs = (
        rows(D_MODEL),
        pl.BlockSpec((tt * HM_PIECES, LANES), lambda b, j: (b * n_t + j, 0)),
        rows(LANES),
        pl.BlockSpec((None, SUBLANES, tt), lambda b, j: (b * n_t + j, 0, 0)),
        pl.BlockSpec((N_EXPERTS, LANES), lambda b, j: (0, 0)),
        pl.BlockSpec((None, n_seg, POOL_STATE, D_POOL), lambda b, j: (0, b, 0, 0)),
        pl.BlockSpec((None, n_seg, CONV_STATE, D_CONV), lambda b, j: (0, b, 0, 0)),
    )
    return pl.pallas_call(
        functools.partial(_mixer_kernel, pos0, tt, n_t, seg),
        out_shape=out_shape,
        grid=(bsz, n_t),
        in_specs=in_specs,
        out_specs=out_specs,
        scratch_shapes=[
            pltpu.VMEM((ext_rows, D_POOL), F32),
            pltpu.VMEM((CONV_BLOCKS, ext_rows, LANES), F32),
            pltpu.VMEM((CONV_BLOCKS, ext_rows, LANES), F32),
            pltpu.VMEM((N_EXPERTS, LANES), F32),
            pltpu.VMEM((ext_rows, D_POOL), F32),
            pltpu.VMEM((ext_rows, D_POOL), F32),
            pltpu.VMEM((ext_rows, D_POOL), F32),
        ],
        compiler_params=pltpu.CompilerParams(
            dimension_semantics=("arbitrary", "arbitrary"), vmem_limit_bytes=VMEM_LIMIT),
        name=name,
    )(x, pool_prefix, conv_prefix, *weights, tri)


def _dispatch_kernel(n_sample_tiles, last_ref, owns_ref, used_ref, dest_ref, hmp_ref, hms_ref, xb_ref,
                     zbuf, sem, zsem):
    i = pl.program_id(0)
    tc = dest_ref.shape[2]
    n_prompt_tiles = pl.num_programs(0) - n_sample_tiles
    tile_rows = zbuf.shape[0]
    n_all = xb_ref.shape[0] // tile_rows

    def zero_copy(t):
        dst = xb_ref.at[pl.ds(pl.multiple_of(t * tile_rows, tile_rows), tile_rows), :]
        return pltpu.make_async_copy(zbuf, dst, zsem)

    @pl.when(i == 0)
    def _():
        zbuf[...] = jnp.zeros_like(zbuf)
        for e in range(N_EXPERTS):
            @pl.when(owns_ref[e] > 0)
            def _(e=e):
                zero_copy(last_ref[e]).start()
        lax.fori_loop(used_ref[0], n_all, lambda t, c: (zero_copy(t).start(), c)[1], 0)
        for e in range(N_EXPERTS):
            @pl.when(owns_ref[e] > 0)
            def _(e=e):
                zero_copy(last_ref[e]).wait()
        lax.fori_loop(used_ref[0], n_all, lambda t, c: (zero_copy(t).wait(), c)[1], 0)

    def scatter(hm_ref):
        for t in range(tc):
            for k in range(TOP_K):
                d = dest_ref[0, k, t]
                src = hm_ref.at[pl.ds(t * HM_PIECES, HM_PIECES), :]
                dst = xb_ref.at[pl.ds(pl.multiple_of(d * HM_PIECES, HM_PIECES), HM_PIECES), :]
                pltpu.make_async_copy(src, dst, sem).start(priority=k)
        for k in range(TOP_K):
            pltpu.make_async_copy(hm_ref, xb_ref.at[pl.ds(0, tc * HM_PIECES), :], sem).wait()

    @pl.when(i < n_prompt_tiles)
    def _():
        scatter(hmp_ref)

    @pl.when(i >= n_prompt_tiles)
    def _():
        scatter(hms_ref)


def _dispatch_call(last_tile, owns, n_used, dest_blocks, hm_p, hm_s, n_slots):
    tc = DISPATCH_TILE
    n_p = hm_p.shape[0] // (tc * HM_PIECES)
    n_s = hm_s.shape[0] // (tc * HM_PIECES)
    grid_spec = pltpu.PrefetchScalarGridSpec(
        num_scalar_prefetch=3,
        grid=(n_p + n_s,),
        in_specs=[
            pl.BlockSpec((1, TOP_K, tc), lambda i, *_: (i, 0, 0), memory_space=pltpu.SMEM),
            pl.BlockSpec((tc * HM_PIECES, LANES), lambda i, *_: (jnp.minimum(i, n_p - 1), 0)),
            pl.BlockSpec((tc * HM_PIECES, LANES), lambda i, *_: (jnp.maximum(i - n_p, 0), 0)),
        ],
        out_specs=pl.BlockSpec(memory_space=pl.ANY),
        scratch_shapes=[
            pltpu.VMEM((EXPERT_TILE * HM_PIECES, LANES), U32),
            pltpu.SemaphoreType.DMA(()),
            pltpu.SemaphoreType.DMA(()),
        ],
    )
    return pl.pallas_call(
        functools.partial(_dispatch_kernel, n_s),
        out_shape=jax.ShapeDtypeStruct((n_slots * HM_PIECES, LANES), U32),
        grid_spec=grid_spec,
        compiler_params=pltpu.CompilerParams(dimension_semantics=("arbitrary",)),
        name="dispatch",
    )(last_tile, owns, n_used, dest_blocks, hm_p, hm_s)


def _pack_bf16_pairs(x):
    w = x.shape[1] // 2
    rounded = x.astype(BF16).astype(F32)
    lo = lax.bitcast_convert_type(rounded[:, :w], U32)
    hi = lax.bitcast_convert_type(rounded[:, w:], U32)
    return (hi & jnp.uint32(0xFFFF0000)) | lax.shift_right_logical(lo, jnp.uint32(16))


def _unpack_bf16_pairs(words):
    lo = lax.bitcast_convert_type(lax.shift_left(words, jnp.uint32(16)), F32)
    hi = lax.bitcast_convert_type(words & jnp.uint32(0xFFFF0000), F32)
    return lo, hi


def _expert_kernel(ord_ref, act_ref, used_ref, nact_ref, short_ref, wg_ref, wu_ref, wd_ref, xb_ref, yb_ref,
                   wgf, wuf, wdf, wgb, wub, wdb, xbuf, ybuf, sem_w, sem_in, sem_out):
    tile_rows = xbuf.shape[1]
    tm = tile_rows // HM_PIECES
    n = used_ref[0]
    n_act = nact_ref[0]

    def weight_copies(k, slot):
        e = act_ref[k]
        return [pltpu.make_async_copy(src.at[e], dst.at[slot], sem_w.at[slot])
                for src, dst in ((wg_ref, wgf), (wu_ref, wuf), (wd_ref, wdf))]

    def tile(ref, t):
        return ref.at[pl.ds(pl.multiple_of(t * tile_rows, tile_rows), tile_rows), :]

    def fetch(t, slot):
        return pltpu.make_async_copy(tile(xb_ref, t), xbuf.at[slot], sem_in.at[slot])

    def put(t, slot):
        return pltpu.make_async_copy(ybuf.at[slot], tile(yb_ref, t), sem_out.at[slot])

    for cp in weight_copies(0, 0):
        cp.start()
    for ahead in range(X_SLOTS - 1):
        @pl.when(ahead < n)
        def _(ahead=ahead):
            fetch(ahead, ahead).start()

    def body(t, carry):
        k = ord_ref[t]

        @pl.when((t == 0) | (k != ord_ref[jnp.maximum(t - 1, 0)]))
        def _():
            wslot = lax.rem(k, 2)
            for cp in weight_copies(k, wslot):
                cp.wait()
            wgb[...] = wgf[wslot].astype(BF16)
            wub[...] = wuf[wslot].astype(BF16)
            wdb[...] = wdf[wslot].astype(BF16)

            @pl.when(k + 1 < n_act)
            def _():
                for cp in weight_copies(k + 1, 1 - wslot):
                    cp.start()

        slot = lax.rem(t, X_SLOTS)
        fetch(t, slot).wait()

        @pl.when(t + X_SLOTS - 1 < n)
        def _():
            fetch(t + X_SLOTS - 1, lax.rem(t + X_SLOTS - 1, X_SLOTS)).start()

        oslot = lax.rem(t, 2)

        @pl.when(t >= 2)
        def _():
            put(t - 2, oslot).wait()

        def mlp(rows):
            pieces = [_unpack_bf16_pairs(xbuf[slot, pl.ds(p, rows, stride=HM_PIECES), :])
                      for p in range(HM_PIECES)]
            xt = jnp.concatenate([lo.astype(BF16) for lo, _ in pieces] + [hi.astype(BF16) for _, hi in pieces],
                                 axis=-1)
            g = jnp.dot(xt, wgb[...], preferred_element_type=F32)
            up = jnp.dot(xt, wub[...], preferred_element_type=F32)
            a = (g * jax.nn.sigmoid(g) * up).astype(BF16)
            packed = _pack_bf16_pairs(jnp.dot(a, wdb[...], preferred_element_type=F32))
            for p in range(HM_PIECES):
                ybuf[oslot, pl.ds(p, rows, stride=HM_PIECES), :] = packed[:, p * LANES:(p + 1) * LANES]
            if rows < tm:
                ybuf[oslot, rows * HM_PIECES:, :] = jnp.zeros(((tm - rows) * HM_PIECES, LANES), ybuf.dtype)

        @pl.when(short_ref[t] == 0)
        def _():
            mlp(tm)

        @pl.when(short_ref[t] != 0)
        def _():
            mlp(tm // 2)

        put(t, oslot).start()
        return carry

    lax.fori_loop(0, n, body, 0)

    @pl.when(n >= 2)
    def _():
        put(n - 2, lax.rem(n, 2)).wait()

    @pl.when(n >= 1)
    def _():
        put(n - 1, lax.rem(n - 1, 2)).wait()

    ybuf[0] = jnp.zeros(ybuf.shape[1:], ybuf.dtype)
    n_all = yb_ref.shape[0] // tile_rows

    def zero_tile(t, carry):
        cp = pltpu.make_async_copy(ybuf.at[0], tile(yb_ref, t), sem_out.at[0])
        cp.start()
        cp.wait()
        return carry

    lax.fori_loop(n, n_all, zero_tile, 0)


def _expert_call(tile_ord, active, n_used, n_active, tile_short, xb, w_eg, w_eu, w_ed):
    tile_rows = EXPERT_TILE * HM_PIECES
    any_spec = pl.BlockSpec(memory_space=pl.ANY)
    grid_spec = pltpu.PrefetchScalarGridSpec(
        num_scalar_prefetch=5,
        grid=(1,),
        in_specs=[any_spec] * 4,
        out_specs=any_spec,
        scratch_shapes=[
            pltpu.VMEM((2, D_MODEL, D_FF), F32),
            pltpu.VMEM((2, D_MODEL, D_FF), F32),
            pltpu.VMEM((2, D_FF, D_MODEL), F32),
            pltpu.VMEM((D_MODEL, D_FF), BF16),
            pltpu.VMEM((D_MODEL, D_FF), BF16),
            pltpu.VMEM((D_FF, D_MODEL), BF16),
            pltpu.VMEM((X_SLOTS, tile_rows, LANES), U32),
            pltpu.VMEM((2, tile_rows, LANES), U32),
            pltpu.SemaphoreType.DMA((2,)),
            pltpu.SemaphoreType.DMA((X_SLOTS,)),
            pltpu.SemaphoreType.DMA((2,)),
        ],
    )
    return pl.pallas_call(
        _expert_kernel,
        out_shape=jax.ShapeDtypeStruct(xb.shape, U32),
        grid_spec=grid_spec,
        compiler_params=pltpu.CompilerParams(
            dimension_semantics=("arbitrary",), vmem_limit_bytes=VMEM_LIMIT),
        name="experts",
    )(tile_ord, active, n_used, n_active, tile_short, w_eg, w_eu, w_ed, xb)


def _gather_rows(dest_ref, yb_ref, buf_ref, sem, tokens):
    for t in tokens:
        for k in range(TOP_K):
            d = dest_ref[0, k, t]
            src = yb_ref.at[pl.ds(pl.multiple_of(d * HM_PIECES, HM_PIECES), HM_PIECES), :]
            dst = buf_ref.at[k, pl.ds(t * HM_PIECES, HM_PIECES), :]
            pltpu.make_async_copy(src, dst, sem).start(priority=k)


def _wait_rows(yb_ref, buf_ref, sem):
    for k in range(TOP_K):
        pltpu.make_async_copy(yb_ref.at[pl.ds(0, buf_ref.shape[1]), :], buf_ref.at[k], sem).wait()


def _combine_kernel(dest0_ref, dest1_ref, dest2_ref, x1_ref, gates_ref, gf_ref, yb_ref, out_ref, buf, sems):
    i = pl.program_id(0)
    tc = TOKEN_TILE
    n_chunks = tc // COMBINE_CHUNK

    @pl.when(i == 0)
    def _():
        _gather_rows(dest0_ref, yb_ref, buf.at[0], sems.at[0], range(tc))

    gf = gf_ref[...]
    for half, next_dest in ((0, dest1_ref), (1, dest2_ref)):
        cur, nxt = buf.at[half], buf.at[1 - half]
        _wait_rows(yb_ref, cur, sems.at[half])
        for c in range(n_chunks):
            rows = pl.ds(half * tc + c * COMBINE_CHUNK, COMBINE_CHUNK)
            gates = gates_ref[rows, :]
            lows, highs = [], []
            for p in range(HM_PIECES):
                piece = pl.ds(c * COMBINE_CHUNK * HM_PIECES + p, COMBINE_CHUNK, stride=HM_PIECES)
                lo0, hi0 = _unpack_bf16_pairs(cur[0, piece, :])
                lo1, hi1 = _unpack_bf16_pairs(cur[1, piece, :])
                lows.append(lo0 * gates[:, 0:1] + lo1 * gates[:, 1:2])
                highs.append(hi0 * gates[:, 0:1] + hi1 * gates[:, 1:2])
            xo = x1_ref[rows, :] + jnp.concatenate(lows + highs, axis=-1)
            _gather_rows(next_dest, yb_ref, nxt, sems.at[1 - half],
                         range(c * COMBINE_CHUNK, (c + 1) * COMBINE_CHUNK))
            ms = jnp.mean(xo * xo, axis=-1, keepdims=True)
            out_ref[rows, :] = xo * lax.rsqrt(ms + EPS) * gf

    @pl.when(i == pl.num_programs(0) - 1)
    def _():
        _wait_rows(yb_ref, buf.at[0], sems.at[0])


def _combine_call(dest_blocks, x1, gates, gf, yb, name):
    n_rows = x1.shape[0]
    tc = TOKEN_TILE
    n_tiles = n_rows // tc
    assert n_tiles % 2 == 0

    def dest_spec(index_map):
        return pl.BlockSpec((1, TOP_K, tc), index_map, memory_space=pltpu.SMEM)

    return pl.pallas_call(
        _combine_kernel,
        out_shape=jax.ShapeDtypeStruct((n_rows, D_MODEL), F32),
        grid=(n_tiles // 2,),
        in_specs=[
            dest_spec(lambda i: (0, 0, 0)),
            dest_spec(lambda i: (2 * i + 1, 0, 0)),
            dest_spec(lambda i: (jnp.minimum(2 * i + 2, n_tiles - 1), 0, 0)),
            pl.BlockSpec((2 * tc, D_MODEL), lambda i: (i, 0)),
            pl.BlockSpec((2 * tc, LANES), lambda i: (i, 0)),
            pl.BlockSpec((1, D_MODEL), lambda i: (0, 0)),
            pl.BlockSpec(memory_space=pl.ANY),
        ],
        out_specs=pl.BlockSpec((2 * tc, D_MODEL), lambda i: (i, 0)),
        scratch_shapes=[pltpu.VMEM((2, TOP_K, tc * HM_PIECES, LANES), U32), pltpu.SemaphoreType.DMA((2,))],
        compiler_params=pltpu.CompilerParams(
            dimension_semantics=("arbitrary",), vmem_limit_bytes=VMEM_LIMIT),
        name=name,
    )(dest_blocks, dest_blocks, dest_blocks, x1, gates, gf, yb)


def _block_diag2(a, b):
    z = jnp.zeros_like(a)
    return jnp.concatenate([jnp.concatenate([a, z], axis=1), jnp.concatenate([z, b], axis=1)], axis=0)


def kernel(x_prompt, x_sample, state_pool, state_conv, norm1_g, w_in, pool_lin, pool_scale, conv_dw, conv_dw_b, conv_ln_g, conv_ln_b, w_out, b_out, norm2_g, w_rg, b_rg, w_re, b_re, w_eg, w_eu, w_ed, norm_f_g):
    assert norm1_g.shape[0] == 1, "single-layer trunk"
    bsz, seq, _ = x_prompt.shape
    dbsz, dseq, _ = x_sample.shape
    past_len = 1024
    n_prompt = bsz * seq
    n_sample = dbsz * dseq
    n_tokens = n_prompt + n_sample
    assert seq % PROMPT_TILE == 0 and SAMPLE_TILE % dseq == 0 and n_sample % SAMPLE_TILE == 0
    assert all(n % t == 0 for n in (n_prompt, n_sample) for t in (TOKEN_TILE, DISPATCH_TILE))

    row = lambda a: a.reshape(1, -1)
    router_w = jnp.zeros((ROUTER_ROWS, D_MODEL), F32)
    router_w = router_w.at[0:N_EXPERT_GROUPS].set(w_rg[0].T)
    router_w = router_w.at[EXPERT_ROW0:EXPERT_ROW0 + N_EXPERTS].set(
        jnp.transpose(w_re[0], (0, 2, 1)).reshape(N_EXPERTS, D_MODEL))
    router_b = jnp.zeros((ROUTER_ROWS,), F32)
    router_b = router_b.at[0:N_EXPERT_GROUPS].set(b_rg[0])
    router_b = router_b.at[EXPERT_ROW0:EXPERT_ROW0 + N_EXPERTS].set(b_re[0].reshape(-1))
    conv_w = jnp.concatenate([conv_dw[0], jnp.zeros((1, D_CONV), F32)], axis=0)
    weights = [
        row(norm1_g[0]), w_in[0].astype(BF16),
        _block_diag2(pool_lin[0, 0], pool_lin[0, 1]).astype(BF16),
        _block_diag2(pool_lin[0, 2], pool_lin[0, 3]).astype(BF16),
        row(pool_scale[0]), conv_w, row(conv_dw_b[0]), row(conv_ln_g[0]), row(conv_ln_b[0]),
        w_out[0].astype(BF16), row(b_out[0]), row(norm2_g[0]),
        router_w.astype(BF16), router_b.reshape(ROUTER_ROWS, 1),
    ]

    zpool = jnp.zeros((bsz, PAD, D_POOL), F32)
    zconv = jnp.zeros((bsz, PAD, D_CONV), F32)
    ppool = jnp.pad(state_pool[0], ((0, 0), (PAD - POOL_STATE, 0), (0, 0)))
    pconv = jnp.pad(state_conv[0], ((0, 0), (PAD - CONV_STATE, 0), (0, 0)))

    x1_p, hm_p, gates_p, route_p, cnt_p, sp_p, sc_p = _mixer_call(
        x_prompt, zpool, zconv, 0, PROMPT_TILE, PROMPT_TILE, weights, "mixer_prompt")
    x1_s, hm_s, gates_s, route_s, cnt_s, sp_s, sc_s = _mixer_call(
        x_sample.reshape(-1, SAMPLE_TILE, D_MODEL), ppool, pconv, past_len, SAMPLE_TILE, dseq, weights,
        "mixer_sample")

    tm = EXPERT_TILE
    cnt_p = cnt_p[:, 0].astype(I32)
    cnt_s = cnt_s[:, 0].astype(I32)
    padded = (cnt_p + cnt_s + tm - 1) // tm * tm
    pad_end = jnp.cumsum(padded)
    pad_start = pad_end - padded

    def dest_blocks(route, base, tile):
        eid = jnp.stack([route[:, k, :].reshape(-1) for k in range(TOP_K)])
        rank = jnp.stack([route[:, TOP_K + k, :].reshape(-1) for k in range(TOP_K)])
        experts = jnp.arange(N_EXPERTS, dtype=I32)
        dest = rank + jnp.sum(jnp.where(eid[..., None] == experts, base, 0), axis=-1)
        return dest.reshape(TOP_K, -1, tile).transpose(1, 0, 2)

    dest_p = dest_blocks(route_p, pad_start, TOKEN_TILE)
    dest_s = dest_blocks(route_s, pad_start + cnt_p, TOKEN_TILE)
    dest_all = jnp.concatenate([dest_blocks(route_p, pad_start, DISPATCH_TILE),
                                dest_blocks(route_s, pad_start + cnt_p, DISPATCH_TILE)], axis=0)
    n_tiles = -(-(n_tokens * TOP_K + N_EXPERTS * (tm - 1)) // tm)
    n_used = (pad_end[-1:] // tm).astype(I32)

    owns = padded > 0
    xb = _dispatch_call((pad_end // tm - 1).astype(I32), owns.astype(I32), n_used,
                        dest_all, hm_p, hm_s, n_tiles * tm)
    active = jnp.argsort(jnp.logical_not(owns), stable=True).astype(I32)
    tile_expert = jnp.minimum(
        jnp.sum(pad_end[None, :] <= (jnp.arange(n_tiles, dtype=I32) * tm)[:, None], axis=1), N_EXPERTS - 1)
    ordinal = jnp.cumsum(owns.astype(I32)) - 1
    tile_ord = jnp.sum(jnp.where(tile_expert[:, None] == jnp.arange(N_EXPERTS, dtype=I32), ordinal, 0),
                       axis=1).astype(I32)
    n_active = jnp.sum(owns.astype(I32)).reshape(1)
    real_end = jnp.sum(jnp.where(tile_expert[:, None] == jnp.arange(N_EXPERTS, dtype=I32),
                                 pad_start + cnt_p + cnt_s, 0), axis=1)
    tile_short = (real_end - jnp.arange(n_tiles, dtype=I32) * tm <= tm // 2).astype(I32)
    yb = _expert_call(tile_ord, active, n_used, n_active, tile_short, xb, w_eg[0], w_eu[0], w_ed[0])
    gf = row(norm_f_g)
    y_p = _combine_call(dest_p, x1_p, gates_p, gf, yb, "combine_prompt")
    y_s = _combine_call(dest_s, x1_s, gates_s, gf, yb, "combine_sample")

    return (y_p.reshape(bsz, seq, D_MODEL), y_s.reshape(dbsz, dseq, D_MODEL),
            sp_p, sc_p, sp_s, sc_s)
```

```python
import functools

import jax
import jax.numpy as jnp
from jax import lax
from jax.experimental import pallas as pl
from jax.experimental.pallas import tpu as pltpu

F32 = jnp.float32
BF16 = jnp.bfloat16
U32 = jnp.uint32
I32 = jnp.int32

D_MODEL = 1024
D_POOL = 512
D_CONV = 512
D_IN = D_POOL + 2 * D_CONV
POOL_WINDOWS = (2, 4, 8, 16)
POOL_GROUP_DIM = D_POOL // len(POOL_WINDOWS)
POOL_STATE = max(POOL_WINDOWS) - 1
CONV_WIDTH = 31
CONV_STATE = CONV_WIDTH - 1
N_EXPERT_GROUPS = 4
EXPERTS_PER_GROUP = 8
N_EXPERTS = N_EXPERT_GROUPS * EXPERTS_PER_GROUP
TOP_K = 2
D_FF = D_MODEL // 2
EPS = 1e-6

SUBLANES = 8
LANES = 128
PAD = 32
ROUTER_ROWS = LANES
EXPERT_ROW0 = SUBLANES
CONV_CHUNK = 64
CONV_STRIDE = 2
CONV_BLOCKS = D_CONV // LANES
MIXER_PART = 256
HM_WORDS = D_MODEL // 2
HM_PIECES = HM_WORDS // LANES

PROMPT_TILE = 512
SAMPLE_TILE = 512
TOKEN_TILE = 512
DISPATCH_TILE = 1024
EXPERT_TILE = 512
X_SLOTS = 4
COMBINE_CHUNK = 32
V7X_VMEM_BYTES = 64 * 1024 * 1024
VMEM_LIMIT = V7X_VMEM_BYTES * 3 // 4


def _iota_f32(shape, axis):
    return lax.broadcasted_iota(I32, shape, axis).astype(F32)


def _mixer_kernel(pos0, tt, n_t, seg,
                  x_ref, pp_ref, cp_ref, g1_ref, win_ref, bd01_ref, bd23_ref, pscale_ref,
                  cw_ref, cb_ref, lng_ref, lnb_ref, wout_ref, bout_ref, g2_ref, wrt_ref, br_ref,
                  tri_ref,
                  x1_ref, hm_ref, gates_ref, route_ref, cnt_ref, spool_ref, sconv_ref,
                  pext, cext, hbuf, carry, lv2, lv4, lv8):
    b = pl.program_id(0)
    j = pl.program_id(1)

    @pl.when((b == 0) & (j == 0))
    def _():
        carry[...] = jnp.zeros_like(carry)

    n_seg = tt // seg
    seg_stride = PAD + seg
    assert n_seg == 1 or n_t == 1, "several segments per tile: each one is a whole sequence"

    @pl.when(j == 0)
    def _():
        for g in range(n_seg):
            pext[g * seg_stride:g * seg_stride + PAD, :] = pp_ref[g]
            for lb in range(CONV_BLOCKS):
                cext[lb, g * seg_stride:g * seg_stride + PAD, :] = cp_ref[g, :, lb * LANES:(lb + 1) * LANES]

    @pl.when(j > 0)
    def _():
        pext[0:PAD, :] = pext[tt:tt + PAD, :]
        cext[:, 0:PAD, :] = cext[:, tt:tt + PAD, :]

    n_parts = max(1, tt // MIXER_PART)
    rp = tt // n_parts
    gd = POOL_GROUP_DIM
    neg = jnp.float32(-jnp.inf)

    def runs_of(part):
        r0 = part * rp
        if n_seg == 1:
            return [(r0, rp, r0)]
        return [(t0, seg, t0 // seg * seg_stride) for t0 in range(r0, r0 + rp, seg)]

    def stage_in(part):
        r0 = part * rp
        x = x_ref[r0:r0 + rp, :]
        ms = jnp.mean(x * x, axis=-1, keepdims=True)
        h = (x * lax.rsqrt(ms + EPS) * g1_ref[...]).astype(BF16)
        z = jnp.dot(h, win_ref[...], preferred_element_type=F32)
        v = z[:, D_POOL:D_POOL + D_CONV] * jax.nn.sigmoid(z[:, D_POOL + D_CONV:])
        for t0, n, v0 in runs_of(part):
            pext[PAD + v0:PAD + v0 + n, :] = z[t0 - r0:t0 - r0 + n, :D_POOL]
            for lb in range(CONV_BLOCKS):
                cext[lb, PAD + v0:PAD + v0 + n, :] = v[t0 - r0:t0 - r0 + n, lb * LANES:(lb + 1) * LANES]

    def stage_mix(part):
        return jnp.concatenate([mix_run(*run) for run in runs_of(part)], axis=0)

    def mix_run(t0, rp, r0):
        first_pos = pos0 + (j * tt + t0 if n_seg == 1 else 0)
        pos = first_pos + lax.broadcasted_iota(I32, (rp, 1), 0)
        all_full = pos0 + (t0 if n_seg == 1 else 0) >= POOL_STATE
        top = r0 + PAD
        end = top + rp
        lv2[r0 + 8:end, :] = pext[r0 + 8:end, :] + pext[r0 + 7:end - 1, :]
        lv4[r0 + 16:end, gd:] = lv2[r0 + 16:end, gd:] + lv2[r0 + 14:end - 2, gd:]
        lv8[r0 + 24:end, 2 * gd:] = lv4[r0 + 24:end, 2 * gd:] + lv4[r0 + 20:end - 4, 2 * gd:]
        sums = [lv2[top:end, 0:gd], lv4[top:end, gd:2 * gd], lv8[top:end, 2 * gd:3 * gd],
                lv8[top:end, 3 * gd:] + lv8[top - 8:end - 8, 3 * gd:]]
        u = pext[top:end, :]
        parts = []
        for g, w in enumerate(POOL_WINDOWS):
            sl = slice(g * gd, (g + 1) * gd)
            cnt = float(w) if all_full else jnp.minimum(pos + 1, w).astype(F32)
            parts.append(sums[g] / cnt - u[:, sl])
        d = jnp.concatenate(parts, axis=-1).astype(BF16)
        half = D_POOL // 2
        yp = jnp.concatenate(
            [jnp.dot(d[:, :half], bd01_ref[...], preferred_element_type=F32),
             jnp.dot(d[:, half:], bd23_ref[...], preferred_element_type=F32)], axis=-1)
        yp = yp * pscale_ref[...]

        chunk = min(CONV_CHUNK, rp // CONV_STRIDE)
        for lb in range(CONV_BLOCKS):
            lanes = slice(lb * LANES, (lb + 1) * LANES)
            wts = [jnp.broadcast_to(cw_ref[k:k + 1, lanes], (chunk, LANES)) for k in range(CONV_WIDTH)]
            for first in range(r0, r0 + rp, CONV_STRIDE * chunk):
                for phase in range(CONV_STRIDE):
                    acc = None
                    for k in range(CONV_WIDTH):
                        start = first + phase + PAD - CONV_STATE + k
                        tap = cext[lb, pl.ds(start, chunk, stride=CONV_STRIDE), :] * wts[k]
                        acc = tap if acc is None else acc + tap
                    hbuf[lb, pl.ds(first + phase, chunk, stride=CONV_STRIDE), :] = acc
        return yp

    def stage_out(part, yp):
        r0 = part * rp
        x = x_ref[r0:r0 + rp, :]
        hc = jnp.concatenate(
            [jnp.concatenate([hbuf[lb, v0:v0 + n, :] for _, n, v0 in runs_of(part)], axis=0)
             for lb in range(CONV_BLOCKS)], axis=-1) + cb_ref[...]
        mu = jnp.mean(hc, axis=-1, keepdims=True)
        var = jnp.mean(jnp.square(hc - mu), axis=-1, keepdims=True)
        yln = (hc - mu) * lax.rsqrt(var + EPS) * lng_ref[...] + lnb_ref[...]
        yc = yln * jax.nn.sigmoid(yln)

        o = (jnp.dot(yp.astype(BF16), wout_ref[0:D_POOL, :], preferred_element_type=F32)
             + jnp.dot(yc.astype(BF16), wout_ref[D_POOL:, :], preferred_element_type=F32))
        x1 = x + (o + bout_ref[...])
        x1_ref[r0:r0 + rp, :] = x1

        ms2 = jnp.mean(x1 * x1, axis=-1, keepdims=True)
        hmb = (x1 * lax.rsqrt(ms2 + EPS) * g2_ref[...]).astype(BF16)
        packed = _pack_bf16_pairs(hmb.astype(F32))
        for p in range(HM_PIECES):
            hm_ref[pl.ds(r0 * HM_PIECES + p, rp, stride=HM_PIECES), :] = packed[:, p * LANES:(p + 1) * LANES]

        lt = lax.dot_general(wrt_ref[...], hmb, (((1,), (1,)), ((), ())),
                             preferred_element_type=F32) + br_ref[...]
        row8 = _iota_f32((SUBLANES, rp), 0)
        lg = jnp.where(row8 < N_EXPERT_GROUPS, lt[0:SUBLANES, :], neg)
        mg = jnp.max(lg, axis=0, keepdims=True)
        p_grp = 1.0 / jnp.sum(jnp.exp(lg - mg), axis=0, keepdims=True)
        gsel = jnp.min(jnp.where(lg == mg, row8, float(SUBLANES)), axis=0, keepdims=True)
        le = jnp.zeros((SUBLANES, rp), F32)
        for g in range(N_EXPERT_GROUPS):
            e0 = EXPERT_ROW0 + g * EXPERTS_PER_GROUP
            le = jnp.where(gsel == float(g), lt[e0:e0 + EXPERTS_PER_GROUP, :], le)
        m1 = jnp.max(le, axis=0, keepdims=True)
        i1 = jnp.min(jnp.where(le == m1, row8, float(SUBLANES)), axis=0, keepdims=True)
        le2 = jnp.where(row8 == i1, neg, le)
        m2 = jnp.max(le2, axis=0, keepdims=True)
        i2 = jnp.min(jnp.where(le2 == m2, row8, float(SUBLANES)), axis=0, keepdims=True)
        e2 = jnp.exp(m2 - m1)
        den = 1.0 + e2
        gate0 = p_grp * (1.0 / den)
        gate1 = p_grp * (e2 / den)
        eid0 = gsel * float(EXPERTS_PER_GROUP) + i1
        eid1 = gsel * float(EXPERTS_PER_GROUP) + i2

        rowe = _iota_f32((N_EXPERTS, rp), 0)
        oh0 = rowe == eid0
        oh1 = rowe == eid1
        ohf = jnp.where(jnp.logical_or(oh0, oh1), 1.0, 0.0)
        before = jnp.dot(ohf.astype(BF16), tri_ref[0:rp, 0:rp], preferred_element_type=F32) + carry[:, 0:1]
        rank0 = jnp.sum(jnp.where(oh0, before, 0.0), axis=0, keepdims=True)
        rank1 = jnp.sum(jnp.where(oh1, before, 0.0), axis=0, keepdims=True)
        carry[...] = carry[...] + jnp.sum(ohf, axis=1, keepdims=True)

        route = jnp.where(row8 == 0.0, eid0,
                          jnp.where(row8 == 1.0, eid1,
                                    jnp.where(row8 == 2.0, rank0,
                                              jnp.where(row8 == 3.0, rank1, 0.0))))
        route_ref[:, r0:r0 + rp] = route.astype(I32)

        tpad = -(-rp // LANES) * LANES
        rowr = _iota_f32((ROUTER_ROWS, rp), 0)
        gt = jnp.where(rowr == 0.0, gate0, jnp.where(rowr == 1.0, gate1, 0.0))
        if tpad != rp:
            gt = jnp.concatenate([gt, jnp.zeros((ROUTER_ROWS, tpad - rp), F32)], axis=1)
        gates_ref[r0:r0 + rp, :] = gt.T[0:rp, :]

    stage_in(0)
    for part in range(n_parts):
        yp = stage_mix(part)
        if part + 1 < n_parts:
            stage_in(part + 1)
        stage_out(part, yp)
    cnt_ref[...] = carry[...]

    @pl.when(j == n_t - 1)
    def _():
        for g in range(n_seg):
            last = g * seg_stride + PAD + seg
            spool_ref[g] = pext[last - POOL_STATE:last, :]
            sconv_ref[g] = jnp.concatenate(
                [cext[lb, last - CONV_STATE:last, :] for lb in range(CONV_BLOCKS)], axis=-1)


def _mixer_call(x, pool_prefix, conv_prefix, pos0, tt, seg, weights, name):
    bsz, seq, _ = x.shape
    n_t = seq // tt
    n_seg = tt // seg
    ext_rows = n_seg * (PAD + seg)
    n_blk = bsz * n_t
    n_rows = bsz * seq
    tri = (lax.broadcasted_iota(I32, (tt, tt), 0) < lax.broadcasted_iota(I32, (tt, tt), 1)).astype(BF16)

    def const(shape):
        return pl.BlockSpec(shape, lambda b, j: (0,) * len(shape))

    def rows(width):
        return pl.BlockSpec((tt, width), lambda b, j: (b * n_t + j, 0))

    in_specs = [
        pl.BlockSpec((None, tt, D_MODEL), lambda b, j: (b, j, 0)),
        pl.BlockSpec((n_seg, PAD, D_POOL), lambda b, j: (b, 0, 0)),
        pl.BlockSpec((n_seg, PAD, D_CONV), lambda b, j: (b, 0, 0)),
    ] + [const(w.shape) for w in weights] + [const((tt, tt))]
    out_shape = (
        jax.ShapeDtypeStruct((n_rows, D_MODEL), F32),
        jax.ShapeDtypeStruct((n_rows * HM_PIECES, LANES), U32),
        jax.ShapeDtypeStruct((n_rows, LANES), F32),
        jax.ShapeDtypeStruct((n_blk, SUBLANES, tt), I32),
        jax.ShapeDtypeStruct((N_EXPERTS, LANES), F32),
        jax.ShapeDtypeStruct((1, bsz * n_seg, POOL_STATE, D_POOL), F32),
        jax.ShapeDtypeStruct((1, bsz * n_seg, CONV_STATE, D_CONV), F32),
    )
    out_specs = (
        rows(D_MODEL),
        pl.BlockSpec((tt * HM_PIECES, LANES), lambda b, j: (b * n_t + j, 0)),
        rows(LANES),
        pl.BlockSpec((None, SUBLANES, tt), lambda b, j: (b * n_t + j, 0, 0)),
        pl.BlockSpec((N_EXPERTS, LANES), lambda b, j: (0, 0)),
        pl.BlockSpec((None, n_seg, POOL_STATE, D_POOL), lambda b, j: (0, b, 0, 0)),
        pl.BlockSpec((None, n_seg, CONV_STATE, D_CONV), lambda b, j: (0, b, 0, 0)),
    )
    return pl.pallas_call(
        functools.partial(_mixer_kernel, pos0, tt, n_t, seg),
        out_shape=out_shape,
        grid=(bsz, n_t),
        in_specs=in_specs,
        out_specs=out_specs,
        scratch_shapes=[
            pltpu.VMEM((ext_rows, D_POOL), F32),
            pltpu.VMEM((CONV_BLOCKS, ext_rows, LANES), F32),
            pltpu.VMEM((CONV_BLOCKS, ext_rows, LANES), F32),
            pltpu.VMEM((N_EXPERTS, LANES), F32),
            pltpu.VMEM((ext_rows, D_POOL), F32),
            pltpu.VMEM((ext_rows, D_POOL), F32),
            pltpu.VMEM((ext_rows, D_POOL), F32),
        ],
        compiler_params=pltpu.CompilerParams(
            dimension_semantics=("arbitrary", "arbitrary"), vmem_limit_bytes=VMEM_LIMIT),
        name=name,
    )(x, pool_prefix, conv_prefix, *weights, tri)


def _dispatch_kernel(n_sample_tiles, last_ref, owns_ref, used_ref, dest_ref, hmp_ref, hms_ref, xb_ref,
                     zbuf, sem, zsem):
    i = pl.program_id(0)
    tc = dest_ref.shape[2]
    n_prompt_tiles = pl.num_programs(0) - n_sample_tiles
    tile_rows = zbuf.shape[0]
    n_all = xb_ref.shape[0] // tile_rows

    def zero_copy(t):
        dst = xb_ref.at[pl.ds(pl.multiple_of(t * tile_rows, tile_rows), tile_rows), :]
        return pltpu.make_async_copy(zbuf, dst, zsem)

    @pl.when(i == 0)
    def _():
        zbuf[...] = jnp.zeros_like(zbuf)
        for e in range(N_EXPERTS):
            @pl.when(owns_ref[e] > 0)
            def _(e=e):
                zero_copy(last_ref[e]).start()
        lax.fori_loop(used_ref[0], n_all, lambda t, c: (zero_copy(t).start(), c)[1], 0)
        for e in range(N_EXPERTS):
            @pl.when(owns_ref[e] > 0)
            def _(e=e):
                zero_copy(last_ref[e]).wait()
        lax.fori_loop(used_ref[0], n_all, lambda t, c: (zero_copy(t).wait(), c)[1], 0)

    def scatter(hm_ref):
        for t in range(tc):
            for k in range(TOP_K):
                d = dest_ref[0, k, t]
                src = hm_ref.at[pl.ds(t * HM_PIECES, HM_PIECES), :]
                dst = xb_ref.at[pl.ds(pl.multiple_of(d * HM_PIECES, HM_PIECES), HM_PIECES), :]
                pltpu.make_async_copy(src, dst, sem).start(priority=k)
        for k in range(TOP_K):
            pltpu.make_async_copy(hm_ref, xb_ref.at[pl.ds(0, tc * HM_PIECES), :], sem).wait()

    @pl.when(i < n_prompt_tiles)
    def _():
        scatter(hmp_ref)

    @pl.when(i >= n_prompt_tiles)
    def _():
        scatter(hms_ref)


def _dispatch_call(last_tile, owns, n_used, dest_blocks, hm_p, hm_s, n_slots):
    tc = DISPATCH_TILE
    n_p = hm_p.shape[0] // (tc * HM_PIECES)
    n_s = hm_s.shape[0] // (tc * HM_PIECES)
    grid_spec = pltpu.PrefetchScalarGridSpec(
        num_scalar_prefetch=3,
        grid=(n_p + n_s,),
        in_specs=[
            pl.BlockSpec((1, TOP_K, tc), lambda i, *_: (i, 0, 0), memory_space=pltpu.SMEM),
            pl.BlockSpec((tc * HM_PIECES, LANES), lambda i, *_: (jnp.minimum(i, n_p - 1), 0)),
            pl.BlockSpec((tc * HM_PIECES, LANES), lambda i, *_: (jnp.maximum(i - n_p, 0), 0)),
        ],
        out_specs=pl.BlockSpec(memory_space=pl.ANY),
        scratch_shapes=[
            pltpu.VMEM((EXPERT_TILE * HM_PIECES, LANES), U32),
            pltpu.SemaphoreType.DMA(()),
            pltpu.SemaphoreType.DMA(()),
        ],
    )
    return pl.pallas_call(
        functools.partial(_dispatch_kernel, n_s),
        out_shape=jax.ShapeDtypeStruct((n_slots * HM_PIECES, LANES), U32),
        grid_spec=grid_spec,
        compiler_params=pltpu.CompilerParams(dimension_semantics=("arbitrary",)),
        name="dispatch",
    )(last_tile, owns, n_used, dest_blocks, hm_p, hm_s)


def _pack_bf16_pairs(x):
    w = x.shape[1] // 2
    rounded = x.astype(BF16).astype(F32)
    lo = lax.bitcast_convert_type(rounded[:, :w], U32)
    hi = lax.bitcast_convert_type(rounded[:, w:], U32)
    return (hi & jnp.uint32(0xFFFF0000)) | lax.shift_right_logical(lo, jnp.uint32(16))


def _unpack_bf16_pairs(words):
    lo = lax.bitcast_convert_type(lax.shift_left(words, jnp.uint32(16)), F32)
    hi = lax.bitcast_convert_type(words & jnp.uint32(0xFFFF0000), F32)
    return lo, hi


def _expert_kernel(ord_ref, act_ref, used_ref, nact_ref, short_ref, wg_ref, wu_ref, wd_ref, xb_ref, yb_ref,
                   wgf, wuf, wdf, wgb, wub, wdb, xbuf, ybuf, sem_w, sem_in, sem_out):
    tile_rows = xbuf.shape[1]
    tm = tile_rows // HM_PIECES
    n = used_ref[0]
    n_act = nact_ref[0]

    def weight_copies(k, slot):
        e = act_ref[k]
        return [pltpu.make_async_copy(src.at[e], dst.at[slot], sem_w.at[slot])
                for src, dst in ((wg_ref, wgf), (wu_ref, wuf), (wd_ref, wdf))]

    def tile(ref, t):
        return ref.at[pl.ds(pl.multiple_of(t * tile_rows, tile_rows), tile_rows), :]

    def fetch(t, slot):
        return pltpu.make_async_copy(tile(xb_ref, t), xbuf.at[slot], sem_in.at[slot])

    def put(t, slot):
        return pltpu.make_async_copy(ybuf.at[slot], tile(yb_ref, t), sem_out.at[slot])

    for cp in weight_copies(0, 0):
        cp.start()
    for ahead in range(X_SLOTS - 1):
        @pl.when(ahead < n)
        def _(ahead=ahead):
            fetch(ahead, ahead).start()

    def body(t, carry):
        k = ord_ref[t]

        @pl.when((t == 0) | (k != ord_ref[jnp.maximum(t - 1, 0)]))
        def _():
            wslot = lax.rem(k, 2)
            for cp in weight_copies(k, wslot):
                cp.wait()
            wgb[...] = wgf[wslot].astype(BF16)
            wub[...] = wuf[wslot].astype(BF16)
            wdb[...] = wdf[wslot].astype(BF16)

            @pl.when(k + 1 < n_act)
            def _():
                for cp in weight_copies(k + 1, 1 - wslot):
                    cp.start()

        slot = lax.rem(t, X_SLOTS)
        fetch(t, slot).wait()

        @pl.when(t + X_SLOTS - 1 < n)
        def _():
            fetch(t + X_SLOTS - 1, lax.rem(t + X_SLOTS - 1, X_SLOTS)).start()

        oslot = lax.rem(t, 2)

        @pl.when(t >= 2)
        def _():
            put(t - 2, oslot).wait()

        def mlp(rows):
            pieces = [_unpack_bf16_pairs(xbuf[slot, pl.ds(p, rows, stride=HM_PIECES), :])
                      for p in range(HM_PIECES)]
            xt = jnp.concatenate([lo.astype(BF16) for lo, _ in pieces] + [hi.astype(BF16) for _, hi in pieces],
                                 axis=-1)
            g = jnp.dot(xt, wgb[...], preferred_element_type=F32)
            up = jnp.dot(xt, wub[...], preferred_element_type=F32)
            a = (g * jax.nn.sigmoid(g) * up).astype(BF16)
            packed = _pack_bf16_pairs(jnp.dot(a, wdb[...], preferred_element_type=F32))
            for p in range(HM_PIECES):
                ybuf[oslot, pl.ds(p, rows, stride=HM_PIECES), :] = packed[:, p * LANES:(p + 1) * LANES]
            if rows < tm:
                ybuf[oslot, rows * HM_PIECES:, :] = jnp.zeros(((tm - rows) * HM_PIECES, LANES), ybuf.dtype)

        @pl.when(short_ref[t] == 0)
        def _():
            mlp(tm)

        @pl.when(short_ref[t] != 0)
        def _():
            mlp(tm // 2)

        put(t, oslot).start()
        return carry

    lax.fori_loop(0, n, body, 0)

    @pl.when(n >= 2)
    def _():
        put(n - 2, lax.rem(n, 2)).wait()

    @pl.when(n >= 1)
    def _():
        put(n - 1, lax.rem(n - 1, 2)).wait()

    ybuf[0] = jnp.zeros(ybuf.shape[1:], ybuf.dtype)
    n_all = yb_ref.shape[0] // tile_rows

    def zero_tile(t, carry):
        cp = pltpu.make_async_copy(ybuf.at[0], tile(yb_ref, t), sem_out.at[0])
        cp.start()
        cp.wait()
        return carry

    lax.fori_loop(n, n_all, zero_tile, 0)


def _expert_call(tile_ord, active, n_used, n_active, tile_short, xb, w_eg, w_eu, w_ed):
    tile_rows = EXPERT_TILE * HM_PIECES
    any_spec = pl.BlockSpec(memory_space=pl.ANY)
    grid_spec = pltpu.PrefetchScalarGridSpec(
        num_scalar_prefetch=5,
        grid=(1,),
        in_specs=[any_spec] * 4,
        out_specs=any_spec,
        scratch_shapes=[
            pltpu.VMEM((2, D_MODEL, D_FF), F32),
            pltpu.VMEM((2, D_MODEL, D_FF), F32),
            pltpu.VMEM((2, D_FF, D_MODEL), F32),
            pltpu.VMEM((D_MODEL, D_FF), BF16),
            pltpu.VMEM((D_MODEL, D_FF), BF16),
            pltpu.VMEM((D_FF, D_MODEL), BF16),
            pltpu.VMEM((X_SLOTS, tile_rows, LANES), U32),
            pltpu.VMEM((2, tile_rows, LANES), U32),
            pltpu.SemaphoreType.DMA((2,)),
            pltpu.SemaphoreType.DMA((X_SLOTS,)),
            pltpu.SemaphoreType.DMA((2,)),
        ],
    )
    return pl.pallas_call(
        _expert_kernel,
        out_shape=jax.ShapeDtypeStruct(xb.shape, U32),
        grid_spec=grid_spec,
        compiler_params=pltpu.CompilerParams(
            dimension_semantics=("arbitrary",), vmem_limit_bytes=VMEM_LIMIT),
        name="experts",
    )(tile_ord, active, n_used, n_active, tile_short, w_eg, w_eu, w_ed, xb)


def _gather_rows(dest_ref, yb_ref, buf_ref, sem, tokens):
    for t in tokens:
        for k in range(TOP_K):
            d = dest_ref[0, k, t]
            src = yb_ref.at[pl.ds(pl.multiple_of(d * HM_PIECES, HM_PIECES), HM_PIECES), :]
            dst = buf_ref.at[k, pl.ds(t * HM_PIECES, HM_PIECES), :]
            pltpu.make_async_copy(src, dst, sem).start(priority=k)


def _wait_rows(yb_ref, buf_ref, sem):
    for k in range(TOP_K):
        pltpu.make_async_copy(yb_ref.at[pl.ds(0, buf_ref.shape[1]), :], buf_ref.at[k], sem).wait()


def _combine_kernel(dest0_ref, dest1_ref, dest2_ref, x1_ref, gates_ref, gf_ref, yb_ref, out_ref, buf, sems):
    i = pl.program_id(0)
    tc = TOKEN_TILE
    n_chunks = tc // COMBINE_CHUNK

    @pl.when(i == 0)
    def _():
        _gather_rows(dest0_ref, yb_ref, buf.at[0], sems.at[0], range(tc))

    gf = gf_ref[...]
    for half, next_dest in ((0, dest1_ref), (1, dest2_ref)):
        cur, nxt = buf.at[half], buf.at[1 - half]
        _wait_rows(yb_ref, cur, sems.at[half])
        for c in range(n_chunks):
            rows = pl.ds(half * tc + c * COMBINE_CHUNK, COMBINE_CHUNK)
            gates = gates_ref[rows, :]
            lows, highs = [], []
            for p in range(HM_PIECES):
                piece = pl.ds(c * COMBINE_CHUNK * HM_PIECES + p, COMBINE_CHUNK, stride=HM_PIECES)
                lo0, hi0 = _unpack_bf16_pairs(cur[0, piece, :])
                lo1, hi1 = _unpack_bf16_pairs(cur[1, piece, :])
                lows.append(lo0 * gates[:, 0:1] + lo1 * gates[:, 1:2])
                highs.append(hi0 * gates[:, 0:1] + hi1 * gates[:, 1:2])
            xo = x1_ref[rows, :] + jnp.concatenate(lows + highs, axis=-1)
            _gather_rows(next_dest, yb_ref, nxt, sems.at[1 - half],
                         range(c * COMBINE_CHUNK, (c + 1) * COMBINE_CHUNK))
            ms = jnp.mean(xo * xo, axis=-1, keepdims=True)
            out_ref[rows, :] = xo * lax.rsqrt(ms + EPS) * gf

    @pl.when(i == pl.num_programs(0) - 1)
    def _():
        _wait_rows(yb_ref, buf.at[0], sems.at[0])


def _combine_call(dest_blocks, x1, gates, gf, yb, name):
    n_rows = x1.shape[0]
    tc = TOKEN_TILE
    n_tiles = n_rows // tc
    assert n_tiles % 2 == 0

    def dest_spec(index_map):
        return pl.BlockSpec((1, TOP_K, tc), index_map, memory_space=pltpu.SMEM)

    return pl.pallas_call(
        _combine_kernel,
        out_shape=jax.ShapeDtypeStruct((n_rows, D_MODEL), F32),
        grid=(n_tiles // 2,),
        in_specs=[
            dest_spec(lambda i: (0, 0, 0)),
            dest_spec(lambda i: (2 * i + 1, 0, 0)),
            dest_spec(lambda i: (jnp.minimum(2 * i + 2, n_tiles - 1), 0, 0)),
            pl.BlockSpec((2 * tc, D_MODEL), lambda i: (i, 0)),
            pl.BlockSpec((2 * tc, LANES), lambda i: (i, 0)),
            pl.BlockSpec((1, D_MODEL), lambda i: (0, 0)),
            pl.BlockSpec(memory_space=pl.ANY),
        ],
        out_specs=pl.BlockSpec((2 * tc, D_MODEL), lambda i: (i, 0)),
        scratch_shapes=[pltpu.VMEM((2, TOP_K, tc * HM_PIECES, LANES), U32), pltpu.SemaphoreType.DMA((2,))],
        compiler_params=pltpu.CompilerParams(
            dimension_semantics=("arbitrary",), vmem_limit_bytes=VMEM_LIMIT),
        name=name,
    )(dest_blocks, dest_blocks, dest_blocks, x1, gates, gf, yb)


def _block_diag2(a, b):
    z = jnp.zeros_like(a)
    return jnp.concatenate([jnp.concatenate([a, z], axis=1), jnp.concatenate([z, b], axis=1)], axis=0)


def kernel(x_prompt, x_sample, state_pool, state_conv, norm1_g, w_in, pool_lin, pool_scale, conv_dw, conv_dw_b, conv_ln_g, conv_ln_b, w_out, b_out, norm2_g, w_rg, b_rg, w_re, b_re, w_eg, w_eu, w_ed, norm_f_g):
    assert norm1_g.shape[0] == 1, "single-layer trunk"
    bsz, seq, _ = x_prompt.shape
    dbsz, dseq, _ = x_sample.shape
    past_len = 1024
    n_prompt = bsz * seq
    n_sample = dbsz * dseq
    n_tokens = n_prompt + n_sample
    assert seq % PROMPT_TILE == 0 and SAMPLE_TILE % dseq == 0 and n_sample % SAMPLE_TILE == 0
    assert all(n % t == 0 for n in (n_prompt, n_sample) for t in (TOKEN_TILE, DISPATCH_TILE))

    row = lambda a: a.reshape(1, -1)
    router_w = jnp.zeros((ROUTER_ROWS, D_MODEL), F32)
    router_w = router_w.at[0:N_EXPERT_GROUPS].set(w_rg[0].T)
    router_w = router_w.at[EXPERT_ROW0:EXPERT_ROW0 + N_EXPERTS].set(
        jnp.transpose(w_re[0], (0, 2, 1)).reshape(N_EXPERTS, D_MODEL))
    router_b = jnp.zeros((ROUTER_ROWS,), F32)
    router_b = router_b.at[0:N_EXPERT_GROUPS].set(b_rg[0])
    router_b = router_b.at[EXPERT_ROW0:EXPERT_ROW0 + N_EXPERTS].set(b_re[0].reshape(-1))
    conv_w = jnp.concatenate([conv_dw[0], jnp.zeros((1, D_CONV), F32)], axis=0)
    weights = [
        row(norm1_g[0]), w_in[0].astype(BF16),
        _block_diag2(pool_lin[0, 0], pool_lin[0, 1]).astype(BF16),
        _block_diag2(pool_lin[0, 2], pool_lin[0, 3]).astype(BF16),
        row(pool_scale[0]), conv_w, row(conv_dw_b[0]), row(conv_ln_g[0]), row(conv_ln_b[0]),
        w_out[0].astype(BF16), row(b_out[0]), row(norm2_g[0]),
        router_w.astype(BF16), router_b.reshape(ROUTER_ROWS, 1),
    ]

    zpool = jnp.zeros((bsz, PAD, D_POOL), F32)
    zconv = jnp.zeros((bsz, PAD, D_CONV), F32)
    ppool = jnp.pad(state_pool[0], ((0, 0), (PAD - POOL_STATE, 0), (0, 0)))
    pconv = jnp.pad(state_conv[0], ((0, 0), (PAD - CONV_STATE, 0), (0, 0)))

    x1_p, hm_p, gates_p, route_p, cnt_p, sp_p, sc_p = _mixer_call(
        x_prompt, zpool, zconv, 0, PROMPT_TILE, PROMPT_TILE, weights, "mixer_prompt")
    x1_s, hm_s, gates_s, route_s, cnt_s, sp_s, sc_s = _mixer_call(
        x_sample.reshape(-1, SAMPLE_TILE, D_MODEL), ppool, pconv, past_len, SAMPLE_TILE, dseq, weights,
        "mixer_sample")

    tm = EXPERT_TILE
    cnt_p = cnt_p[:, 0].astype(I32)
    cnt_s = cnt_s[:, 0].astype(I32)
    padded = (cnt_p + cnt_s + tm - 1) // tm * tm
    pad_end = jnp.cumsum(padded)
    pad_start = pad_end - padded

    def dest_blocks(route, base, tile):
        eid = jnp.stack([route[:, k, :].reshape(-1) for k in range(TOP_K)])
        rank = jnp.stack([route[:, TOP_K + k, :].reshape(-1) for k in range(TOP_K)])
        experts = jnp.arange(N_EXPERTS, dtype=I32)
        dest = rank + jnp.sum(jnp.where(eid[..., None] == experts, base, 0), axis=-1)
        return dest.reshape(TOP_K, -1, tile).transpose(1, 0, 2)

    dest_p = dest_blocks(route_p, pad_start, TOKEN_TILE)
    dest_s = dest_blocks(route_s, pad_start + cnt_p, TOKEN_TILE)
    dest_all = jnp.concatenate([dest_blocks(route_p, pad_start, DISPATCH_TILE),
                                dest_blocks(route_s, pad_start + cnt_p, DISPATCH_TILE)], axis=0)
    n_tiles = -(-(n_tokens * TOP_K + N_EXPERTS * (tm - 1)) // tm)
    n_used = (pad_end[-1:] // tm).astype(I32)

    owns = padded > 0
    xb = _dispatch_call((pad_end // tm - 1).astype(I32), owns.astype(I32), n_used,
                        dest_all, hm_p, hm_s, n_tiles * tm)
    active = jnp.argsort(jnp.logical_not(owns), stable=True).astype(I32)
    tile_expert = jnp.minimum(
        jnp.sum(pad_end[None, :] <= (jnp.arange(n_tiles, dtype=I32) * tm)[:, None], axis=1), N_EXPERTS - 1)
    ordinal = jnp.cumsum(owns.astype(I32)) - 1
    tile_ord = jnp.sum(jnp.where(tile_expert[:, None] == jnp.arange(N_EXPERTS, dtype=I32), ordinal, 0),
                       axis=1).astype(I32)
    n_active = jnp.sum(owns.astype(I32)).reshape(1)
    real_end = jnp.sum(jnp.where(tile_expert[:, None] == jnp.arange(N_EXPERTS, dtype=I32),
                                 pad_start + cnt_p + cnt_s, 0), axis=1)
    tile_short = (real_end - jnp.arange(n_tiles, dtype=I32) * tm <= tm // 2).astype(I32)
    yb = _expert_call(tile_ord, active, n_used, n_active, tile_short, xb, w_eg[0], w_eu[0], w_ed[0])
    gf = row(norm_f_g)
    y_p = _combine_call(dest_p, x1_p, gates_p, gf, yb, "combine_prompt")
    y_s = _combine_call(dest_s, x1_s, gates_s, gf, yb, "combine_sample")

    return (y_p.reshape(bsz, seq, D_MODEL), y_s.reshape(dbsz, dseq, D_MODEL),
            sp_p, sc_p, sp_s, sc_s)
```

```python
import functools

import jax
import jax.numpy as jnp
from jax import lax
from jax.experimental import pallas as pl
from jax.experimental.pallas import tpu as pltpu

F32 = jnp.float32
BF16 = jnp.bfloat16
U32 = jnp.uint32
I32 = jnp.int32

D_MODEL = 1024
D_POOL = 512
D_CONV = 512
D_IN = D_POOL + 2 * D_CONV
POOL_WINDOWS = (2, 4, 8, 16)
POOL_GROUP_DIM = D_POOL // len(POOL_WINDOWS)
POOL_STATE = max(POOL_WINDOWS) - 1
CONV_WIDTH = 31
CONV_STATE = CONV_WIDTH - 1
N_EXPERT_GROUPS = 4
EXPERTS_PER_GROUP = 8
N_EXPERTS = N_EXPERT_GROUPS * EXPERTS_PER_GROUP
TOP_K = 2
D_FF = D_MODEL // 2
EPS = 1e-6

SUBLANES = 8
LANES = 128
PAD = 32
ROUTER_ROWS = LANES
EXPERT_ROW0 = SUBLANES
CONV_CHUNK = 64
CONV_STRIDE = 2
CONV_BLOCKS = D_CONV // LANES
MIXER_PART = 256
HM_WORDS = D_MODEL // 2
HM_PIECES = HM_WORDS // LANES

PROMPT_TILE = 512
SAMPLE_TILE = 512
TOKEN_TILE = 512
DISPATCH_TILE = 1024
EXPERT_TILE = 512
X_SLOTS = 4
COMBINE_CHUNK = 32
V7X_VMEM_BYTES = 64 * 1024 * 1024
VMEM_LIMIT = V7X_VMEM_BYTES * 3 // 4


def _iota_f32(shape, axis):
    return lax.broadcasted_iota(I32, shape, axis).astype(F32)


def _mixer_kernel(pos0, tt, n_t, seg,
                  x_ref, pp_ref, cp_ref, g1_ref, win_ref, bd01_ref, bd23_ref, pscale_ref,
                  cw_ref, cb_ref, lng_ref, lnb_ref, wout_ref, bout_ref, g2_ref, wrt_ref, br_ref,
                  tri_ref,
                  x1_ref, hm_ref, gates_ref, route_ref, cnt_ref, spool_ref, sconv_ref,
                  pext, cext, hbuf, carry, lv2, lv4, lv8):
    b = pl.program_id(0)
    j = pl.program_id(1)

    @pl.when((b == 0) & (j == 0))
    def _():
        carry[...] = jnp.zeros_like(carry)

    n_seg = tt // seg
    seg_stride = PAD + seg
    assert n_seg == 1 or n_t == 1, "several segments per tile: each one is a whole sequence"

    @pl.when(j == 0)
    def _():
        for g in range(n_seg):
            pext[g * seg_stride:g * seg_stride + PAD, :] = pp_ref[g]
            for lb in range(CONV_BLOCKS):
                cext[lb, g * seg_stride:g * seg_stride + PAD, :] = cp_ref[g, :, lb * LANES:(lb + 1) * LANES]

    @pl.when(j > 0)
    def _():
        pext[0:PAD, :] = pext[tt:tt + PAD, :]
        cext[:, 0:PAD, :] = cext[:, tt:tt + PAD, :]

    n_parts = max(1, tt // MIXER_PART)
    rp = tt // n_parts
    gd = POOL_GROUP_DIM
    neg = jnp.float32(-jnp.inf)

    def runs_of(part):
        r0 = part * rp
        if n_seg == 1:
            return [(r0, rp, r0)]
        return [(t0, seg, t0 // seg * seg_stride) for t0 in range(r0, r0 + rp, seg)]

    def stage_in(part):
        r0 = part * rp
        x = x_ref[r0:r0 + rp, :]
        ms = jnp.mean(x * x, axis=-1, keepdims=True)
        h = (x * lax.rsqrt(ms + EPS) * g1_ref[...]).astype(BF16)
        z = jnp.dot(h, win_ref[...], preferred_element_type=F32)
        v = z[:, D_POOL:D_POOL + D_CONV] * jax.nn.sigmoid(z[:, D_POOL + D_CONV:])
        for t0, n, v0 in runs_of(part):
            pext[PAD + v0:PAD + v0 + n, :] = z[t0 - r0:t0 - r0 + n, :D_POOL]
            for lb in range(CONV_BLOCKS):
                cext[lb, PAD + v0:PAD + v0 + n, :] = v[t0 - r0:t0 - r0 + n, lb * LANES:(lb + 1) * LANES]

    def stage_mix(part):
        return jnp.concatenate([mix_run(*run) for run in runs_of(part)], axis=0)

    def mix_run(t0, rp, r0):
        first_pos = pos0 + (j * tt + t0 if n_seg == 1 else 0)
        pos = first_pos + lax.broadcasted_iota(I32, (rp, 1), 0)
        all_full = pos0 + (t0 if n_seg == 1 else 0) >= POOL_STATE
        top = r0 + PAD
        end = top + rp
        lv2[r0 + 8:end, :] = pext[r0 + 8:end, :] + pext[r0 + 7:end - 1, :]
        lv4[r0 + 16:end, gd:] = lv2[r0 + 16:end, gd:] + lv2[r0 + 14:end - 2, gd:]
        lv8[r0 + 24:end, 2 * gd:] = lv4[r0 + 24:end, 2 * gd:] + lv4[r0 + 20:end - 4, 2 * gd:]
        sums = [lv2[top:end, 0:gd], lv4[top:end, gd:2 * gd], lv8[top:end, 2 * gd:3 * gd],
                lv8[top:end, 3 * gd:] + lv8[top - 8:end - 8, 3 * gd:]]
        u = pext[top:end, :]
        parts = []
        for g, w in enumerate(POOL_WINDOWS):
            sl = slice(g * gd, (g + 1) * gd)
            cnt = float(w) if all_full else jnp.minimum(pos + 1, w).astype(F32)
            parts.append(sums[g] / cnt - u[:, sl])
        d = jnp.concatenate(parts, axis=-1).astype(BF16)
        half = D_POOL // 2
        yp = jnp.concatenate(
            [jnp.dot(d[:, :half], bd01_ref[...], preferred_element_type=F32),
             jnp.dot(d[:, half:], bd23_ref[...], preferred_element_type=F32)], axis=-1)
        yp = yp * pscale_ref[...]

        chunk = min(CONV_CHUNK, rp // CONV_STRIDE)
        for lb in range(CONV_BLOCKS):
            lanes = slice(lb * LANES, (lb + 1) * LANES)
            wts = [jnp.broadcast_to(cw_ref[k:k + 1, lanes], (chunk, LANES)) for k in range(CONV_WIDTH)]
            for first in range(r0, r0 + rp, CONV_STRIDE * chunk):
                for phase in range(CONV_STRIDE):
                    acc = None
                    for k in range(CONV_WIDTH):
                        start = first + phase + PAD - CONV_STATE + k
                        tap = cext[lb, pl.ds(start, chunk, stride=CONV_STRIDE), :] * wts[k]
                        acc = tap if acc is None else acc + tap
                    hbuf[lb, pl.ds(first + phase, chunk, stride=CONV_STRIDE), :] = acc
        return yp

    def stage_out(part, yp):
        r0 = part * rp
        x = x_ref[r0:r0 + rp, :]
        hc = jnp.concatenate(
            [jnp.concatenate([hbuf[lb, v0:v0 + n, :] for _, n, v0 in runs_of(part)], axis=0)
             for lb in range(CONV_BLOCKS)], axis=-1) + cb_ref[...]
        mu = jnp.mean(hc, axis=-1, keepdims=True)
        var = jnp.mean(jnp.square(hc - mu), axis=-1, keepdims=True)
        yln = (hc - mu) * lax.rsqrt(var + EPS) * lng_ref[...] + lnb_ref[...]
        yc = yln * jax.nn.sigmoid(yln)

        o = (jnp.dot(yp.astype(BF16), wout_ref[0:D_POOL, :], preferred_element_type=F32)
             + jnp.dot(yc.astype(BF16), wout_ref[D_POOL:, :], preferred_element_type=F32))
        x1 = x + (o + bout_ref[...])
        x1_ref[r0:r0 + rp, :] = x1

        ms2 = jnp.mean(x1 * x1, axis=-1, keepdims=True)
        hmb = (x1 * lax.rsqrt(ms2 + EPS) * g2_ref[...]).astype(BF16)
        packed = _pack_bf16_pairs(hmb.astype(F32))
        for p in range(HM_PIECES):
            hm_ref[pl.ds(r0 * HM_PIECES + p, rp, stride=HM_PIECES), :] = packed[:, p * LANES:(p + 1) * LANES]

        lt = lax.dot_general(wrt_ref[...], hmb, (((1,), (1,)), ((), ())),
                             preferred_element_type=F32) + br_ref[...]
        row8 = _iota_f32((SUBLANES, rp), 0)
        lg = jnp.where(row8 < N_EXPERT_GROUPS, lt[0:SUBLANES, :], neg)
        mg = jnp.max(lg, axis=0, keepdims=True)
        p_grp = 1.0 / jnp.sum(jnp.exp(lg - mg), axis=0, keepdims=True)
        gsel = jnp.min(jnp.where(lg == mg, row8, float(SUBLANES)), axis=0, keepdims=True)
        le = jnp.zeros((SUBLANES, rp), F32)
        for g in range(N_EXPERT_GROUPS):
            e0 = EXPERT_ROW0 + g * EXPERTS_PER_GROUP
            le = jnp.where(gsel == float(g), lt[e0:e0 + EXPERTS_PER_GROUP, :], le)
        m1 = jnp.max(le, axis=0, keepdims=True)
        i1 = jnp.min(jnp.where(le == m1, row8, float(SUBLANES)), axis=0, keepdims=True)
        le2 = jnp.where(row8 == i1, neg, le)
        m2 = jnp.max(le2, axis=0, keepdims=True)
        i2 = jnp.min(jnp.where(le2 == m2, row8, float(SUBLANES)), axis=0, keepdims=True)
        e2 = jnp.exp(m2 - m1)
        den = 1.0 + e2
        gate0 = p_grp * (1.0 / den)
        gate1 = p_grp * (e2 / den)
        eid0 = gsel * float(EXPERTS_PER_GROUP) + i1
        eid1 = gsel * float(EXPERTS_PER_GROUP) + i2

        rowe = _iota_f32((N_EXPERTS, rp), 0)
        oh0 = rowe == eid0
        oh1 = rowe == eid1
        ohf = jnp.where(jnp.logical_or(oh0, oh1), 1.0, 0.0)
        before = jnp.dot(ohf.astype(BF16), tri_ref[0:rp, 0:rp], preferred_element_type=F32) + carry[:, 0:1]
        rank0 = jnp.sum(jnp.where(oh0, before, 0.0), axis=0, keepdims=True)
        rank1 = jnp.sum(jnp.where(oh1, before, 0.0), axis=0, keepdims=True)
        carry[...] = carry[...] + jnp.sum(ohf, axis=1, keepdims=True)

        route = jnp.where(row8 == 0.0, eid0,
                          jnp.where(row8 == 1.0, eid1,
                                    jnp.where(row8 == 2.0, rank0,
                                              jnp.where(row8 == 3.0, rank1, 0.0))))
        route_ref[:, r0:r0 + rp] = route.astype(I32)

        tpad = -(-rp // LANES) * LANES
        rowr = _iota_f32((ROUTER_ROWS, rp), 0)
        gt = jnp.where(rowr == 0.0, gate0, jnp.where(rowr == 1.0, gate1, 0.0))
        if tpad != rp:
            gt = jnp.concatenate([gt, jnp.zeros((ROUTER_ROWS, tpad - rp), F32)], axis=1)
        gates_ref[r0:r0 + rp, :] = gt.T[0:rp, :]

    stage_in(0)
    for part in range(n_parts):
        yp = stage_mix(part)
        if part + 1 < n_parts:
            stage_in(part + 1)
        stage_out(part, yp)
    cnt_ref[...] = carry[...]

    @pl.when(j == n_t - 1)
    def _():
        for g in range(n_seg):
            last = g * seg_stride + PAD + seg
            spool_ref[g] = pext[last - POOL_STATE:last, :]
            sconv_ref[g] = jnp.concatenate(
                [cext[lb, last - CONV_STATE:last, :] for lb in range(CONV_BLOCKS)], axis=-1)


def _mixer_call(x, pool_prefix, conv_prefix, pos0, tt, seg, weights, name):
    bsz, seq, _ = x.shape
    n_t = seq // tt
    n_seg = tt // seg
    ext_rows = n_seg * (PAD + seg)
    n_blk = bsz * n_t
    n_rows = bsz * seq
    tri = (lax.broadcasted_iota(I32, (tt, tt), 0) < lax.broadcasted_iota(I32, (tt, tt), 1)).astype(BF16)

    def const(shape):
        return pl.BlockSpec(shape, lambda b, j: (0,) * len(shape))

    def rows(width):
        return pl.BlockSpec((tt, width), lambda b, j: (b * n_t + j, 0))

    in_specs = [
        pl.BlockSpec((None, tt, D_MODEL), lambda b, j: (b, j, 0)),
        pl.BlockSpec((n_seg, PAD, D_POOL), lambda b, j: (b, 0, 0)),
        pl.BlockSpec((n_seg, PAD, D_CONV), lambda b, j: (b, 0, 0)),
    ] + [const(w.shape) for w in weights] + [const((tt, tt))]
    out_shape = (
        jax.ShapeDtypeStruct((n_rows, D_MODEL), F32),
        jax.ShapeDtypeStruct((n_rows * HM_PIECES, LANES), U32),
        jax.ShapeDtypeStruct((n_rows, LANES), F32),
        jax.ShapeDtypeStruct((n_blk, SUBLANES, tt), I32),
        jax.ShapeDtypeStruct((N_EXPERTS, LANES), F32),
        jax.ShapeDtypeStruct((1, bsz * n_seg, POOL_STATE, D_POOL), F32),
        jax.ShapeDtypeStruct((1, bsz * n_seg, CONV_STATE, D_CONV), F32),
    )
    out_specs = (
        rows(D_MODEL),
        pl.BlockSpec((tt * HM_PIECES, LANES), lambda b, j: (b * n_t + j, 0)),
        rows(LANES),
        pl.BlockSpec((None, SUBLANES, tt), lambda b, j: (b * n_t + j, 0, 0)),
        pl.BlockSpec((N_EXPERTS, LANES), lambda b, j: (0, 0)),
        pl.BlockSpec((None, n_seg, POOL_STATE, D_POOL), lambda b, j: (0, b, 0, 0)),
        pl.BlockSpec((None, n_seg, CONV_STATE, D_CONV), lambda b, j: (0, b, 0, 0)),
    )
    return pl.pallas_call(
        functools.partial(_mixer_kernel, pos0, tt, n_t, seg),
        out_shape=out_shape,
        grid=(bsz, n_t),
        in_specs=in_specs,
        out_specs=out_specs,
        scratch_shapes=[
            pltpu.VMEM((ext_rows, D_POOL), F32),
            pltpu.VMEM((CONV_BLOCKS, ext_rows, LANES), F32),
            pltpu.VMEM((CONV_BLOCKS, ext_rows, LANES), F32),
            pltpu.VMEM((N_EXPERTS, LANES), F32),
            pltpu.VMEM((ext_rows, D_POOL), F32),
            pltpu.VMEM((ext_rows, D_POOL), F32),
            pltpu.VMEM((ext_rows, D_POOL), F32),
        ],
        compiler_params=pltpu.CompilerParams(
            dimension_semantics=("arbitrary", "arbitrary"), vmem_limit_bytes=VMEM_LIMIT),
        name=name,
    )(x, pool_prefix, conv_prefix, *weights, tri)


def _dispatch_kernel(n_sample_tiles, last_ref, owns_ref, used_ref, dest_ref, hmp_ref, hms_ref, xb_ref,
                     zbuf, sem, zsem):
    i = pl.program_id(0)
    tc = dest_ref.shape[2]
    n_prompt_tiles = pl.num_programs(0) - n_sample_tiles
    tile_rows = zbuf.shape[0]
    n_all = xb_ref.shape[0] // tile_rows

    def zero_copy(t):
        dst = xb_ref.at[pl.ds(pl.multiple_of(t * tile_rows, tile_rows), tile_rows), :]
        return pltpu.make_async_copy(zbuf, dst, zsem)

    @pl.when(i == 0)
    def _():
        zbuf[...] = jnp.zeros_like(zbuf)
        for e in range(N_EXPERTS):
            @pl.when(owns_ref[e] > 0)
            def _(e=e):
                zero_copy(last_ref[e]).start()
        lax.fori_loop(used_ref[0], n_all, lambda t, c: (zero_copy(t).start(), c)[1], 0)
        for e in range(N_EXPERTS):
            @pl.when(owns_ref[e] > 0)
            def _(e=e):
                zero_copy(last_ref[e]).wait()
        lax.fori_loop(used_ref[0], n_all, lambda t, c: (zero_copy(t).wait(), c)[1], 0)

    def scatter(hm_ref):
        for t in range(tc):
            for k in range(TOP_K):
                d = dest_ref[0, k, t]
                src = hm_ref.at[pl.ds(t * HM_PIECES, HM_PIECES), :]
                dst = xb_ref.at[pl.ds(pl.multiple_of(d * HM_PIECES, HM_PIECES), HM_PIECES), :]
                pltpu.make_async_copy(src, dst, sem).start(priority=k)
        for k in range(TOP_K):
            pltpu.make_async_copy(hm_ref, xb_ref.at[pl.ds(0, tc * HM_PIECES), :], sem).wait()

    @pl.when(i < n_prompt_tiles)
    def _():
        scatter(hmp_ref)

    @pl.when(i >= n_prompt_tiles)
    def _():
        scatter(hms_ref)


def _dispatch_call(last_tile, owns, n_used, dest_blocks, hm_p, hm_s, n_slots):
    tc = DISPATCH_TILE
    n_p = hm_p.shape[0] // (tc * HM_PIECES)
    n_s = hm_s.shape[0] // (tc * HM_PIECES)
    grid_spec = pltpu.PrefetchScalarGridSpec(
        num_scalar_prefetch=3,
        grid=(n_p + n_s,),
        in_specs=[
            pl.BlockSpec((1, TOP_K, tc), lambda i, *_: (i, 0, 0), memory_space=pltpu.SMEM),
            pl.BlockSpec((tc * HM_PIECES, LANES), lambda i, *_: (jnp.minimum(i, n_p - 1), 0)),
            pl.BlockSpec((tc * HM_PIECES, LANES), lambda i, *_: (jnp.maximum(i - n_p, 0), 0)),
        ],
        out_specs=pl.BlockSpec(memory_space=pl.ANY),
        scratch_shapes=[
            pltpu.VMEM((EXPERT_TILE * HM_PIECES, LANES), U32),
            pltpu.SemaphoreType.DMA(()),
            pltpu.SemaphoreType.DMA(()),
        ],
    )
    return pl.pallas_call(
        functools.partial(_dispatch_kernel, n_s),
        out_shape=jax.ShapeDtypeStruct((n_slots * HM_PIECES, LANES), U32),
        grid_spec=grid_spec,
        compiler_params=pltpu.CompilerParams(dimension_semantics=("arbitrary",)),
        name="dispatch",
    )(last_tile, owns, n_used, dest_blocks, hm_p, hm_s)


def _pack_bf16_pairs(x):
    w = x.shape[1] // 2
    rounded = x.astype(BF16).astype(F32)
    lo = lax.bitcast_convert_type(rounded[:, :w], U32)
    hi = lax.bitcast_convert_type(rounded[:, w:], U32)
    return (hi & jnp.uint32(0xFFFF0000)) | lax.shift_right_logical(lo, jnp.uint32(16))


def _unpack_bf16_pairs(words):
    lo = lax.bitcast_convert_type(lax.shift_left(words, jnp.uint32(16)), F32)
    hi = lax.bitcast_convert_type(words & jnp.uint32(0xFFFF0000), F32)
    return lo, hi


def _expert_kernel(ord_ref, act_ref, used_ref, nact_ref, short_ref, wg_ref, wu_ref, wd_ref, xb_ref, yb_ref,
                   wgf, wuf, wdf, wgb, wub, wdb, xbuf, ybuf, sem_w, sem_in, sem_out):
    tile_rows = xbuf.shape[1]
    tm = tile_rows // HM_PIECES
    n = used_ref[0]
    n_act = nact_ref[0]

    def weight_copies(k, slot):
        e = act_ref[k]
        return [pltpu.make_async_copy(src.at[e], dst.at[slot], sem_w.at[slot])
                for src, dst in ((wg_ref, wgf), (wu_ref, wuf), (wd_ref, wdf))]

    def tile(ref, t):
        return ref.at[pl.ds(pl.multiple_of(t * tile_rows, tile_rows), tile_rows), :]

    def fetch(t, slot):
        return pltpu.make_async_copy(tile(xb_ref, t), xbuf.at[slot], sem_in.at[slot])

    def put(t, slot):
        return pltpu.make_async_copy(ybuf.at[slot], tile(yb_ref, t), sem_out.at[slot])

    for cp in weight_copies(0, 0):
        cp.start()
    for ahead in range(X_SLOTS - 1):
        @pl.when(ahead < n)
        def _(ahead=ahead):
            fetch(ahead, ahead).start()

    def body(t, carry):
        k = ord_ref[t]

        @pl.when((t == 0) | (k != ord_ref[jnp.maximum(t - 1, 0)]))
        def _():
            wslot = lax.rem(k, 2)
            for cp in weight_copies(k, wslot):
                cp.wait()
            wgb[...] = wgf[wslot].astype(BF16)
            wub[...] = wuf[wslot].astype(BF16)
            wdb[...] = wdf[wslot].astype(BF16)

            @pl.when(k + 1 < n_act)
            def _():
                for cp in weight_copies(k + 1, 1 - wslot):
                    cp.start()

        slot = lax.rem(t, X_SLOTS)
        fetch(t, slot).wait()

        @pl.when(t + X_SLOTS - 1 < n)
        def _():
            fetch(t + X_SLOTS - 1, lax.rem(t + X_SLOTS - 1, X_SLOTS)).start()

        oslot = lax.rem(t, 2)

        @pl.when(t >= 2)
        def _():
            put(t - 2, oslot).wait()

        def mlp(rows):
            pieces = [_unpack_bf16_pairs(xbuf[slot, pl.ds(p, rows, stride=HM_PIECES), :])
                      for p in range(HM_PIECES)]
            xt = jnp.concatenate([lo.astype(BF16) for lo, _ in pieces] + [hi.astype(BF16) for _, hi in pieces],
                                 axis=-1)
            g = jnp.dot(xt, wgb[...], preferred_element_type=F32)
            up = jnp.dot(xt, wub[...], preferred_element_type=F32)
            a = (g * jax.nn.sigmoid(g) * up).astype(BF16)
            packed = _pack_bf16_pairs(jnp.dot(a, wdb[...], preferred_element_type=F32))
            for p in range(HM_PIECES):
                ybuf[oslot, pl.ds(p, rows, stride=HM_PIECES), :] = packed[:, p * LANES:(p + 1) * LANES]
            if rows < tm:
                ybuf[oslot, rows * HM_PIECES:, :] = jnp.zeros(((tm - rows) * HM_PIECES, LANES), ybuf.dtype)

        @pl.when(short_ref[t] == 0)
        def _():
            mlp(tm)

        @pl.when(short_ref[t] != 0)
        def _():
            mlp(tm // 2)

        put(t, oslot).start()
        return carry

    lax.fori_loop(0, n, body, 0)

    @pl.when(n >= 2)
    def _():
        put(n - 2, lax.rem(n, 2)).wait()

    @pl.when(n >= 1)
    def _():
        put(n - 1, lax.rem(n - 1, 2)).wait()

    ybuf[0] = jnp.zeros(ybuf.shape[1:], ybuf.dtype)
    n_all = yb_ref.shape[0] // tile_rows

    def zero_tile(t):
        return pltpu.make_async_copy(ybuf.at[0], tile(yb_ref, t), sem_out.at[0])

    lax.fori_loop(n, n_all, lambda t, c: (zero_tile(t).start(), c)[1], 0)
    lax.fori_loop(n, n_all, lambda t, c: (zero_tile(t).wait(), c)[1], 0)


def _expert_call(tile_ord, active, n_used, n_active, tile_short, xb, w_eg, w_eu, w_ed):
    tile_rows = EXPERT_TILE * HM_PIECES
    any_spec = pl.BlockSpec(memory_space=pl.ANY)
    grid_spec = pltpu.PrefetchScalarGridSpec(
        num_scalar_prefetch=5,
        grid=(1,),
        in_specs=[any_spec] * 4,
        out_specs=any_spec,
        scratch_shapes=[
            pltpu.VMEM((2, D_MODEL, D_FF), F32),
            pltpu.VMEM((2, D_MODEL, D_FF), F32),
            pltpu.VMEM((2, D_FF, D_MODEL), F32),
            pltpu.VMEM((D_MODEL, D_FF), BF16),
            pltpu.VMEM((D_MODEL, D_FF), BF16),
            pltpu.VMEM((D_FF, D_MODEL), BF16),
            pltpu.VMEM((X_SLOTS, tile_rows, LANES), U32),
            pltpu.VMEM((2, tile_rows, LANES), U32),
            pltpu.SemaphoreType.DMA((2,)),
            pltpu.SemaphoreType.DMA((X_SLOTS,)),
            pltpu.SemaphoreType.DMA((2,)),
        ],
    )
    return pl.pallas_call(
        _expert_kernel,
        out_shape=jax.ShapeDtypeStruct(xb.shape, U32),
        grid_spec=grid_spec,
        compiler_params=pltpu.CompilerParams(
            dimension_semantics=("arbitrary",), vmem_limit_bytes=VMEM_LIMIT),
        name="experts",
    )(tile_ord, active, n_used, n_active, tile_short, w_eg, w_eu, w_ed, xb)


def _gather_rows(dest_ref, yb_ref, buf_ref, sem, tokens):
    for t in tokens:
        for k in range(TOP_K):
            d = dest_ref[0, k, t]
            src = yb_ref.at[pl.ds(pl.multiple_of(d * HM_PIECES, HM_PIECES), HM_PIECES), :]
            dst = buf_ref.at[k, pl.ds(t * HM_PIECES, HM_PIECES), :]
            pltpu.make_async_copy(src, dst, sem).start(priority=k)


def _wait_rows(yb_ref, buf_ref, sem):
    for k in range(TOP_K):
        pltpu.make_async_copy(yb_ref.at[pl.ds(0, buf_ref.shape[1]), :], buf_ref.at[k], sem).wait()


def _combine_kernel(dest0_ref, dest1_ref, dest2_ref, x1_ref, gates_ref, gf_ref, yb_ref, out_ref, buf, sems):
    i = pl.program_id(0)
    tc = TOKEN_TILE
    n_chunks = tc // COMBINE_CHUNK

    @pl.when(i == 0)
    def _():
        _gather_rows(dest0_ref, yb_ref, buf.at[0], sems.at[0], range(tc))

    gf = gf_ref[...]
    for half, next_dest in ((0, dest1_ref), (1, dest2_ref)):
        cur, nxt = buf.at[half], buf.at[1 - half]
        _wait_rows(yb_ref, cur, sems.at[half])
        for c in range(n_chunks):
            rows = pl.ds(half * tc + c * COMBINE_CHUNK, COMBINE_CHUNK)
            gates = gates_ref[rows, :]
            lows, highs = [], []
            for p in range(HM_PIECES):
                piece = pl.ds(c * COMBINE_CHUNK * HM_PIECES + p, COMBINE_CHUNK, stride=HM_PIECES)
                lo0, hi0 = _unpack_bf16_pairs(cur[0, piece, :])
                lo1, hi1 = _unpack_bf16_pairs(cur[1, piece, :])
                lows.append(lo0 * gates[:, 0:1] + lo1 * gates[:, 1:2])
                highs.append(hi0 * gates[:, 0:1] + hi1 * gates[:, 1:2])
            xo = x1_ref[rows, :] + jnp.concatenate(lows + highs, axis=-1)
            _gather_rows(next_dest, yb_ref, nxt, sems.at[1 - half],
                         range(c * COMBINE_CHUNK, (c + 1) * COMBINE_CHUNK))
            ms = jnp.mean(xo * xo, axis=-1, keepdims=True)
            out_ref[rows, :] = xo * lax.rsqrt(ms + EPS) * gf

    @pl.when(i == pl.num_programs(0) - 1)
    def _():
        _wait_rows(yb_ref, buf.at[0], sems.at[0])


def _combine_call(dest_blocks, x1, gates, gf, yb, name):
    n_rows = x1.shape[0]
    tc = TOKEN_TILE
    n_tiles = n_rows // tc
    assert n_tiles % 2 == 0

    def dest_spec(index_map):
        return pl.BlockSpec((1, TOP_K, tc), index_map, memory_space=pltpu.SMEM)

    return pl.pallas_call(
        _combine_kernel,
        out_shape=jax.ShapeDtypeStruct((n_rows, D_MODEL), F32),
        grid=(n_tiles // 2,),
        in_specs=[
            dest_spec(lambda i: (0, 0, 0)),
            dest_spec(lambda i: (2 * i + 1, 0, 0)),
            dest_spec(lambda i: (jnp.minimum(2 * i + 2, n_tiles - 1), 0, 0)),
            pl.BlockSpec((2 * tc, D_MODEL), lambda i: (i, 0)),
            pl.BlockSpec((2 * tc, LANES), lambda i: (i, 0)),
            pl.BlockSpec((1, D_MODEL), lambda i: (0, 0)),
            pl.BlockSpec(memory_space=pl.ANY),
        ],
        out_specs=pl.BlockSpec((2 * tc, D_MODEL), lambda i: (i, 0)),
        scratch_shapes=[pltpu.VMEM((2, TOP_K, tc * HM_PIECES, LANES), U32), pltpu.SemaphoreType.DMA((2,))],
        compiler_params=pltpu.CompilerParams(
            dimension_semantics=("arbitrary",), vmem_limit_bytes=VMEM_LIMIT),
        name=name,
    )(dest_blocks, dest_blocks, dest_blocks, x1, gates, gf, yb)


def _block_diag2(a, b):
    z = jnp.zeros_like(a)
    return jnp.concatenate([jnp.concatenate([a, z], axis=1), jnp.concatenate([z, b], axis=1)], axis=0)


def kernel(x_prompt, x_sample, state_pool, state_conv, norm1_g, w_in, pool_lin, pool_scale, conv_dw, conv_dw_b, conv_ln_g, conv_ln_b, w_out, b_out, norm2_g, w_rg, b_rg, w_re, b_re, w_eg, w_eu, w_ed, norm_f_g):
    assert norm1_g.shape[0] == 1, "single-layer trunk"
    bsz, seq, _ = x_prompt.shape
    dbsz, dseq, _ = x_sample.shape
    past_len = 1024
    n_prompt = bsz * seq
    n_sample = dbsz * dseq
    n_tokens = n_prompt + n_sample
    assert seq % PROMPT_TILE == 0 and SAMPLE_TILE % dseq == 0 and n_sample % SAMPLE_TILE == 0
    assert all(n % t == 0 for n in (n_prompt, n_sample) for t in (TOKEN_TILE, DISPATCH_TILE))

    row = lambda a: a.reshape(1, -1)
    router_w = jnp.zeros((ROUTER_ROWS, D_MODEL), F32)
    router_w = router_w.at[0:N_EXPERT_GROUPS].set(w_rg[0].T)
    router_w = router_w.at[EXPERT_ROW0:EXPERT_ROW0 + N_EXPERTS].set(
        jnp.transpose(w_re[0], (0, 2, 1)).reshape(N_EXPERTS, D_MODEL))
    router_b = jnp.zeros((ROUTER_ROWS,), F32)
    router_b = router_b.at[0:N_EXPERT_GROUPS].set(b_rg[0])
    router_b = router_b.at[EXPERT_ROW0:EXPERT_ROW0 + N_EXPERTS].set(b_re[0].reshape(-1))
    conv_w = jnp.concatenate([conv_dw[0], jnp.zeros((1, D_CONV), F32)], axis=0)
    weights = [
        row(norm1_g[0]), w_in[0].astype(BF16),
        _block_diag2(pool_lin[0, 0], pool_lin[0, 1]).astype(BF16),
        _block_diag2(pool_lin[0, 2], pool_lin[0, 3]).astype(BF16),
        row(pool_scale[0]), conv_w, row(conv_dw_b[0]), row(conv_ln_g[0]), row(conv_ln_b[0]),
        w_out[0].astype(BF16), row(b_out[0]), row(norm2_g[0]),
        router_w.astype(BF16), router_b.reshape(ROUTER_ROWS, 1),
    ]

    zpool = jnp.zeros((bsz, PAD, D_POOL), F32)
    zconv = jnp.zeros((bsz, PAD, D_CONV), F32)
    ppool = jnp.pad(state_pool[0], ((0, 0), (PAD - POOL_STATE, 0), (0, 0)))
    pconv = jnp.pad(state_conv[0], ((0, 0), (PAD - CONV_STATE, 0), (0, 0)))

    x1_p, hm_p, gates_p, route_p, cnt_p, sp_p, sc_p = _mixer_call(
        x_prompt, zpool, zconv, 0, PROMPT_TILE, PROMPT_TILE, weights, "mixer_prompt")
    x1_s, hm_s, gates_s, route_s, cnt_s, sp_s, sc_s = _mixer_call(
        x_sample.reshape(-1, SAMPLE_TILE, D_MODEL), ppool, pconv, past_len, SAMPLE_TILE, dseq, weights,
        "mixer_sample")

    tm = EXPERT_TILE
    cnt_p = cnt_p[:, 0].astype(I32)
    cnt_s = cnt_s[:, 0].astype(I32)
    padded = (cnt_p + cnt_s + tm - 1) // tm * tm
    pad_end = jnp.cumsum(padded)
    pad_start = pad_end - padded

    def dest_blocks(route, base, tile):
        eid = jnp.stack([route[:, k, :].reshape(-1) for k in range(TOP_K)])
        rank = jnp.stack([route[:, TOP_K + k, :].reshape(-1) for k in range(TOP_K)])
        experts = jnp.arange(N_EXPERTS, dtype=I32)
        dest = rank + jnp.sum(jnp.where(eid[..., None] == experts, base, 0), axis=-1)
        return dest.reshape(TOP_K, -1, tile).transpose(1, 0, 2)

    dest_p = dest_blocks(route_p, pad_start, TOKEN_TILE)
    dest_s = dest_blocks(route_s, pad_start + cnt_p, TOKEN_TILE)
    dest_all = jnp.concatenate([dest_blocks(route_p, pad_start, DISPATCH_TILE),
                                dest_blocks(route_s, pad_start + cnt_p, DISPATCH_TILE)], axis=0)
    n_tiles = -(-(n_tokens * TOP_K + N_EXPERTS * (tm - 1)) // tm)
    n_used = (pad_end[-1:] // tm).astype(I32)

    owns = padded > 0
    xb = _dispatch_call((pad_end // tm - 1).astype(I32), owns.astype(I32), n_used,
                        dest_all, hm_p, hm_s, n_tiles * tm)
    active = jnp.argsort(jnp.logical_not(owns), stable=True).astype(I32)
    tile_expert = jnp.minimum(
        jnp.sum(pad_end[None, :] <= (jnp.arange(n_tiles, dtype=I32) * tm)[:, None], axis=1), N_EXPERTS - 1)
    ordinal = jnp.cumsum(owns.astype(I32)) - 1
    tile_ord = jnp.sum(jnp.where(tile_expert[:, None] == jnp.arange(N_EXPERTS, dtype=I32), ordinal, 0),
                       axis=1).astype(I32)
    n_active = jnp.sum(owns.astype(I32)).reshape(1)
    real_end = jnp.sum(jnp.where(tile_expert[:, None] == jnp.arange(N_EXPERTS, dtype=I32),
                                 pad_start + cnt_p + cnt_s, 0), axis=1)
    tile_short = (real_end - jnp.arange(n_tiles, dtype=I32) * tm <= tm // 2).astype(I32)
    yb = _expert_call(tile_ord, active, n_used, n_active, tile_short, xb, w_eg[0], w_eu[0], w_ed[0])
    gf = row(norm_f_g)
    y_p = _combine_call(dest_p, x1_p, gates_p, gf, yb, "combine_prompt")
    y_s = _combine_call(dest_s, x1_s, gates_s, gf, yb, "combine_sample")

    return (y_p.reshape(bsz, seq, D_MODEL), y_s.reshape(dbsz, dseq, D_MODEL),
            sp_p, sc_p, sp_s, sc_s)
```

```python
import functools

import jax
import jax.numpy as jnp
from jax import lax
from jax.experimental import pallas as pl
from jax.experimental.pallas import tpu as pltpu

F32 = jnp.float32
BF16 = jnp.bfloat16
U32 = jnp.uint32
I32 = jnp.int32

D_MODEL = 1024
D_POOL = 512
D_CONV = 512
D_IN = D_POOL + 2 * D_CONV
POOL_WINDOWS = (2, 4, 8, 16)
POOL_GROUP_DIM = D_POOL // len(POOL_WINDOWS)
POOL_STATE = max(POOL_WINDOWS) - 1
CONV_WIDTH = 31
CONV_STATE = CONV_WIDTH - 1
N_EXPERT_GROUPS = 4
EXPERTS_PER_GROUP = 8
N_EXPERTS = N_EXPERT_GROUPS * EXPERTS_PER_GROUP
TOP_K = 2
D_FF = D_MODEL // 2
EPS = 1e-6

SUBLANES = 8
LANES = 128
PAD = 32
ROUTER_ROWS = LANES
EXPERT_ROW0 = SUBLANES
CONV_CHUNK = 64
CONV_STRIDE = 2
CONV_BLOCKS = D_CONV // LANES
MIXER_PART = 256
HM_WORDS = D_MODEL // 2
HM_PIECES = HM_WORDS // LANES

PROMPT_TILE = 512
SAMPLE_TILE = 512
TOKEN_TILE = 512
DISPATCH_TILE = 1024
EXPERT_TILE = 512
X_SLOTS = 4
FF_CHUNK = 256
COMBINE_CHUNK = 32
V7X_VMEM_BYTES = 64 * 1024 * 1024
VMEM_LIMIT = V7X_VMEM_BYTES * 3 // 4


def _iota_f32(shape, axis):
    return lax.broadcasted_iota(I32, shape, axis).astype(F32)


def _mixer_kernel(pos0, tt, n_t, seg,
                  x_ref, pp_ref, cp_ref, g1_ref, win_ref, bd01_ref, bd23_ref, pscale_ref,
                  cw_ref, cb_ref, lng_ref, lnb_ref, wout_ref, bout_ref, g2_ref, wrt_ref, br_ref,
                  tri_ref,
                  x1_ref, hm_ref, gates_ref, route_ref, cnt_ref, spool_ref, sconv_ref,
                  pext, cext, hbuf, carry, lv2, lv4, lv8):
    b = pl.program_id(0)
    j = pl.program_id(1)

    @pl.when((b == 0) & (j == 0))
    def _():
        carry[...] = jnp.zeros_like(carry)

    n_seg = tt // seg
    seg_stride = PAD + seg
    assert n_seg == 1 or n_t == 1, "several segments per tile: each one is a whole sequence"

    @pl.when(j == 0)
    def _():
        for g in range(n_seg):
            pext[g * seg_stride:g * seg_stride + PAD, :] = pp_ref[g]
            for lb in range(CONV_BLOCKS):
                cext[lb, g * seg_stride:g * seg_stride + PAD, :] = cp_ref[g, :, lb * LANES:(lb + 1) * LANES]

    @pl.when(j > 0)
    def _():
        pext[0:PAD, :] = pext[tt:tt + PAD, :]
        cext[:, 0:PAD, :] = cext[:, tt:tt + PAD, :]

    n_parts = max(1, tt // MIXER_PART)
    rp = tt // n_parts
    gd = POOL_GROUP_DIM
    neg = jnp.float32(-jnp.inf)

    def runs_of(part):
        r0 = part * rp
        if n_seg == 1:
            return [(r0, rp, r0)]
        return [(t0, seg, t0 // seg * seg_stride) for t0 in range(r0, r0 + rp, seg)]

    def stage_in(part):
        r0 = part * rp
        x = x_ref[r0:r0 + rp, :]
        ms = jnp.mean(x * x, axis=-1, keepdims=True)
        h = (x * lax.rsqrt(ms + EPS) * g1_ref[...]).astype(BF16)
        z = jnp.dot(h, win_ref[...], preferred_element_type=F32)
        v = z[:, D_POOL:D_POOL + D_CONV] * jax.nn.sigmoid(z[:, D_POOL + D_CONV:])
        for t0, n, v0 in runs_of(part):
            pext[PAD + v0:PAD + v0 + n, :] = z[t0 - r0:t0 - r0 + n, :D_POOL]
            for lb in range(CONV_BLOCKS):
                cext[lb, PAD + v0:PAD + v0 + n, :] = v[t0 - r0:t0 - r0 + n, lb * LANES:(lb + 1) * LANES]

    def stage_mix(part):
        return jnp.concatenate([mix_run(*run) for run in runs_of(part)], axis=0)

    def mix_run(t0, rp, r0):
        first_pos = pos0 + (j * tt + t0 if n_seg == 1 else 0)
        pos = first_pos + lax.broadcasted_iota(I32, (rp, 1), 0)
        all_full = pos0 + (t0 if n_seg == 1 else 0) >= POOL_STATE
        top = r0 + PAD
        end = top + rp
        lv2[r0 + 8:end, :] = pext[r0 + 8:end, :] + pext[r0 + 7:end - 1, :]
        lv4[r0 + 16:end, gd:] = lv2[r0 + 16:end, gd:] + lv2[r0 + 14:end - 2, gd:]
        lv8[r0 + 24:end, 2 * gd:] = lv4[r0 + 24:end, 2 * gd:] + lv4[r0 + 20:end - 4, 2 * gd:]
        sums = [lv2[top:end, 0:gd], lv4[top:end, gd:2 * gd], lv8[top:end, 2 * gd:3 * gd],
                lv8[top:end, 3 * gd:] + lv8[top - 8:end - 8, 3 * gd:]]
        u = pext[top:end, :]
        parts = []
        for g, w in enumerate(POOL_WINDOWS):
            sl = slice(g * gd, (g + 1) * gd)
            cnt = float(w) if all_full else jnp.minimum(pos + 1, w).astype(F32)
            parts.append(sums[g] / cnt - u[:, sl])
        d = jnp.concatenate(parts, axis=-1).astype(BF16)
        half = D_POOL // 2
        yp = jnp.concatenate(
            [jnp.dot(d[:, :half], bd01_ref[...], preferred_element_type=F32),
             jnp.dot(d[:, half:], bd23_ref[...], preferred_element_type=F32)], axis=-1)
        yp = yp * pscale_ref[...]

        chunk = min(CONV_CHUNK, rp // CONV_STRIDE)
        for lb in range(CONV_BLOCKS):
            lanes = slice(lb * LANES, (lb + 1) * LANES)
            wts = [jnp.broadcast_to(cw_ref[k:k + 1, lanes], (chunk, LANES)) for k in range(CONV_WIDTH)]
            for first in range(r0, r0 + rp, CONV_STRIDE * chunk):
                for phase in range(CONV_STRIDE):
                    acc = None
                    for k in range(CONV_WIDTH):
                        start = first + phase + PAD - CONV_STATE + k
                        tap = cext[lb, pl.ds(start, chunk, stride=CONV_STRIDE), :] * wts[k]
                        acc = tap if acc is None else acc + tap
                    hbuf[lb, pl.ds(first + phase, chunk, stride=CONV_STRIDE), :] = acc
        return yp

    def stage_out(part, yp):
        r0 = part * rp
        x = x_ref[r0:r0 + rp, :]
        hc = jnp.concatenate(
            [jnp.concatenate([hbuf[lb, v0:v0 + n, :] for _, n, v0 in runs_of(part)], axis=0)
             for lb in range(CONV_BLOCKS)], axis=-1) + cb_ref[...]
        mu = jnp.mean(hc, axis=-1, keepdims=True)
        var = jnp.mean(jnp.square(hc - mu), axis=-1, keepdims=True)
        yln = (hc - mu) * lax.rsqrt(var + EPS) * lng_ref[...] + lnb_ref[...]
        yc = yln * jax.nn.sigmoid(yln)

        o = (jnp.dot(yp.astype(BF16), wout_ref[0:D_POOL, :], preferred_element_type=F32)
             + jnp.dot(yc.astype(BF16), wout_ref[D_POOL:, :], preferred_element_type=F32))
        x1 = x + (o + bout_ref[...])
        x1_ref[r0:r0 + rp, :] = x1

        ms2 = jnp.mean(x1 * x1, axis=-1, keepdims=True)
        hmb = (x1 * lax.rsqrt(ms2 + EPS) * g2_ref[...]).astype(BF16)
        packed = _pack_bf16_pairs(hmb.astype(F32))
        for p in range(HM_PIECES):
            hm_ref[pl.ds(r0 * HM_PIECES + p, rp, stride=HM_PIECES), :] = packed[:, p * LANES:(p + 1) * LANES]

        lt = lax.dot_general(wrt_ref[...], hmb, (((1,), (1,)), ((), ())),
                             preferred_element_type=F32) + br_ref[...]
        row8 = _iota_f32((SUBLANES, rp), 0)
        lg = jnp.where(row8 < N_EXPERT_GROUPS, lt[0:SUBLANES, :], neg)
        mg = jnp.max(lg, axis=0, keepdims=True)
        p_grp = 1.0 / jnp.sum(jnp.exp(lg - mg), axis=0, keepdims=True)
        gsel = jnp.min(jnp.where(lg == mg, row8, float(SUBLANES)), axis=0, keepdims=True)
        le = jnp.zeros((SUBLANES, rp), F32)
        for g in range(N_EXPERT_GROUPS):
            e0 = EXPERT_ROW0 + g * EXPERTS_PER_GROUP
            le = jnp.where(gsel == float(g), lt[e0:e0 + EXPERTS_PER_GROUP, :], le)
        m1 = jnp.max(le, axis=0, keepdims=True)
        i1 = jnp.min(jnp.where(le == m1, row8, float(SUBLANES)), axis=0, keepdims=True)
        le2 = jnp.where(row8 == i1, neg, le)
        m2 = jnp.max(le2, axis=0, keepdims=True)
        i2 = jnp.min(jnp.where(le2 == m2, row8, float(SUBLANES)), axis=0, keepdims=True)
        e2 = jnp.exp(m2 - m1)
        den = 1.0 + e2
        gate0 = p_grp * (1.0 / den)
        gate1 = p_grp * (e2 / den)
        eid0 = gsel * float(EXPERTS_PER_GROUP) + i1
        eid1 = gsel * float(EXPERTS_PER_GROUP) + i2

        rowe = _iota_f32((N_EXPERTS, rp), 0)
        oh0 = rowe == eid0
        oh1 = rowe == eid1
        ohf = jnp.where(jnp.logical_or(oh0, oh1), 1.0, 0.0)
        before = jnp.dot(ohf.astype(BF16), tri_ref[0:rp, 0:rp], preferred_element_type=F32) + carry[:, 0:1]
        rank0 = jnp.sum(jnp.where(oh0, before, 0.0), axis=0, keepdims=True)
        rank1 = jnp.sum(jnp.where(oh1, before, 0.0), axis=0, keepdims=True)
        carry[...] = carry[...] + jnp.sum(ohf, axis=1, keepdims=True)

        route = jnp.where(row8 == 0.0, eid0,
                          jnp.where(row8 == 1.0, eid1,
                                    jnp.where(row8 == 2.0, rank0,
                                              jnp.where(row8 == 3.0, rank1, 0.0))))
        route_ref[:, r0:r0 + rp] = route.astype(I32)

        tpad = -(-rp // LANES) * LANES
        rowr = _iota_f32((ROUTER_ROWS, rp), 0)
        gt = jnp.where(rowr == 0.0, gate0, jnp.where(rowr == 1.0, gate1, 0.0))
        if tpad != rp:
            gt = jnp.concatenate([gt, jnp.zeros((ROUTER_ROWS, tpad - rp), F32)], axis=1)
        gates_ref[r0:r0 + rp, :] = gt.T[0:rp, :]

    stage_in(0)
    for part in range(n_parts):
        yp = stage_mix(part)
        if part + 1 < n_parts:
            stage_in(part + 1)
        stage_out(part, yp)
    cnt_ref[...] = carry[...]

    @pl.when(j == n_t - 1)
    def _():
        for g in range(n_seg):
            last = g * seg_stride + PAD + seg
            spool_ref[g] = pext[last - POOL_STATE:last, :]
            sconv_ref[g] = jnp.concatenate(
                [cext[lb, last - CONV_STATE:last, :] for lb in range(CONV_BLOCKS)], axis=-1)


def _mixer_call(x, pool_prefix, conv_prefix, pos0, tt, seg, weights, name):
    bsz, seq, _ = x.shape
    n_t = seq // tt
    n_seg = tt // seg
    ext_rows = n_seg * (PAD + seg)
    n_blk = bsz * n_t
    n_rows = bsz * seq
    tri = (lax.broadcasted_iota(I32, (tt, tt), 0) < lax.broadcasted_iota(I32, (tt, tt), 1)).astype(BF16)

    def const(shape):
        return pl.BlockSpec(shape, lambda b, j: (0,) * len(shape))

    def rows(width):
        return pl.BlockSpec((tt, width), lambda b, j: (b * n_t + j, 0))

    in_specs = [
        pl.BlockSpec((None, tt, D_MODEL), lambda b, j: (b, j, 0)),
        pl.BlockSpec((n_seg, PAD, D_POOL), lambda b, j: (b, 0, 0)),
        pl.BlockSpec((n_seg, PAD, D_CONV), lambda b, j: (b, 0, 0)),
    ] + [const(w.shape) for w in weights] + [const((tt, tt))]
    out_shape = (
        jax.ShapeDtypeStruct((n_rows, D_MODEL), F32),
        jax.ShapeDtypeStruct((n_rows * HM_PIECES, LANES), U32),
        jax.ShapeDtypeStruct((n_rows, LANES), F32),
        jax.ShapeDtypeStruct((n_blk, SUBLANES, tt), I32),
        jax.ShapeDtypeStruct((N_EXPERTS, LANES), F32),
        jax.ShapeDtypeStruct((1, bsz * n_seg, POOL_STATE, D_POOL), F32),
        jax.ShapeDtypeStruct((1, bsz * n_seg, CONV_STATE, D_CONV), F32),
    )
    out_specs = (
        rows(D_MODEL),
        pl.BlockSpec((tt * HM_PIECES, LANES), lambda b, j: (b * n_t + j, 0)),
        rows(LANES),
        pl.BlockSpec((None, SUBLANES, tt), lambda b, j: (b * n_t + j, 0, 0)),
        pl.BlockSpec((N_EXPERTS, LANES), lambda b, j: (0, 0)),
        pl.BlockSpec((None, n_seg, POOL_STATE, D_POOL), lambda b, j: (0, b, 0, 0)),
        pl.BlockSpec((None, n_seg, CONV_STATE, D_CONV), lambda b, j: (0, b, 0, 0)),
    )
    return pl.pallas_call(
        functools.partial(_mixer_kernel, pos0, tt, n_t, seg),
        out_shape=out_shape,
        grid=(bsz, n_t),
        in_specs=in_specs,
        out_specs=out_specs,
        scratch_shapes=[
            pltpu.VMEM((ext_rows, D_POOL), F32),
            pltpu.VMEM((CONV_BLOCKS, ext_rows, LANES), F32),
            pltpu.VMEM((CONV_BLOCKS, ext_rows, LANES), F32),
            pltpu.VMEM((N_EXPERTS, LANES), F32),
            pltpu.VMEM((ext_rows, D_POOL), F32),
            pltpu.VMEM((ext_rows, D_POOL), F32),
            pltpu.VMEM((ext_rows, D_POOL), F32),
        ],
        compiler_params=pltpu.CompilerParams(
            dimension_semantics=("arbitrary", "arbitrary"), vmem_limit_bytes=VMEM_LIMIT),
        name=name,
    )(x, pool_prefix, conv_prefix, *weights, tri)


def _dispatch_kernel(n_sample_tiles, last_ref, owns_ref, used_ref, dest_ref, hmp_ref, hms_ref, xb_ref,
                     zbuf, sem, zsem):
    i = pl.program_id(0)
    tc = dest_ref.shape[2]
    n_prompt_tiles = pl.num_programs(0) - n_sample_tiles
    tile_rows = zbuf.shape[0]
    n_all = xb_ref.shape[0] // tile_rows

    def zero_copy(t):
        dst = xb_ref.at[pl.ds(pl.multiple_of(t * tile_rows, tile_rows), tile_rows), :]
        return pltpu.make_async_copy(zbuf, dst, zsem)

    @pl.when(i == 0)
    def _():
        zbuf[...] = jnp.zeros_like(zbuf)
        for e in range(N_EXPERTS):
            @pl.when(owns_ref[e] > 0)
            def _(e=e):
                zero_copy(last_ref[e]).start()
        lax.fori_loop(used_ref[0], n_all, lambda t, c: (zero_copy(t).start(), c)[1], 0)
        for e in range(N_EXPERTS):
            @pl.when(owns_ref[e] > 0)
            def _(e=e):
                zero_copy(last_ref[e]).wait()
        lax.fori_loop(used_ref[0], n_all, lambda t, c: (zero_copy(t).wait(), c)[1], 0)

    def scatter(hm_ref):
        for t in range(tc):
            for k in range(TOP_K):
                d = dest_ref[0, k, t]
                src = hm_ref.at[pl.ds(t * HM_PIECES, HM_PIECES), :]
                dst = xb_ref.at[pl.ds(pl.multiple_of(d * HM_PIECES, HM_PIECES), HM_PIECES), :]
                pltpu.make_async_copy(src, dst, sem).start(priority=k)
        for k in range(TOP_K):
            pltpu.make_async_copy(hm_ref, xb_ref.at[pl.ds(0, tc * HM_PIECES), :], sem).wait()

    @pl.when(i < n_prompt_tiles)
    def _():
        scatter(hmp_ref)

    @pl.when(i >= n_prompt_tiles)
    def _():
        scatter(hms_ref)


def _dispatch_call(last_tile, owns, n_used, dest_blocks, hm_p, hm_s, n_slots):
    tc = DISPATCH_TILE
    n_p = hm_p.shape[0] // (tc * HM_PIECES)
    n_s = hm_s.shape[0] // (tc * HM_PIECES)
    grid_spec = pltpu.PrefetchScalarGridSpec(
        num_scalar_prefetch=3,
        grid=(n_p + n_s,),
        in_specs=[
            pl.BlockSpec((1, TOP_K, tc), lambda i, *_: (i, 0, 0), memory_space=pltpu.SMEM),
            pl.BlockSpec((tc * HM_PIECES, LANES), lambda i, *_: (jnp.minimum(i, n_p - 1), 0)),
            pl.BlockSpec((tc * HM_PIECES, LANES), lambda i, *_: (jnp.maximum(i - n_p, 0), 0)),
        ],
        out_specs=pl.BlockSpec(memory_space=pl.ANY),
        scratch_shapes=[
            pltpu.VMEM((EXPERT_TILE * HM_PIECES, LANES), U32),
            pltpu.SemaphoreType.DMA(()),
            pltpu.SemaphoreType.DMA(()),
        ],
    )
    return pl.pallas_call(
        functools.partial(_dispatch_kernel, n_s),
        out_shape=jax.ShapeDtypeStruct((n_slots * HM_PIECES, LANES), U32),
        grid_spec=grid_spec,
        compiler_params=pltpu.CompilerParams(dimension_semantics=("arbitrary",)),
        name="dispatch",
    )(last_tile, owns, n_used, dest_blocks, hm_p, hm_s)


def _pack_bf16_pairs(x):
    w = x.shape[1] // 2
    rounded = x.astype(BF16).astype(F32)
    lo = lax.bitcast_convert_type(rounded[:, :w], U32)
    hi = lax.bitcast_convert_type(rounded[:, w:], U32)
    return (hi & jnp.uint32(0xFFFF0000)) | lax.shift_right_logical(lo, jnp.uint32(16))


def _unpack_bf16_pairs(words):
    lo = lax.bitcast_convert_type(lax.shift_left(words, jnp.uint32(16)), F32)
    hi = lax.bitcast_convert_type(words & jnp.uint32(0xFFFF0000), F32)
    return lo, hi


def _expert_kernel(ord_ref, act_ref, used_ref, nact_ref, short_ref, wg_ref, wu_ref, wd_ref, xb_ref, yb_ref,
                   wgf, wuf, wdf, wgb, wub, wdb, xbuf, ybuf, sem_w, sem_in, sem_out):
    tile_rows = xbuf.shape[1]
    tm = tile_rows // HM_PIECES
    n = used_ref[0]
    n_act = nact_ref[0]

    def weight_copies(k, slot):
        e = act_ref[k]
        return [pltpu.make_async_copy(src.at[e], dst.at[slot], sem_w.at[slot])
                for src, dst in ((wg_ref, wgf), (wu_ref, wuf), (wd_ref, wdf))]

    def tile(ref, t):
        return ref.at[pl.ds(pl.multiple_of(t * tile_rows, tile_rows), tile_rows), :]

    def fetch(t, slot):
        return pltpu.make_async_copy(tile(xb_ref, t), xbuf.at[slot], sem_in.at[slot])

    def put(t, slot):
        return pltpu.make_async_copy(ybuf.at[slot], tile(yb_ref, t), sem_out.at[slot])

    for cp in weight_copies(0, 0):
        cp.start()
    for ahead in range(X_SLOTS - 1):
        @pl.when(ahead < n)
        def _(ahead=ahead):
            fetch(ahead, ahead).start()

    def body(t, carry):
        k = ord_ref[t]

        @pl.when((t == 0) | (k != ord_ref[jnp.maximum(t - 1, 0)]))
        def _():
            wslot = lax.rem(k, 2)
            for cp in weight_copies(k, wslot):
                cp.wait()
            wgb[...] = wgf[wslot].astype(BF16)
            wub[...] = wuf[wslot].astype(BF16)
            wdb[...] = wdf[wslot].astype(BF16)

            @pl.when(k + 1 < n_act)
            def _():
                for cp in weight_copies(k + 1, 1 - wslot):
                    cp.start()

        slot = lax.rem(t, X_SLOTS)
        fetch(t, slot).wait()

        @pl.when(t + X_SLOTS - 1 < n)
        def _():
            fetch(t + X_SLOTS - 1, lax.rem(t + X_SLOTS - 1, X_SLOTS)).start()

        oslot = lax.rem(t, 2)

        @pl.when(t >= 2)
        def _():
            put(t - 2, oslot).wait()

        def mlp(rows):
            pieces = [_unpack_bf16_pairs(xbuf[slot, pl.ds(p, rows, stride=HM_PIECES), :])
                      for p in range(HM_PIECES)]
            xt = jnp.concatenate([lo.astype(BF16) for lo, _ in pieces] + [hi.astype(BF16) for _, hi in pieces],
                                 axis=-1)
            y = None
            for c0 in range(0, D_FF, FF_CHUNK):
                cols = slice(c0, c0 + FF_CHUNK)
                g = jnp.dot(xt, wgb[:, cols], preferred_element_type=F32)
                up = jnp.dot(xt, wub[:, cols], preferred_element_type=F32)
                a = (g * jax.nn.sigmoid(g) * up).astype(BF16)
                part = jnp.dot(a, wdb[cols, :], preferred_element_type=F32)
                y = part if y is None else y + part
            packed = _pack_bf16_pairs(y)
            for p in range(HM_PIECES):
                ybuf[oslot, pl.ds(p, rows, stride=HM_PIECES), :] = packed[:, p * LANES:(p + 1) * LANES]
            if rows < tm:
                ybuf[oslot, rows * HM_PIECES:, :] = jnp.zeros(((tm - rows) * HM_PIECES, LANES), ybuf.dtype)

        @pl.when(short_ref[t] == 0)
        def _():
            mlp(tm)

        @pl.when(short_ref[t] != 0)
        def _():
            mlp(tm // 2)

        put(t, oslot).start()
        return carry

    lax.fori_loop(0, n, body, 0)

    @pl.when(n >= 2)
    def _():
        put(n - 2, lax.rem(n, 2)).wait()

    @pl.when(n >= 1)
    def _():
        put(n - 1, lax.rem(n - 1, 2)).wait()

    ybuf[0] = jnp.zeros(ybuf.shape[1:], ybuf.dtype)
    n_all = yb_ref.shape[0] // tile_rows

    def zero_tile(t):
        return pltpu.make_async_copy(ybuf.at[0], tile(yb_ref, t), sem_out.at[0])

    lax.fori_loop(n, n_all, lambda t, c: (zero_tile(t).start(), c)[1], 0)
    lax.fori_loop(n, n_all, lambda t, c: (zero_tile(t).wait(), c)[1], 0)


def _expert_call(tile_ord, active, n_used, n_active, tile_short, xb, w_eg, w_eu, w_ed):
    tile_rows = EXPERT_TILE * HM_PIECES
    any_spec = pl.BlockSpec(memory_space=pl.ANY)
    grid_spec = pltpu.PrefetchScalarGridSpec(
        num_scalar_prefetch=5,
        grid=(1,),
        in_specs=[any_spec] * 4,
        out_specs=any_spec,
        scratch_shapes=[
            pltpu.VMEM((2, D_MODEL, D_FF), F32),
            pltpu.VMEM((2, D_MODEL, D_FF), F32),
            pltpu.VMEM((2, D_FF, D_MODEL), F32),
            pltpu.VMEM((D_MODEL, D_FF), BF16),
            pltpu.VMEM((D_MODEL, D_FF), BF16),
            pltpu.VMEM((D_FF, D_MODEL), BF16),
            pltpu.VMEM((X_SLOTS, tile_rows, LANES), U32),
            pltpu.VMEM((2, tile_rows, LANES), U32),
            pltpu.SemaphoreType.DMA((2,)),
            pltpu.SemaphoreType.DMA((X_SLOTS,)),
            pltpu.SemaphoreType.DMA((2,)),
        ],
    )
    return pl.pallas_call(
        _expert_kernel,
        out_shape=jax.ShapeDtypeStruct(xb.shape, U32),
        grid_spec=grid_spec,
        compiler_params=pltpu.CompilerParams(
            dimension_semantics=("arbitrary",), vmem_limit_bytes=VMEM_LIMIT),
        name="experts",
    )(tile_ord, active, n_used, n_active, tile_short, w_eg, w_eu, w_ed, xb)


def _gather_rows(dest_ref, yb_ref, buf_ref, sem, tokens):
    for t in tokens:
        for k in range(TOP_K):
            d = dest_ref[0, k, t]
            src = yb_ref.at[pl.ds(pl.multiple_of(d * HM_PIECES, HM_PIECES), HM_PIECES), :]
            dst = buf_ref.at[k, pl.ds(t * HM_PIECES, HM_PIECES), :]
            pltpu.make_async_copy(src, dst, sem).start(priority=k)


def _wait_rows(yb_ref, buf_ref, sem):
    for k in range(TOP_K):
        pltpu.make_async_copy(yb_ref.at[pl.ds(0, buf_ref.shape[1]), :], buf_ref.at[k], sem).wait()


def _combine_kernel(dest0_ref, dest1_ref, dest2_ref, x1_ref, gates_ref, gf_ref, yb_ref, out_ref, buf, sems):
    i = pl.program_id(0)
    tc = TOKEN_TILE
    n_chunks = tc // COMBINE_CHUNK

    @pl.when(i == 0)
    def _():
        _gather_rows(dest0_ref, yb_ref, buf.at[0], sems.at[0], range(tc))

    gf = gf_ref[...]
    for half, next_dest in ((0, dest1_ref), (1, dest2_ref)):
        cur, nxt = buf.at[half], buf.at[1 - half]
        _wait_rows(yb_ref, cur, sems.at[half])
        for c in range(n_chunks):
            rows = pl.ds(half * tc + c * COMBINE_CHUNK, COMBINE_CHUNK)
            gates = gates_ref[rows, :]
            lows, highs = [], []
            for p in range(HM_PIECES):
                piece = pl.ds(c * COMBINE_CHUNK * HM_PIECES + p, COMBINE_CHUNK, stride=HM_PIECES)
                lo0, hi0 = _unpack_bf16_pairs(cur[0, piece, :])
                lo1, hi1 = _unpack_bf16_pairs(cur[1, piece, :])
                lows.append(lo0 * gates[:, 0:1] + lo1 * gates[:, 1:2])
                highs.append(hi0 * gates[:, 0:1] + hi1 * gates[:, 1:2])
            xo = x1_ref[rows, :] + jnp.concatenate(lows + highs, axis=-1)
            _gather_rows(next_dest, yb_ref, nxt, sems.at[1 - half],
                         range(c * COMBINE_CHUNK, (c + 1) * COMBINE_CHUNK))
            ms = jnp.mean(xo * xo, axis=-1, keepdims=True)
            out_ref[rows, :] = xo * lax.rsqrt(ms + EPS) * gf

    @pl.when(i == pl.num_programs(0) - 1)
    def _():
        _wait_rows(yb_ref, buf.at[0], sems.at[0])


def _combine_call(dest_blocks, x1, gates, gf, yb, name):
    n_rows = x1.shape[0]
    tc = TOKEN_TILE
    n_tiles = n_rows // tc
    assert n_tiles % 2 == 0

    def dest_spec(index_map):
        return pl.BlockSpec((1, TOP_K, tc), index_map, memory_space=pltpu.SMEM)

    return pl.pallas_call(
        _combine_kernel,
        out_shape=jax.ShapeDtypeStruct((n_rows, D_MODEL), F32),
        grid=(n_tiles // 2,),
        in_specs=[
            dest_spec(lambda i: (0, 0, 0)),
            dest_spec(lambda i: (2 * i + 1, 0, 0)),
            dest_spec(lambda i: (jnp.minimum(2 * i + 2, n_tiles - 1), 0, 0)),
            pl.BlockSpec((2 * tc, D_MODEL), lambda i: (i, 0)),
            pl.BlockSpec((2 * tc, LANES), lambda i: (i, 0)),
            pl.BlockSpec((1, D_MODEL), lambda i: (0, 0)),
            pl.BlockSpec(memory_space=pl.ANY),
        ],
        out_specs=pl.BlockSpec((2 * tc, D_MODEL), lambda i: (i, 0)),
        scratch_shapes=[pltpu.VMEM((2, TOP_K, tc * HM_PIECES, LANES), U32), pltpu.SemaphoreType.DMA((2,))],
        compiler_params=pltpu.CompilerParams(
            dimension_semantics=("arbitrary",), vmem_limit_bytes=VMEM_LIMIT),
        name=name,
    )(dest_blocks, dest_blocks, dest_blocks, x1, gates, gf, yb)


def _block_diag2(a, b):
    z = jnp.zeros_like(a)
    return jnp.concatenate([jnp.concatenate([a, z], axis=1), jnp.concatenate([z, b], axis=1)], axis=0)


def kernel(x_prompt, x_sample, state_pool, state_conv, norm1_g, w_in, pool_lin, pool_scale, conv_dw, conv_dw_b, conv_ln_g, conv_ln_b, w_out, b_out, norm2_g, w_rg, b_rg, w_re, b_re, w_eg, w_eu, w_ed, norm_f_g):
    assert norm1_g.shape[0] == 1, "single-layer trunk"
    bsz, seq, _ = x_prompt.shape
    dbsz, dseq, _ = x_sample.shape
    past_len = 1024
    n_prompt = bsz * seq
    n_sample = dbsz * dseq
    n_tokens = n_prompt + n_sample
    assert seq % PROMPT_TILE == 0 and SAMPLE_TILE % dseq == 0 and n_sample % SAMPLE_TILE == 0
    assert all(n % t == 0 for n in (n_prompt, n_sample) for t in (TOKEN_TILE, DISPATCH_TILE))

    row = lambda a: a.reshape(1, -1)
    router_w = jnp.zeros((ROUTER_ROWS, D_MODEL), F32)
    router_w = router_w.at[0:N_EXPERT_GROUPS].set(w_rg[0].T)
    router_w = router_w.at[EXPERT_ROW0:EXPERT_ROW0 + N_EXPERTS].set(
        jnp.transpose(w_re[0], (0, 2, 1)).reshape(N_EXPERTS, D_MODEL))
    router_b = jnp.zeros((ROUTER_ROWS,), F32)
    router_b = router_b.at[0:N_EXPERT_GROUPS].set(b_rg[0])
    router_b = router_b.at[EXPERT_ROW0:EXPERT_ROW0 + N_EXPERTS].set(b_re[0].reshape(-1))
    conv_w = jnp.concatenate([conv_dw[0], jnp.zeros((1, D_CONV), F32)], axis=0)
    weights = [
        row(norm1_g[0]), w_in[0].astype(BF16),
        _block_diag2(pool_lin[0, 0], pool_lin[0, 1]).astype(BF16),
        _block_diag2(pool_lin[0, 2], pool_lin[0, 3]).astype(BF16),
        row(pool_scale[0]), conv_w, row(conv_dw_b[0]), row(conv_ln_g[0]), row(conv_ln_b[0]),
        w_out[0].astype(BF16), row(b_out[0]), row(norm2_g[0]),
        router_w.astype(BF16), router_b.reshape(ROUTER_ROWS, 1),
    ]

    zpool = jnp.zeros((bsz, PAD, D_POOL), F32)
    zconv = jnp.zeros((bsz, PAD, D_CONV), F32)
    ppool = jnp.pad(state_pool[0], ((0, 0), (PAD - POOL_STATE, 0), (0, 0)))
    pconv = jnp.pad(state_conv[0], ((0, 0), (PAD - CONV_STATE, 0), (0, 0)))

    x1_p, hm_p, gates_p, route_p, cnt_p, sp_p, sc_p = _mixer_call(
        x_prompt, zpool, zconv, 0, PROMPT_TILE, PROMPT_TILE, weights, "mixer_prompt")
    x1_s, hm_s, gates_s, route_s, cnt_s, sp_s, sc_s = _mixer_call(
        x_sample.reshape(-1, SAMPLE_TILE, D_MODEL), ppool, pconv, past_len, SAMPLE_TILE, dseq, weights,
        "mixer_sample")

    tm = EXPERT_TILE
    cnt_p = cnt_p[:, 0].astype(I32)
    cnt_s = cnt_s[:, 0].astype(I32)
    padded = (cnt_p + cnt_s + tm - 1) // tm * tm
    pad_end = jnp.cumsum(padded)
    pad_start = pad_end - padded

    def dest_blocks(route, base, tile):
        eid = jnp.stack([route[:, k, :].reshape(-1) for k in range(TOP_K)])
        rank = jnp.stack([route[:, TOP_K + k, :].reshape(-1) for k in range(TOP_K)])
        experts = jnp.arange(N_EXPERTS, dtype=I32)
        dest = rank + jnp.sum(jnp.where(eid[..., None] == experts, base, 0), axis=-1)
        return dest.reshape(TOP_K, -1, tile).transpose(1, 0, 2)

    dest_p = dest_blocks(route_p, pad_start, TOKEN_TILE)
    dest_s = dest_blocks(route_s, pad_start + cnt_p, TOKEN_TILE)
    dest_all = jnp.concatenate([dest_blocks(route_p, pad_start, DISPATCH_TILE),
                                dest_blocks(route_s, pad_start + cnt_p, DISPATCH_TILE)], axis=0)
    n_tiles = -(-(n_tokens * TOP_K + N_EXPERTS * (tm - 1)) // tm)
    n_used = (pad_end[-1:] // tm).astype(I32)

    owns = padded > 0
    xb = _dispatch_call((pad_end // tm - 1).astype(I32), owns.astype(I32), n_used,
                        dest_all, hm_p, hm_s, n_tiles * tm)
    active = jnp.argsort(jnp.logical_not(owns), stable=True).astype(I32)
    tile_expert = jnp.minimum(
        jnp.sum(pad_end[None, :] <= (jnp.arange(n_tiles, dtype=I32) * tm)[:, None], axis=1), N_EXPERTS - 1)
    ordinal = jnp.cumsum(owns.astype(I32)) - 1
    tile_ord = jnp.sum(jnp.where(tile_expert[:, None] == jnp.arange(N_EXPERTS, dtype=I32), ordinal, 0),
                       axis=1).astype(I32)
    n_active = jnp.sum(owns.astype(I32)).reshape(1)
    real_end = jnp.sum(jnp.where(tile_expert[:, None] == jnp.arange(N_EXPERTS, dtype=I32),
                                 pad_start + cnt_p + cnt_s, 0), axis=1)
    tile_short = (real_end - jnp.arange(n_tiles, dtype=I32) * tm <= tm // 2).astype(I32)
    yb = _expert_call(tile_ord, active, n_used, n_active, tile_short, xb, w_eg[0], w_eu[0], w_ed[0])
    gf = row(norm_f_g)
    y_p = _combine_call(dest_p, x1_p, gates_p, gf, yb, "combine_prompt")
    y_s = _combine_call(dest_s, x1_s, gates_s, gf, yb, "combine_sample")

    return (y_p.reshape(bsz, seq, D_MODEL), y_s.reshape(dbsz, dseq, D_MODEL),
            sp_p, sc_p, sp_s, sc_s)
```
